```python
import math
import jax, jax.numpy as jnp
from jax import lax
import numpy as np

D_MODEL = 2048
BATCH = 4
SEQ = 4096
DEPTH = 4

GRID_W = 64
CTX_LEN = 256
N_MIXERS = 3
N_A = (DEPTH + 2) // 3
N_B = (DEPTH + 1) // 3
N_C = DEPTH // 3

A_HEADS = 16
A_KV_HEADS = 4
A_GROUP = A_HEADS // A_KV_HEADS
A_HEAD_DIM = D_MODEL // A_HEADS

LRU_WIDTH = D_MODEL
LRU_BLOCKS = 8
LRU_BLOCK = LRU_WIDTH // LRU_BLOCKS
CONV_W = 4
LRU_C = 8.0

MLA_HEADS = 16
MLA_Q_LORA = 512
MLA_KV_LORA = 512
MLA_QK_NOPE = 128
MLA_QK_ROPE = 64
MLA_V_DIM = 128

D_FF = 4 * D_MODEL

Q_BLOCK = 128
ROPE_THETA = 10000.0
EPS = 1e-6
DN_ALPHA = (2 * DEPTH) ** 0.25
DN_BETA = (8 * DEPTH) ** -0.25

kernel_name = 'hybrid_gqa_rglru_mla_diffusion_trunk'


def layer_norm(x, g, b):
    xf = x.astype(jnp.float32)
    mu = jnp.mean(xf, -1, keepdims=True)
    var = jnp.mean(jnp.square(xf - mu), -1, keepdims=True)
    return ((xf - mu) * lax.rsqrt(var + EPS) * g + b).astype(x.dtype)


def rms_norm(x, g):
    xf = x.astype(jnp.float32)
    return (xf * lax.rsqrt(jnp.mean(jnp.square(xf), -1, keepdims=True) + EPS) * g).astype(x.dtype)


def axial_tables(rows, rot_dim):
    row = jnp.repeat(jnp.arange(rows), GRID_W).astype(jnp.float32)
    col = (jnp.arange(rows * GRID_W) % GRID_W).astype(jnp.float32)
    m = rot_dim // 2
    inv = ROPE_THETA ** (-(jnp.arange(m // 2, dtype=jnp.float32) * 2.0) / m)
    ang_r = row[:, None] * inv
    ang_c = col[:, None] * inv
    return (jnp.cos(ang_r), jnp.sin(ang_r), jnp.cos(ang_c), jnp.sin(ang_c))


def _rot_half(x, cos, sin):
    shape = (cos.shape[0],) + (1,) * (x.ndim - 3) + (cos.shape[1],)
    c = cos.reshape(shape)
    s = sin.reshape(shape)
    x1, x2 = jnp.split(x, 2, axis=-1)
    return jnp.concatenate([x1 * c - x2 * s, x2 * c + x1 * s], axis=-1)


def rope_2d(x, tabs):
    cr, sr, cc, sc = tabs
    xr, xc = jnp.split(x.astype(jnp.float32), 2, axis=-1)
    return jnp.concatenate([_rot_half(xr, cr, sr), _rot_half(xc, cc, sc)], axis=-1).astype(x.dtype)


def block_attention(q, k, v, scale):
    bsz, s = q.shape[:2]
    nb = s // Q_BLOCK
    qb = jnp.moveaxis(q.reshape((bsz, nb, Q_BLOCK) + q.shape[2:]), 1, 0)

    def one_block(qi):
        sc = jnp.einsum('bqkgd,btkd->bkgqt', qi, k, preferred_element_type=jnp.float32) * scale
        p = jax.nn.softmax(sc, axis=-1).astype(v.dtype)
        return jnp.einsum('bkgqt,btkd->bqkgd', p, v)

    o = lax.map(one_block, qb)
    return jnp.moveaxis(o, 0, 1).reshape((bsz, s) + o.shape[3:])


def gqa_mixer(h_lat, h_ctx, wq, wk, wv, wo, q_g, k_g, tabs, need_ctx_out):
    def q_proj(h):
        bsz, t, _ = h.shape
        return rms_norm((h @ wq).reshape(bsz, t, A_KV_HEADS, A_GROUP, A_HEAD_DIM), q_g)

    def kv_proj(h):
        bsz, t, _ = h.shape
        k = rms_norm((h @ wk).reshape(bsz, t, A_KV_HEADS, A_HEAD_DIM), k_g)
        v = (h @ wv).reshape(bsz, t, A_KV_HEADS, A_HEAD_DIM)
        return k, v

    scale = A_HEAD_DIM ** -0.5
    k_c, v_c = kv_proj(h_ctx)
    q_l = rope_2d(q_proj(h_lat), tabs)
    k_l, v_l = kv_proj(h_lat)
    k_l = rope_2d(k_l, tabs)
    o_l = block_attention(q_l, jnp.concatenate([k_c, k_l], 1), jnp.concatenate([v_c, v_l], 1), scale)
    out_l = o_l.reshape(h_lat.shape[0], h_lat.shape[1], -1) @ wo
    out_c = None
    if need_ctx_out:
        o_c = block_attention(q_proj(h_ctx), k_c, v_c, scale)
        out_c = o_c.reshape(h_ctx.shape[0], h_ctx.shape[1], -1) @ wo
    return out_l, out_c


def centered_dwconv(u, w, b):
    t = u.shape[1]
    left = CONV_W // 2
    right = CONV_W - 1 - left
    up = jnp.pad(u, ((0, 0), (left, right), (0, 0)))
    return sum(up[:, j:j + t] * w[j] for j in range(CONV_W)) + b


def linear_scan(a, b, h0):
    b = b.at[:, 0].add(a[:, 0] * h0)

    def comb(l, r):
        return (l[0] * r[0], r[0] * l[1] + r[1])

    return lax.associative_scan(comb, (a, b), axis=1)[1]


def rglru_mixer(h_lat, h_ctx, wx, wy, conv_w, conv_b, ra_w, ra_b, ix_w, ix_b, lam, wo, need_ctx_out):
    dt = h_lat.dtype

    def coeffs(u, d):
        shp = u.shape
        ub = u.reshape(shp[0], shp[1], LRU_BLOCKS, LRU_BLOCK)
        r = jax.nn.sigmoid(jnp.einsum('btnk,nkj->btnj', ub, ra_w[d]).reshape(shp) + ra_b[d])
        gi = jax.nn.sigmoid(jnp.einsum('btnk,nkj->btnj', ub, ix_w[d]).reshape(shp) + ix_b[d])
        log_a = LRU_C * r.astype(jnp.float32) * jax.nn.log_sigmoid(lam[d].astype(jnp.float32))
        a = jnp.exp(log_a)
        bx = jnp.sqrt(-jnp.expm1(2.0 * log_a)) * (gi * u).astype(jnp.float32)
        return a, bx

    def scan_fwd(u, h0):
        a, bx = coeffs(u, 0)
        return linear_scan(a, bx, h0)

    def scan_bwd(u, h0):
        a, bx = coeffs(u, 1)
        return jnp.flip(linear_scan(jnp.flip(a, 1), jnp.flip(bx, 1), h0), 1)

    u_c = centered_dwconv(h_ctx @ wx, conv_w, conv_b)
    u_l = centered_dwconv(h_lat @ wx, conv_w, conv_b)
    zeros = jnp.zeros((h_ctx.shape[0], LRU_WIDTH), jnp.float32)
    hc_f = scan_fwd(u_c, zeros)
    hc_b = scan_bwd(u_c, zeros)
    hl_f = scan_fwd(u_l, hc_f[:, -1])
    hl_b = scan_bwd(u_l, hc_b[:, 0])
    out_l = ((hl_f + hl_b).astype(dt) * jax.nn.gelu(h_lat @ wy)) @ wo
    out_c = None
    if need_ctx_out:
        out_c = ((hc_f + hc_b).astype(dt) * jax.nn.gelu(h_ctx @ wy)) @ wo
    return out_l, out_c


def mla_mixer(h_lat, h_ctx, wq_a, q_a_g, wq_b, wkv_a, kv_a_g, wkv_b, wo, tabs, need_ctx_out):
    def q_proj(h, rotary):
        bsz, t, _ = h.shape
        q = (rms_norm(h @ wq_a, q_a_g) @ wq_b).reshape(bsz, t, MLA_HEADS, MLA_QK_NOPE + MLA_QK_ROPE)
        q_nope, q_pe = jnp.split(q, [MLA_QK_NOPE], axis=-1)
        if rotary:
            q_pe = rope_2d(q_pe, tabs)
        return jnp.concatenate([q_nope, q_pe], -1)[:, :, :, None, :]

    def kv_proj(h, rotary):
        bsz, t, _ = h.shape
        ckv, k_pe = jnp.split(h @ wkv_a, [MLA_KV_LORA], axis=-1)
        kv = (rms_norm(ckv, kv_a_g) @ wkv_b).reshape(bsz, t, MLA_HEADS, MLA_QK_NOPE + MLA_V_DIM)
        k_nope, v = jnp.split(kv, [MLA_QK_NOPE], axis=-1)
        k_pe = k_pe[:, :, None, :]
        if rotary:
            k_pe = rope_2d(k_pe, tabs)
        k = jnp.concatenate([k_nope, jnp.broadcast_to(k_pe, (bsz, t, MLA_HEADS, MLA_QK_ROPE))], -1)
        return k, v

    scale = (MLA_QK_NOPE + MLA_QK_ROPE) ** -0.5
    k_c, v_c = kv_proj(h_ctx, False)
    k_l, v_l = kv_proj(h_lat, True)
    o_l = block_attention(q_proj(h_lat, True), jnp.concatenate([k_c, k_l], 1),
                          jnp.concatenate([v_c, v_l], 1), scale)
    out_l = o_l.reshape(h_lat.shape[0], h_lat.shape[1], -1) @ wo
    out_c = None
    if need_ctx_out:
        o_c = block_attention(q_proj(h_ctx, False), k_c, v_c, scale)
        out_c = o_c.reshape(h_ctx.shape[0], h_ctx.shape[1], -1) @ wo
    return out_l, out_c


def sq_relu_mlp(h, w1, w2):
    return jnp.square(jax.nn.relu(h @ w1)) @ w2


def setup_inputs(seed: int = 0) -> dict:
    key = jax.random.key(seed)
    ks = iter(jax.random.split(key, 48))
    f32 = jnp.float32
    d = D_MODEL

    def nrm(shape, scale):
        return jax.random.normal(next(ks), shape, f32) * scale

    x = nrm((BATCH, SEQ, d), 1.0)
    c = nrm((BATCH, d), 1.0)
    ctx = nrm((BATCH, CTX_LEN, d), 1.0)
    c_ctx = nrm((d,), 1.0)
    ada_w = nrm((DEPTH, d, 6 * d), 0.5 * d ** -0.5)
    ada_b = nrm((DEPTH, 6 * d), 0.01)
    ln_g = 1.0 + nrm((DEPTH, 2, d), 0.01)
    ln_b = nrm((DEPTH, 2, d), 0.01)
    mlp_w1 = nrm((DEPTH, d, D_FF), d ** -0.5)
    mlp_w2 = nrm((DEPTH, D_FF, d), D_FF ** -0.5 * DN_BETA)
    gqa_wq = nrm((N_A, d, A_HEADS * A_HEAD_DIM), d ** -0.5)
    gqa_wk = nrm((N_A, d, A_KV_HEADS * A_HEAD_DIM), d ** -0.5)
    gqa_wv = nrm((N_A, d, A_KV_HEADS * A_HEAD_DIM), d ** -0.5)
    gqa_wo = nrm((N_A, A_HEADS * A_HEAD_DIM, d), (A_HEADS * A_HEAD_DIM) ** -0.5 * DN_BETA)
    gqa_q_g = 1.0 + nrm((N_A, A_HEAD_DIM), 0.01)
    gqa_k_g = 1.0 + nrm((N_A, A_HEAD_DIM), 0.01)
    lru_wx = nrm((N_B, d, LRU_WIDTH), d ** -0.5)
    lru_wy = nrm((N_B, d, LRU_WIDTH), d ** -0.5)
    lru_conv_w = nrm((N_B, CONV_W, LRU_WIDTH), CONV_W ** -0.5)
    lru_conv_b = nrm((N_B, LRU_WIDTH), 0.01)
    lru_ra_w = nrm((N_B, 2, LRU_BLOCKS, LRU_BLOCK, LRU_BLOCK), LRU_BLOCK ** -0.5)
    lru_ra_b = nrm((N_B, 2, LRU_WIDTH), 0.01)
    lru_ix_w = nrm((N_B, 2, LRU_BLOCKS, LRU_BLOCK, LRU_BLOCK), LRU_BLOCK ** -0.5)
    lru_ix_b = nrm((N_B, 2, LRU_WIDTH), 0.01)
    a0 = jax.random.uniform(next(ks), (N_B, 2, LRU_WIDTH), f32, 0.9, 0.999)
    s = a0 ** (1.0 / LRU_C)
    lru_lam = jnp.log(s) - jnp.log1p(-s)
    lru_wo = nrm((N_B, LRU_WIDTH, d), LRU_WIDTH ** -0.5 * DN_BETA)
    mla_wq_a = nrm((N_C, d, MLA_Q_LORA), d ** -0.5)
    mla_q_a_g = 1.0 + nrm((N_C, MLA_Q_LORA), 0.01)
    mla_wq_b = nrm((N_C, MLA_Q_LORA, MLA_HEADS * (MLA_QK_NOPE + MLA_QK_ROPE)), MLA_Q_LORA ** -0.5)
    mla_wkv_a = nrm((N_C, d, MLA_KV_LORA + MLA_QK_ROPE), d ** -0.5)
    mla_kv_a_g = 1.0 + nrm((N_C, MLA_KV_LORA), 0.01)
    mla_wkv_b = nrm((N_C, MLA_KV_LORA, MLA_HEADS * (MLA_QK_NOPE + MLA_V_DIM)), MLA_KV_LORA ** -0.5)
    mla_wo = nrm((N_C, MLA_HEADS * MLA_V_DIM, d), (MLA_HEADS * MLA_V_DIM) ** -0.5 * DN_BETA)
    return {'x': x, 'c': c, 'ctx': ctx, 'c_ctx': c_ctx, 'ada_w': ada_w, 'ada_b': ada_b,
            'ln_g': ln_g, 'ln_b': ln_b, 'mlp_w1': mlp_w1, 'mlp_w2': mlp_w2,
            'gqa_wq': gqa_wq, 'gqa_wk': gqa_wk, 'gqa_wv': gqa_wv, 'gqa_wo': gqa_wo,
            'gqa_q_g': gqa_q_g, 'gqa_k_g': gqa_k_g,
            'lru_wx': lru_wx, 'lru_wy': lru_wy, 'lru_conv_w': lru_conv_w, 'lru_conv_b': lru_conv_b,
            'lru_ra_w': lru_ra_w, 'lru_ra_b': lru_ra_b, 'lru_ix_w': lru_ix_w, 'lru_ix_b': lru_ix_b,
            'lru_lam': lru_lam, 'lru_wo': lru_wo,
            'mla_wq_a': mla_wq_a, 'mla_q_a_g': mla_q_a_g, 'mla_wq_b': mla_wq_b, 'mla_wkv_a': mla_wkv_a,
            'mla_kv_a_g': mla_kv_a_g, 'mla_wkv_b': mla_wkv_b, 'mla_wo': mla_wo}


def reference(x, c, ctx, c_ctx, ada_w, ada_b, ln_g, ln_b, mlp_w1, mlp_w2,
              gqa_wq, gqa_wk, gqa_wv, gqa_wo, gqa_q_g, gqa_k_g,
              lru_wx, lru_wy, lru_conv_w, lru_conv_b, lru_ra_w, lru_ra_b, lru_ix_w, lru_ix_b,
              lru_lam, lru_wo,
              mla_wq_a, mla_q_a_g, mla_wq_b, mla_wkv_a, mla_kv_a_g, mla_wkv_b, mla_wo):
    n_lat = x.shape[1]
    rows = n_lat // GRID_W
    tabs_gqa = axial_tables(rows, A_HEAD_DIM)
    tabs_mla = axial_tables(rows, MLA_QK_ROPE)
    sc_c = jax.nn.silu(c)
    sc_ctx = jax.nn.silu(c_ctx)
    hctx = ctx
    for i in range(DEPTH):
        kind = i % N_MIXERS
        slot = i // N_MIXERS
        last = i == DEPTH - 1
        m_l = jnp.split((sc_c @ ada_w[i] + ada_b[i])[:, None, :], 6, axis=-1)
        m_c = jnp.split(sc_ctx @ ada_w[i] + ada_b[i], 6, axis=-1)
        h_l = x * (1.0 + m_l[1]) + m_l[0]
        h_c = hctx * (1.0 + m_c[1]) + m_c[0]
        if kind == 0:
            o_l, o_c = gqa_mixer(h_l, h_c, gqa_wq[slot], gqa_wk[slot], gqa_wv[slot], gqa_wo[slot],
                                 gqa_q_g[slot], gqa_k_g[slot], tabs_gqa, not last)
        elif kind == 1:
            o_l, o_c = rglru_mixer(h_l, h_c, lru_wx[slot], lru_wy[slot], lru_conv_w[slot], lru_conv_b[slot],
                                   lru_ra_w[slot], lru_ra_b[slot], lru_ix_w[slot], lru_ix_b[slot],
                                   lru_lam[slot], lru_wo[slot], not last)
        else:
            o_l, o_c = mla_mixer(h_l, h_c, mla_wq_a[slot], mla_q_a_g[slot], mla_wq_b[slot], mla_wkv_a[slot],
                                 mla_kv_a_g[slot], mla_wkv_b[slot], mla_wo[slot], tabs_mla, not last)
        x = layer_norm(DN_ALPHA * x + (1.0 + m_l[2]) * o_l, ln_g[i, 0], ln_b[i, 0])
        f_l = sq_relu_mlp(x * (1.0 + m_l[4]) + m_l[3], mlp_w1[i], mlp_w2[i])
        x = layer_norm(DN_ALPHA * x + (1.0 + m_l[5]) * f_l, ln_g[i, 1], ln_b[i, 1])
        if not last:
            hctx = layer_norm(DN_ALPHA * hctx + (1.0 + m_c[2]) * o_c, ln_g[i, 0], ln_b[i, 0])
            f_c = sq_relu_mlp(hctx * (1.0 + m_c[4]) + m_c[3], mlp_w1[i], mlp_w2[i])
            hctx = layer_norm(DN_ALPHA * hctx + (1.0 + m_c[5]) * f_c, ln_g[i, 1], ln_b[i, 1])
    return x
```

```python
import functools
import math

import numpy as np
import jax
import jax.numpy as jnp
from jax import lax
from jax.experimental import pallas as pl
from jax.experimental.pallas import tpu as pltpu

F32 = jnp.float32
BF16 = jnp.bfloat16

TILE = 256
GRID_W = 64
ROPE_THETA = 10000.0
EPS = 1e-6
LANE = 128
VMEM_LIMIT_MB = 56

A_HEADS = 16
A_KV_HEADS = 4
A_GROUP = A_HEADS // A_KV_HEADS
A_HEAD_DIM = 128

LRU_BLOCKS = 8
CONV_W = 4
LRU_C = 8.0

MLA_HEADS = 16
MLA_LORA = 512
MLA_NOPE = 128
MLA_ROPE = 64
MLA_V = 128

KV_CHUNK = 512


def _cparams(sem, vmem_mb=VMEM_LIMIT_MB):
    return pltpu.CompilerParams(dimension_semantics=sem,
                                vmem_limit_bytes=vmem_mb * 1024 * 1024)


def _mm_body(*refs, n_a, n_ex, n_out, nk, prologue, epilogue):
    a_refs = refs[:n_a]
    w_ref = refs[n_a]
    ex = refs[n_a + 1:n_a + 1 + n_ex]
    outs = refs[n_a + 1 + n_ex:n_a + 1 + n_ex + n_out]
    scratch = refs[n_a + 1 + n_ex + n_out:]
    if prologue is None:
        a = a_refs[0][...]
    else:
        a = prologue(a_refs)
    part = jnp.dot(a, w_ref[...], preferred_element_type=F32)
    if nk == 1:
        epilogue(part, ex, outs)
    else:
        acc_ref = scratch[0]
        k = pl.program_id(2)

        @pl.when(k == 0)
        def _():
            acc_ref[...] = part

        @pl.when(k > 0)
        def _():
            acc_ref[...] += part

        @pl.when(k == nk - 1)
        def _():
            epilogue(acc_ref, ex, outs)


def _matmul(a_list, w, extras, outs, *, tm, tn, tk, epilogue, prologue=None, name=None):
    m, kdim = a_list[0].shape
    n = w.shape[1]
    assert m % tm == 0 and n % tn == 0 and kdim % tk == 0
    nk = kdim // tk
    grid = (n // tn, m // tm, nk)
    in_specs = [pl.BlockSpec((tm, tk), lambda j, i, k: (i, k)) for _ in a_list]
    in_specs.append(pl.BlockSpec((tk, tn), lambda j, i, k: (k, j)))
    in_specs += [s for _, s in extras]
    scratch = [pltpu.VMEM((tm, tn), F32)] if nk > 1 else []
    body = functools.partial(_mm_body, n_a=len(a_list), n_ex=len(extras), n_out=len(outs),
                             nk=nk, prologue=prologue, epilogue=epilogue)
    res = pl.pallas_call(
        body,
        grid=grid,
        in_specs=in_specs,
        out_specs=[s for _, s in outs],
        out_shape=[s for s, _ in outs],
        scratch_shapes=scratch,
        compiler_params=_cparams(("arbitrary", "arbitrary", "arbitrary")),
        name=name,
    )(*a_list, w, *[arr for arr, _ in extras])
    return res


def _row_spec(tm, width):
    return pl.BlockSpec((tm, width), lambda j, i, k: (i, 0))


def _out_spec(tm, tn):
    return pl.BlockSpec((tm, tn), lambda j, i, k: (i, j))


def _full_spec(shape):
    zeros = (0,) * len(shape)
    return pl.BlockSpec(shape, lambda j, i, k: zeros)


def _ep_cast(acc, ex, outs):
    outs[0][...] = acc[...].astype(outs[0].dtype)


def _ep_relu2(acc, ex, outs):
    r = jnp.maximum(acc[...], 0.0)
    outs[0][...] = (r * r).astype(outs[0].dtype)


def _ep_gelu(acc, ex, outs):
    outs[0][...] = jax.nn.gelu(acc[...], approximate=True).astype(outs[0].dtype)


def _rope(x, cos, s_hi, s_lo, off):
    return x * cos + pltpu.roll(x, LANE - off, 1) * s_hi + pltpu.roll(x, off, 1) * s_lo


def _ep_headnorm_rope(acc, ex, outs, *, nblk, off, scale):
    g_ref, cos_ref, shi_ref, slo_ref = ex
    g = g_ref[...]
    cos, shi, slo = cos_ref[...], shi_ref[...], slo_ref[...]
    for j in range(nblk):
        x = acc[:, j * LANE:(j + 1) * LANE]
        ms = jnp.mean(x * x, axis=-1, keepdims=True)
        xn = x * lax.rsqrt(ms + EPS) * g
        y = _rope(xn, cos, shi, slo, off)
        if scale != 1.0:
            y = y * scale
        outs[0][:, j * LANE:(j + 1) * LANE] = y.astype(outs[0].dtype)


def _ep_rmsnorm(acc, ex, outs):
    g_ref, = ex
    x = acc[...]
    ms = jnp.mean(x * x, axis=-1, keepdims=True)
    outs[0][...] = (x * lax.rsqrt(ms + EPS) * g_ref[...]).astype(outs[0].dtype)


def _ep_mla_kva(acc, ex, outs, *, lora, off):
    g_ref, cos_ref, shi_ref, slo_ref = ex
    x = acc[:, 0:lora]
    ms = jnp.mean(x * x, axis=-1, keepdims=True)
    outs[0][:, 0:lora] = (x * lax.rsqrt(ms + EPS) * g_ref[...]).astype(outs[0].dtype)
    pe = acc[:, lora:lora + LANE]
    outs[0][:, lora:lora + LANE] = _rope(pe, cos_ref[...], shi_ref[...], slo_ref[...], off).astype(outs[0].dtype)


def _ep_mla_q(acc, ex, outs, *, nheads, off, scale):
    cos_ref, shi_ref, slo_ref = ex
    cos, shi, slo = cos_ref[...], shi_ref[...], slo_ref[...]
    for h in range(nheads):
        c0 = h * 2 * LANE
        outs[0][:, c0:c0 + LANE] = (acc[:, c0:c0 + LANE] * scale).astype(outs[0].dtype)
        pe = acc[:, c0 + LANE:c0 + 2 * LANE]
        outs[0][:, c0 + LANE:c0 + 2 * LANE] = (_rope(pe, cos, shi, slo, off) * scale).astype(outs[0].dtype)


def _ep_resid_ln_mod(acc, ex, outs, *, ngroups, alpha):
    xold_ref, mod_ref, lng_ref, lnb_ref = ex
    x_out, h_out = outs
    lng, lnb = lng_ref[...], lnb_ref[...]
    for g in range(ngroups):
        r0 = g * TILE
        gate = mod_ref[g, 0:1, :]
        scale = mod_ref[g, 1:2, :]
        shift = mod_ref[g, 2:3, :]
        y = alpha * xold_ref[r0:r0 + TILE, :] + (1.0 + gate) * acc[r0:r0 + TILE, :]
        mu = jnp.mean(y, axis=-1, keepdims=True)
        yc = y - mu
        var = jnp.mean(yc * yc, axis=-1, keepdims=True)
        xn = yc * lax.rsqrt(var + EPS) * lng + lnb
        x_out[r0:r0 + TILE, :] = xn
        h_out[r0:r0 + TILE, :] = (xn * (1.0 + scale) + shift).astype(h_out.dtype)


def _pro_lru_gate(a_refs):
    hf_ref, hb_ref, y_ref = a_refs
    return ((hf_ref[...] + hb_ref[...]) * y_ref[...]).astype(BF16)


def _ada_body(c_ref, w_ref, b_ref, o_ref):
    cond = c_ref[...]
    sc = (cond * jax.nn.sigmoid(cond)).astype(BF16)
    o_ref[0] = jnp.dot(sc, w_ref[0].astype(BF16), preferred_element_type=F32) + b_ref[0]


def _ada_mods(cond, ada_w, ada_b, tn=1024):
    depth, d, n = ada_w.shape
    rows = cond.shape[0]
    return pl.pallas_call(
        _ada_body,
        grid=(depth, n // tn),
        in_specs=[pl.BlockSpec((rows, d), lambda l, j: (0, 0)),
                  pl.BlockSpec((1, d, tn), lambda l, j: (l, 0, j)),
                  pl.BlockSpec((1, 1, tn), lambda l, j: (l, 0, j))],
        out_specs=pl.BlockSpec((1, rows, tn), lambda l, j: (l, 0, j)),
        out_shape=jax.ShapeDtypeStruct((depth, rows, n), F32),
        compiler_params=_cparams(("arbitrary", "arbitrary")),
        name="ada_mods",
    )(cond, ada_w, ada_b.reshape(depth, 1, n))


def _modulate_body(x_ref, mod_ref, h_ref):
    h_ref[...] = (x_ref[...] * (1.0 + mod_ref[0, 1:2, :]) + mod_ref[0, 0:1, :]).astype(h_ref.dtype)


def _modulate(xs, mod):
    r, d = xs.shape
    return pl.pallas_call(
        _modulate_body,
        grid=(r // TILE,),
        in_specs=[pl.BlockSpec((TILE, d), lambda i: (i, 0)),
                  pl.BlockSpec((1, mod.shape[1], d), lambda i: (i, 0, 0))],
        out_specs=pl.BlockSpec((TILE, d), lambda i: (i, 0)),
        out_shape=jax.ShapeDtypeStruct((r, d), BF16),
        compiler_params=_cparams(("arbitrary",)),
        name="modulate",
    )(xs, mod)


def _dot_nt(a, b):
    return lax.dot_general(a, b, (((1,), (1,)), ((), ())), preferred_element_type=F32)


def _attn_body(*refs, n_terms, group, n_lat_chunks):
    q_ref = refs[0]
    k_refs = refs[1:1 + n_terms]
    v_ref = refs[1 + n_terms]
    o_ref = refs[2 + n_terms]
    i = pl.program_id(2)
    tq = q_ref.shape[0]
    dv = v_ref.shape[1]
    n_chunks = jnp.where(i == 0, 0, n_lat_chunks)

    for g in range(group):
        q_terms = [q_ref[:, (g * n_terms + t) * LANE:(g * n_terms + t + 1) * LANE]
                   for t in range(n_terms)]

        def scores(start, size):
            s = _dot_nt(q_terms[0], k_refs[0][pl.ds(start, size), :])
            for t in range(1, n_terms):
                s = s + _dot_nt(q_terms[t], k_refs[t][pl.ds(start, size), :])
            return s

        s0 = scores(0, TILE)
        m0 = jnp.max(s0, axis=-1, keepdims=True)
        p0 = jnp.exp(s0 - m0)
        l0 = jnp.sum(p0, axis=-1, keepdims=True)
        acc0 = jnp.dot(p0.astype(BF16), v_ref[pl.ds(0, TILE), :], preferred_element_type=F32)

        def step(c, carry):
            m, l, acc = carry
            start = pl.multiple_of(TILE + c * KV_CHUNK, TILE)
            s = scores(start, KV_CHUNK)
            m_new = jnp.maximum(m, jnp.max(s, axis=-1, keepdims=True))
            alpha = jnp.exp(m - m_new)
            p = jnp.exp(s - m_new)
            l = alpha * l + jnp.sum(p, axis=-1, keepdims=True)
            acc = alpha * acc + jnp.dot(p.astype(BF16), v_ref[pl.ds(start, KV_CHUNK), :],
                                        preferred_element_type=F32)
            return m_new, l, acc

        m, l, acc = lax.fori_loop(0, n_chunks, step, (m0, l0, acc0))
        o_ref[:, g * dv:(g + 1) * dv] = (acc * (1.0 / l)).astype(o_ref.dtype)


def _attention(q, q_cols, k_list, v, out_cols, *, batch, n_tiles, n_heads_kv, group):
    r = q.shape[0]
    t_len = n_tiles * TILE
    n_terms = len(k_list)
    n_lat_chunks = (t_len - TILE) // KV_CHUNK
    assert (t_len - TILE) % KV_CHUNK == 0
    in_specs = [pl.BlockSpec((TILE, q_cols), lambda b, h, i: (b * n_tiles + i, h))]
    args = [q]
    for arr, colfn in k_list + [v]:
        in_specs.append(pl.BlockSpec((t_len, LANE), functools.partial(
            lambda b, h, i, colfn: (b, colfn(h)), colfn=colfn)))
        args.append(arr)
    return pl.pallas_call(
        functools.partial(_attn_body, n_terms=n_terms, group=group, n_lat_chunks=n_lat_chunks),
        grid=(batch, n_heads_kv, n_tiles),
        in_specs=in_specs,
        out_specs=pl.BlockSpec((TILE, out_cols), lambda b, h, i: (b * n_tiles + i, h)),
        out_shape=jax.ShapeDtypeStruct((r, n_heads_kv * out_cols), BF16),
        compiler_params=_cparams(("arbitrary", "arbitrary", "arbitrary")),
        name="attention",
    )(*args)


def _conv_body(x_ref, prev_ref, next_ref, w_ref, b_ref, u_ref, ext_ref, *, n_tiles):
    i = pl.program_id(0)
    seg = i % n_tiles
    has_prev = seg > 1
    has_next = jnp.logical_and(seg > 0, seg < n_tiles - 1)
    ext_ref[0:8, :] = jnp.where(has_prev, prev_ref[...], 0.0)
    ext_ref[8:8 + TILE, :] = x_ref[...]
    ext_ref[8 + TILE:16 + TILE, :] = jnp.where(has_next, next_ref[...], 0.0)
    left = CONV_W // 2
    acc = b_ref[...] + ext_ref[pl.ds(8 - left, TILE), :] * w_ref[0:1, :]
    for j in range(1, CONV_W):
        acc = acc + ext_ref[pl.ds(8 - left + j, TILE), :] * w_ref[j:j + 1, :]
    u_ref[...] = acc


def _dwconv(xw, conv_w, conv_b, n_tiles):
    r, w = xw.shape
    sub = TILE // 8
    n_sub = r // 8
    return pl.pallas_call(
        functools.partial(_conv_body, n_tiles=n_tiles),
        grid=(r // TILE,),
        in_specs=[pl.BlockSpec((TILE, w), lambda i: (i, 0)),
                  pl.BlockSpec((8, w), lambda i: (jnp.maximum(i * sub - 1, 0), 0)),
                  pl.BlockSpec((8, w), lambda i: (jnp.minimum((i + 1) * sub, n_sub - 1), 0)),
                  pl.BlockSpec((CONV_W, w), lambda i: (0, 0)),
                  pl.BlockSpec((1, w), lambda i: (0, 0))],
        out_specs=pl.BlockSpec((TILE, w), lambda i: (i, 0)),
        out_shape=jax.ShapeDtypeStruct((r, w), F32),
        scratch_shapes=[pltpu.VMEM((TILE + 16, w), F32)],
        compiler_params=_cparams(("arbitrary",)),
        name="dwconv",
    )(xw, xw, xw, conv_w, conv_b.reshape(1, w))


def _log_sigmoid(x):
    return jnp.minimum(x, 0.0) - jnp.log1p(jnp.exp(-jnp.abs(x)))


def _scan_tile(a, b, reverse):
    n = a.shape[0]
    row = lax.broadcasted_iota(jnp.int32, a.shape, 0)
    s = 1
    while s < n:
        if reverse:
            a_sh = pltpu.roll(a, n - s, 0)
            b_sh = pltpu.roll(b, n - s, 0)
            valid = row < n - s
        else:
            a_sh = pltpu.roll(a, s, 0)
            b_sh = pltpu.roll(b, s, 0)
            valid = row >= s
        b = jnp.where(valid, b + a * b_sh, b)
        a = jnp.where(valid, a * a_sh, a)
        s *= 2
    return a, b


def _lru_body(uf_ref, ub_ref, raw_ref, ixw_ref, rab_ref, ixb_ref, lam_ref,
              hf_ref, hb_ref, carry_ref, *, blk):
    j = pl.program_id(1)

    @pl.when(j == 0)
    def _():
        carry_ref[...] = jnp.zeros_like(carry_ref)

    for d, (u_ref, h_ref) in enumerate(((uf_ref, hf_ref), (ub_ref, hb_ref))):
        lsl = LRU_C * _log_sigmoid(lam_ref[d])
        for n in range(LRU_BLOCKS):
            cs = slice(n * blk, (n + 1) * blk)
            u = u_ref[:, cs]
            ub16 = u.astype(BF16)
            r = jax.nn.sigmoid(jnp.dot(ub16, raw_ref[d, n], preferred_element_type=F32) + rab_ref[d][:, cs])
            gi = jax.nn.sigmoid(jnp.dot(ub16, ixw_ref[d, n], preferred_element_type=F32) + ixb_ref[d][:, cs])
            log_a = r * lsl[:, cs]
            a = jnp.exp(log_a)
            bx = jnp.sqrt(1.0 - a * a) * (gi * u)
            a_cum, h_loc = _scan_tile(a, bx, reverse=(d == 1))
            h = h_loc + a_cum * carry_ref[d, 0:1, cs]
            h_ref[:, cs] = h
            last = h[0:1, :] if d == 1 else h[TILE - 1:TILE, :]
            carry_ref[d, :, cs] = jnp.broadcast_to(last, (8, blk))


def _lru_scan(u, ra_w, ix_w, ra_b, ix_b, lam, *, batch, n_tiles):
    r, w = u.shape
    blk = w // LRU_BLOCKS

    def fwd_map(b, j):
        return (b * n_tiles + j, 0)

    def bwd_map(b, j):
        return (b * n_tiles + jnp.where(j == 0, 0, n_tiles - j), 0)

    full5 = pl.BlockSpec((2, LRU_BLOCKS, blk, blk), lambda b, j: (0, 0, 0, 0))
    vec = pl.BlockSpec((2, 1, w), lambda b, j: (0, 0, 0))
    return pl.pallas_call(
        functools.partial(_lru_body, blk=blk),
        grid=(batch, n_tiles),
        in_specs=[pl.BlockSpec((TILE, w), fwd_map), pl.BlockSpec((TILE, w), bwd_map),
                  full5, full5, vec, vec, vec],
        out_specs=[pl.BlockSpec((TILE, w), fwd_map), pl.BlockSpec((TILE, w), bwd_map)],
        out_shape=[jax.ShapeDtypeStruct((r, w), F32), jax.ShapeDtypeStruct((r, w), F32)],
        scratch_shapes=[pltpu.VMEM((2, 8, w), F32)],
        compiler_params=_cparams(("arbitrary", "arbitrary")),
        name="lru_scan",
    )(u, u, ra_w, ix_w, ra_b.reshape(2, 1, w), ix_b.reshape(2, 1, w), lam.reshape(2, 1, w))


def _rope_tables(n_lat, rot_dim):
    m = rot_dim // 2
    half = m // 2
    t = np.arange(n_lat)
    pos = np.stack([t // GRID_W, t % GRID_W], axis=1).astype(np.float32)
    inv = (ROPE_THETA ** (-(np.arange(half, dtype=np.float32) * 2.0) / m)).astype(np.float32)
    lane = np.arange(LANE)
    axis = np.minimum(lane // m, 1)
    freq = inv[lane % half]
    ang = pos[:, axis] * freq[None, :]
    live = (lane < rot_dim)[None, :]
    first = ((lane % m) < half)[None, :]
    cos = np.where(live, np.cos(ang), 1.0)
    sin = np.where(live, np.sin(ang), 0.0)
    s_hi = np.where(first, -sin, 0.0)
    s_lo = np.where(first, 0.0, sin)
    ident = np.concatenate([np.ones((TILE, LANE)), np.zeros((TILE, LANE)), np.zeros((TILE, LANE))], 1)
    tab = np.concatenate([cos, s_hi, s_lo], axis=1)
    tab = np.concatenate([ident, tab], axis=0).astype(np.float32)
    return (jnp.asarray(tab[:, :LANE]), jnp.asarray(tab[:, LANE:2 * LANE]), jnp.asarray(tab[:, 2 * LANE:]))


def kernel(x, c, ctx, c_ctx, ada_w, ada_b, ln_g, ln_b, mlp_w1, mlp_w2, gqa_wq, gqa_wk, gqa_wv, gqa_wo, gqa_q_g, gqa_k_g, lru_wx, lru_wy, lru_conv_w, lru_conv_b, lru_ra_w, lru_ra_b, lru_ix_w, lru_ix_b, lru_lam, lru_wo, mla_wq_a, mla_q_a_g, mla_wq_b, mla_wkv_a, mla_kv_a_g, mla_wkv_b, mla_wo):
    batch, n_lat, d = x.shape
    depth = ada_w.shape[0]
    assert ctx.shape[1] == TILE and n_lat % KV_CHUNK == 0 and n_lat % GRID_W == 0
    t_len = TILE + n_lat
    n_tiles = t_len // TILE
    rows = batch * t_len
    n_groups = rows // TILE
    alpha = (2 * depth) ** 0.25

    xs = jnp.concatenate([ctx, x], axis=1).reshape(rows, d)

    cond_rows = -(-(batch + 1) // 8) * 8
    cond = jnp.concatenate([c, c_ctx[None, :], jnp.zeros((cond_rows - batch - 1, d), F32)], axis=0)
    mods = _ada_mods(cond, ada_w, ada_b)
    m_lat = jnp.broadcast_to(mods[:, :batch, None, :], (depth, batch, n_tiles - 1, 6 * d))
    m_ctx = jnp.broadcast_to(mods[:, batch:batch + 1, None, :], (depth, batch, 1, 6 * d))
    mods = jnp.concatenate([m_ctx, m_lat], axis=2).reshape(depth, n_groups, 6, d)

    def mod3(gate, scale, shift):
        return jnp.stack([gate, scale, shift], axis=1)

    tabs_gqa = _rope_tables(n_lat, A_HEAD_DIM)
    tabs_mla = _rope_tables(n_lat, MLA_ROPE)

    def tab_specs():
        return [pl.BlockSpec((TILE, LANE), lambda j, i, k: (i % n_tiles, 0)) for _ in range(3)]

    def mod_spec(tm):
        return pl.BlockSpec((tm // TILE, 3, d), lambda j, i, k: (i, 0, 0))

    def resid_ln(a_list, w, x_old, mod, lng, lnb, *, tm, tk, prologue=None, name=None):
        ex = [(x_old, _row_spec(tm, d)), (mod, mod_spec(tm)),
              (lng.reshape(1, d), _full_spec((1, d))), (lnb.reshape(1, d), _full_spec((1, d)))]
        outs = [(jax.ShapeDtypeStruct((rows, d), F32), _out_spec(tm, d)),
                (jax.ShapeDtypeStruct((rows, d), BF16), _out_spec(tm, d))]
        return _matmul(a_list, w, ex, outs, tm=tm, tn=d, tk=tk, prologue=prologue, name=name,
                       epilogue=functools.partial(_ep_resid_ln_mod, ngroups=tm // TILE, alpha=alpha))

    def plain(a, w, *, tm, tn, epilogue=_ep_cast, dtype=BF16, name=None):
        n = w.shape[1]
        outs = [(jax.ShapeDtypeStruct((a.shape[0], n), dtype), _out_spec(tm, tn))]
        return _matmul([a], w, [], outs, tm=tm, tn=tn, tk=a.shape[1], epilogue=epilogue, name=name)[0]

    h = _modulate(xs, mods[0])

    for i in range(depth):
        kind = i % 3
        slot = i // 3
        m_i = mods[i]
        if kind == 0:
            wq, wk, wv, wo = (gqa_wq[slot].astype(BF16), gqa_wk[slot].astype(BF16),
                              gqa_wv[slot].astype(BF16), gqa_wo[slot].astype(BF16))

            def qk_proj(w, gain, scale, name):
                n = w.shape[1]
                ex = [(gain.reshape(1, LANE), _full_spec((1, LANE)))] + list(zip(tabs_gqa, tab_specs()))
                outs = [(jax.ShapeDtypeStruct((rows, n), BF16), _out_spec(TILE, n))]
                ep = functools.partial(_ep_headnorm_rope, nblk=n // LANE, off=A_HEAD_DIM // 4, scale=scale)
                return _matmul([h], w, ex, outs, tm=TILE, tn=n, tk=d, epilogue=ep, name=name)[0]

            q = qk_proj(wq, gqa_q_g[slot], A_HEAD_DIM ** -0.5, "gqa_q")
            k = qk_proj(wk, gqa_k_g[slot], 1.0, "gqa_k")
            v = plain(h, wv, tm=2 * TILE, tn=wv.shape[1], name="gqa_v")
            o = _attention(q, A_GROUP * LANE, [(k, lambda hh: hh)], (v, lambda hh: hh), A_GROUP * LANE,
                           batch=batch, n_tiles=n_tiles, n_heads_kv=A_KV_HEADS, group=A_GROUP)
            a_list, w_out, prologue = [o], wo, None
        elif kind == 1:
            wx, wy, wo = lru_wx[slot].astype(BF16), lru_wy[slot].astype(BF16), lru_wo[slot].astype(BF16)
            xw = plain(h, wx, tm=2 * TILE, tn=wx.shape[1], dtype=F32, name="lru_x")
            y = plain(h, wy, tm=2 * TILE, tn=wy.shape[1], epilogue=_ep_gelu, dtype=F32, name="lru_y")
            u = _dwconv(xw, lru_conv_w[slot], lru_conv_b[slot], n_tiles)
            hf, hb = _lru_scan(u, lru_ra_w[slot].astype(BF16), lru_ix_w[slot].astype(BF16),
                               lru_ra_b[slot], lru_ix_b[slot], lru_lam[slot],
                               batch=batch, n_tiles=n_tiles)
            a_list, w_out, prologue = [hf, hb, y], wo, _pro_lru_gate
        else:
            nq = MLA_NOPE + MLA_ROPE
            wq_a = mla_wq_a[slot].astype(BF16)
            wq_b = mla_wq_b[slot].reshape(MLA_LORA, MLA_HEADS, nq)
            wq_b = jnp.concatenate([wq_b, jnp.zeros((MLA_LORA, MLA_HEADS, 2 * LANE - nq), F32)], axis=-1)
            wq_b = wq_b.reshape(MLA_LORA, MLA_HEADS * 2 * LANE).astype(BF16)
            wkv_a = jnp.concatenate([mla_wkv_a[slot], jnp.zeros((d, LANE - MLA_ROPE), F32)], axis=-1).astype(BF16)
            wkv_b = mla_wkv_b[slot].reshape(MLA_LORA, MLA_HEADS, MLA_NOPE + MLA_V)
            wkv_b = jnp.concatenate([wkv_b[:, :, :MLA_NOPE].reshape(MLA_LORA, -1),
                                     wkv_b[:, :, MLA_NOPE:].reshape(MLA_LORA, -1)], axis=-1).astype(BF16)
            wo = mla_wo[slot].astype(BF16)

            qa = _matmul([h], wq_a, [(mla_q_a_g[slot].reshape(1, MLA_LORA), _full_spec((1, MLA_LORA)))],
                         [(jax.ShapeDtypeStruct((rows, MLA_LORA), BF16), _out_spec(2 * TILE, MLA_LORA))],
                         tm=2 * TILE, tn=MLA_LORA, tk=d, epilogue=_ep_rmsnorm, name="mla_qa")[0]
            nqb = wq_b.shape[1]
            q = _matmul([qa], wq_b, list(zip(tabs_mla, tab_specs())),
                        [(jax.ShapeDtypeStruct((rows, nqb), BF16), _out_spec(TILE, nqb))],
                        tm=TILE, tn=nqb, tk=MLA_LORA, name="mla_q",
                        epilogue=functools.partial(_ep_mla_q, nheads=MLA_HEADS, off=MLA_ROPE // 4,
                                                   scale=float(nq) ** -0.5))[0]
            nkva = wkv_a.shape[1]
            ex = [(mla_kv_a_g[slot].reshape(1, MLA_LORA), _full_spec((1, MLA_LORA)))] + list(zip(tabs_mla, tab_specs()))
            kva = _matmul([h], wkv_a, ex,
                          [(jax.ShapeDtypeStruct((rows, nkva), BF16), _out_spec(TILE, nkva))],
                          tm=TILE, tn=nkva, tk=d, name="mla_kva",
                          epilogue=functools.partial(_ep_mla_kva, lora=MLA_LORA, off=MLA_ROPE // 4))[0]
            nkv = wkv_b.shape[1]
            kv = pl.pallas_call(
                functools.partial(_mm_body, n_a=1, n_ex=0, n_out=1, nk=1, prologue=None, epilogue=_ep_cast),
                grid=(1, rows // (2 * TILE), 1),
                in_specs=[pl.BlockSpec((2 * TILE, MLA_LORA), lambda j, ii, kk: (ii, 0)),
                          pl.BlockSpec((MLA_LORA, nkv), lambda j, ii, kk: (0, 0))],
                out_specs=[pl.BlockSpec((2 * TILE, nkv), lambda j, ii, kk: (ii, 0))],
                out_shape=[jax.ShapeDtypeStruct((rows, nkv), BF16)],
                compiler_params=_cparams(("arbitrary", "arbitrary", "arbitrary")),
                name="mla_kv",
            )(kva, wkv_b)[0]
            o = _attention(q, 2 * LANE,
                           [(kv, lambda hh: hh), (kva, lambda hh: MLA_LORA // LANE)],
                           (kv, lambda hh: MLA_HEADS + hh), LANE,
                           batch=batch, n_tiles=n_tiles, n_heads_kv=MLA_HEADS, group=1)
            a_list, w_out, prologue = [o], wo, None

        xs, h2 = resid_ln(a_list, w_out, xs, mod3(m_i[:, 2], m_i[:, 4], m_i[:, 3]),
                          ln_g[i, 0], ln_b[i, 0], tm=TILE, tk=w_out.shape[0], prologue=prologue,
                          name="mix_out")
        w1 = mlp_w1[i].astype(BF16)
        w2 = mlp_w2[i].astype(BF16)
        hid = plain(h2, w1, tm=2 * TILE, tn=1024, epilogue=_ep_relu2, name="mlp_up")
        if i + 1 < depth:
            nxt = mod3(m_i[:, 5], mods[i + 1][:, 1], mods[i + 1][:, 0])
        else:
            zero = jnp.zeros_like(m_i[:, 5])
            nxt = mod3(m_i[:, 5], zero, zero)
        xs, h = resid_ln([hid], w2, xs, nxt, ln_g[i, 1], ln_b[i, 1], tm=2 * TILE, tk=1024,
                         name="mlp_down")

    return xs.reshape(batch, t_len, d)[:, TILE:, :]
```

```python
import functools
import math

import numpy as np
import jax
import jax.numpy as jnp
from jax import lax
from jax.experimental import pallas as pl
from jax.experimental.pallas import tpu as pltpu

F32 = jnp.float32
BF16 = jnp.bfloat16

TILE = 256
GRID_W = 64
ROPE_THETA = 10000.0
EPS = 1e-6
LANE = 128
VMEM_LIMIT_MB = 56

A_HEADS = 16
A_KV_HEADS = 4
A_GROUP = A_HEADS // A_KV_HEADS
A_HEAD_DIM = 128

LRU_BLOCKS = 8
CONV_W = 4
LRU_C = 8.0

MLA_HEADS = 16
MLA_LORA = 512
MLA_NOPE = 128
MLA_ROPE = 64
MLA_V = 128

KV_CHUNK = 512
MLA_STEP_HEADS = 4
LOG2E = math.log2(math.e)


def _cparams(sem, vmem_mb=VMEM_LIMIT_MB):
    return pltpu.CompilerParams(dimension_semantics=sem,
                                vmem_limit_bytes=vmem_mb * 1024 * 1024)


def _mm_body(*refs, n_a, n_ex, n_out, nk, prologue, epilogue):
    a_refs = refs[:n_a]
    w_ref = refs[n_a]
    ex = refs[n_a + 1:n_a + 1 + n_ex]
    outs = refs[n_a + 1 + n_ex:n_a + 1 + n_ex + n_out]
    scratch = refs[n_a + 1 + n_ex + n_out:]
    if prologue is None:
        a = a_refs[0][...]
    else:
        a = prologue(a_refs)
    part = jnp.dot(a, w_ref[...], preferred_element_type=F32)
    if nk == 1:
        epilogue(part, ex, outs)
    else:
        acc_ref = scratch[0]
        k = pl.program_id(2)

        @pl.when(k == 0)
        def _():
            acc_ref[...] = part

        @pl.when(k > 0)
        def _():
            acc_ref[...] += part

        @pl.when(k == nk - 1)
        def _():
            epilogue(acc_ref, ex, outs)


def _matmul(a_list, w, extras, outs, *, tm, tn, tk, epilogue, prologue=None, name=None, kdim=None):
    m = a_list[0].shape[0]
    kdim = a_list[0].shape[1] if kdim is None else kdim
    n = w.shape[1]
    assert m % tm == 0 and n % tn == 0 and kdim % tk == 0
    nk = kdim // tk
    grid = (n // tn, m // tm, nk)
    in_specs = [pl.BlockSpec((tm, tk), lambda j, i, k: (i, k)) for _ in a_list]
    in_specs.append(pl.BlockSpec((tk, tn), lambda j, i, k: (k, j)))
    in_specs += [s for _, s in extras]
    scratch = [pltpu.VMEM((tm, tn), F32)] if nk > 1 else []
    body = functools.partial(_mm_body, n_a=len(a_list), n_ex=len(extras), n_out=len(outs),
                             nk=nk, prologue=prologue, epilogue=epilogue)
    res = pl.pallas_call(
        body,
        grid=grid,
        in_specs=in_specs,
        out_specs=[s for _, s in outs],
        out_shape=[s for s, _ in outs],
        scratch_shapes=scratch,
        compiler_params=_cparams(("arbitrary", "arbitrary", "arbitrary")),
        name=name,
    )(*a_list, w, *[arr for arr, _ in extras])
    return res


def _row_spec(tm, width):
    return pl.BlockSpec((tm, width), lambda j, i, k: (i, 0))


def _out_spec(tm, tn):
    return pl.BlockSpec((tm, tn), lambda j, i, k: (i, j))


def _full_spec(shape):
    zeros = (0,) * len(shape)
    return pl.BlockSpec(shape, lambda j, i, k: zeros)


def _ep_cast(acc, ex, outs):
    outs[0][...] = acc[...].astype(outs[0].dtype)


def _ep_cast_t(acc, ex, outs):
    outs[0][...] = acc[...].T.astype(outs[0].dtype)


def _ep_gelu(acc, ex, outs):
    outs[0][...] = jax.nn.gelu(acc[...], approximate=True).astype(outs[0].dtype)


def _rope(x, cos, s_hi, s_lo, off):
    return x * cos + pltpu.roll(x, LANE - off, 1) * s_hi + pltpu.roll(x, off, 1) * s_lo


def _ep_headnorm_rope(acc, ex, outs, *, nblk, off, scale):
    g_ref, cos_ref, shi_ref, slo_ref = ex
    g = g_ref[...]
    cos, shi, slo = cos_ref[...], shi_ref[...], slo_ref[...]
    for j in range(nblk):
        x = acc[:, j * LANE:(j + 1) * LANE]
        ms = jnp.mean(x * x, axis=-1, keepdims=True)
        xn = x * lax.rsqrt(ms + EPS) * g
        y = _rope(xn, cos, shi, slo, off)
        if scale != 1.0:
            y = y * scale
        outs[0][:, j * LANE:(j + 1) * LANE] = y.astype(outs[0].dtype)


def _ep_rmsnorm(acc, ex, outs):
    g_ref, = ex
    x = acc[...]
    ms = jnp.mean(x * x, axis=-1, keepdims=True)
    outs[0][...] = (x * lax.rsqrt(ms + EPS) * g_ref[...]).astype(outs[0].dtype)


def _ep_mla_kva(acc, ex, outs, *, lora, off):
    g_ref, cos_ref, shi_ref, slo_ref = ex
    x = acc[:, 0:lora]
    ms = jnp.mean(x * x, axis=-1, keepdims=True)
    outs[0][:, 0:lora] = (x * lax.rsqrt(ms + EPS) * g_ref[...]).astype(outs[0].dtype)
    pe = acc[:, lora:lora + LANE]
    outs[0][:, lora:lora + LANE] = _rope(pe, cos_ref[...], shi_ref[...], slo_ref[...], off).astype(outs[0].dtype)


def _ep_mla_q(acc, ex, outs, *, nheads, off, scale):
    cos_ref, shi_ref, slo_ref = ex
    cos, shi, slo = cos_ref[...], shi_ref[...], slo_ref[...]
    for h in range(nheads):
        c0 = h * 2 * LANE
        outs[0][:, c0:c0 + LANE] = (acc[:, c0:c0 + LANE] * scale).astype(outs[0].dtype)
        pe = acc[:, c0 + LANE:c0 + 2 * LANE]
        outs[0][:, c0 + LANE:c0 + 2 * LANE] = (_rope(pe, cos, shi, slo, off) * scale).astype(outs[0].dtype)


def _ep_mla_kv(acc, ex, outs, *, nheads):
    kpe_ref, = ex
    k_out, vt_out = outs
    kpe = kpe_ref[...]
    for h in range(nheads):
        k_out[:, 2 * h * LANE:(2 * h + 1) * LANE] = acc[:, h * LANE:(h + 1) * LANE].astype(k_out.dtype)
        k_out[:, (2 * h + 1) * LANE:(2 * h + 2) * LANE] = kpe
    vt_out[...] = acc[:, nheads * LANE:2 * nheads * LANE].T.astype(vt_out.dtype)


def _ep_resid_ln_mod(acc, ex, outs, *, ngroups, alpha):
    xold_ref, mod_ref, lng_ref, lnb_ref = ex
    x_out, h_out = outs
    lng, lnb = lng_ref[...], lnb_ref[...]
    for g in range(ngroups):
        r0 = g * TILE
        gate = mod_ref[g, 0:1, :]
        scale = mod_ref[g, 1:2, :]
        shift = mod_ref[g, 2:3, :]
        y = alpha * xold_ref[r0:r0 + TILE, :] + (1.0 + gate) * acc[r0:r0 + TILE, :]
        mu = jnp.mean(y, axis=-1, keepdims=True)
        yc = y - mu
        var = jnp.mean(yc * yc, axis=-1, keepdims=True)
        xn = yc * lax.rsqrt(var + EPS) * lng + lnb
        x_out[r0:r0 + TILE, :] = xn
        h_out[r0:r0 + TILE, :] = (xn * (1.0 + scale) + shift).astype(h_out.dtype)


def _pro_lru_gate(a_refs):
    hf_ref, hb_ref, y_ref = a_refs
    return ((hf_ref[...] + hb_ref[...]) * y_ref[...]).astype(BF16)


def _mlp_body(h_ref, w1_ref, w2_ref, xold_ref, mod_ref, lng_ref, lnb_ref, x_out, h_out, acc_ref,
              *, nf, ngroups, alpha):
    f = pl.program_id(1)
    u = jnp.maximum(jnp.dot(h_ref[...], w1_ref[...], preferred_element_type=F32), 0.0)
    part = jnp.dot((u * u).astype(BF16), w2_ref[...], preferred_element_type=F32)

    @pl.when(f == 0)
    def _():
        acc_ref[...] = part

    @pl.when(f > 0)
    def _():
        acc_ref[...] += part

    @pl.when(f == nf - 1)
    def _():
        _ep_resid_ln_mod(acc_ref, (xold_ref, mod_ref, lng_ref, lnb_ref), (x_out, h_out),
                         ngroups=ngroups, alpha=alpha)


def _mlp(h, w1, w2, x_old, mod, lng, lnb, *, tm, tf, alpha):
    rows, d = h.shape
    ff = w1.shape[1]
    nf = ff // tf
    row = pl.BlockSpec((tm, d), lambda i, f: (i, 0))
    vec = pl.BlockSpec((1, d), lambda i, f: (0, 0))
    return pl.pallas_call(
        functools.partial(_mlp_body, nf=nf, ngroups=tm // TILE, alpha=alpha),
        grid=(rows // tm, nf),
        in_specs=[row,
                  pl.BlockSpec((d, tf), lambda i, f: (0, f)),
                  pl.BlockSpec((tf, d), lambda i, f: (f, 0)),
                  row,
                  pl.BlockSpec((tm // TILE, 3, d), lambda i, f: (i, 0, 0)),
                  vec, vec],
        out_specs=[row, row],
        out_shape=[jax.ShapeDtypeStruct((rows, d), F32), jax.ShapeDtypeStruct((rows, d), BF16)],
        scratch_shapes=[pltpu.VMEM((tm, d), F32)],
        compiler_params=_cparams(("arbitrary", "arbitrary")),
        name="mlp",
    )(h, w1, w2, x_old, mod, lng.reshape(1, d), lnb.reshape(1, d))


def _ada_body(c_ref, w_ref, b_ref, o_ref):
    cond = c_ref[...]
    sc = (cond * jax.nn.sigmoid(cond)).astype(BF16)
    o_ref[0] = jnp.dot(sc, w_ref[0].astype(BF16), preferred_element_type=F32) + b_ref[0]


def _ada_mods(cond, ada_w, ada_b, tn=1024):
    depth, d, n = ada_w.shape
    rows = cond.shape[0]
    return pl.pallas_call(
        _ada_body,
        grid=(depth, n // tn),
        in_specs=[pl.BlockSpec((rows, d), lambda l, j: (0, 0)),
                  pl.BlockSpec((1, d, tn), lambda l, j: (l, 0, j)),
                  pl.BlockSpec((1, 1, tn), lambda l, j: (l, 0, j))],
        out_specs=pl.BlockSpec((1, rows, tn), lambda l, j: (l, 0, j)),
        out_shape=jax.ShapeDtypeStruct((depth, rows, n), F32),
        compiler_params=_cparams(("arbitrary", "arbitrary")),
        name="ada_mods",
    )(cond, ada_w, ada_b.reshape(depth, 1, n))


def _modulate_body(x_ref, mod_ref, h_ref):
    h_ref[...] = (x_ref[...] * (1.0 + mod_ref[0, 1:2, :]) + mod_ref[0, 0:1, :]).astype(h_ref.dtype)


def _modulate(xs, mod):
    r, d = xs.shape
    return pl.pallas_call(
        _modulate_body,
        grid=(r // TILE,),
        in_specs=[pl.BlockSpec((TILE, d), lambda i: (i, 0)),
                  pl.BlockSpec((1, mod.shape[1], d), lambda i: (i, 0, 0))],
        out_specs=pl.BlockSpec((TILE, d), lambda i: (i, 0)),
        out_shape=jax.ShapeDtypeStruct((r, d), BF16),
        compiler_params=_cparams(("arbitrary",)),
        name="modulate",
    )(xs, mod)


def _dot_nt(a, b):
    return lax.dot_general(a, b, (((1,), (1,)), ((), ())), preferred_element_type=F32)


def _attn_body(q_ref, k_ref, vt_ref, o_ref, s_ref, p_ref, *, group, dq, dv, shared_kv, n_lat_chunks):
    i = pl.program_id(2)

    def kcol(g):
        return 0 if shared_kv else g * dq

    def vrow(g):
        return 0 if shared_kv else g * dv

    def finish(g, acc, l8):
        l = jnp.sum(l8, axis=0, keepdims=True)
        out = acc * (1.0 / l)
        o_ref[:, g * dv:(g + 1) * dv] = out.T.astype(o_ref.dtype)

    def colmax8(x):
        return jnp.max(x.reshape(x.shape[0] // 8, 8, x.shape[1]), axis=0)

    def colsum8(x):
        return jnp.sum(x.reshape(x.shape[0] // 8, 8, x.shape[1]), axis=0)

    @pl.when(i == 0)
    def _():
        for g in range(group):
            st = _dot_nt(k_ref[0:TILE, kcol(g):kcol(g) + dq], q_ref[:, g * dq:(g + 1) * dq])
            m = jnp.max(st, axis=0, keepdims=True)
            p = jnp.exp2(st - m)
            acc = jnp.dot(vt_ref[vrow(g):vrow(g) + dv, 0:TILE], p.astype(BF16), preferred_element_type=F32)
            finish(g, acc, colsum8(p))

    @pl.when(i > 0)
    def _():
        neg = jnp.full((8, TILE), -jnp.inf, F32)
        zero8 = jnp.zeros((8, TILE), F32)
        zacc = jnp.zeros((dv, TILE), F32)
        state = {}
        for ph in range(group + 2):
            ga, gb, gc = ph, ph - 1, ph - 2
            do_a, do_b, do_c = ga < group, 0 <= gb < group, 0 <= gc < group
            q_a = q_ref[:, ga * dq:(ga + 1) * dq] if do_a else None
            m_b = jnp.max(state["m8"], axis=0, keepdims=True) if do_b else None

            def stage(start, size, carry, q_a=q_a, m_b=m_b, ga=ga, gb=gb, gc=gc,
                      do_a=do_a, do_b=do_b, do_c=do_c):
                m8, l8, acc = carry
                if do_a:
                    st = _dot_nt(k_ref[pl.ds(start, size), kcol(ga):kcol(ga) + dq], q_a)
                    s_ref[ga % 2, pl.ds(start, size), :] = st
                    m8 = jnp.maximum(m8, colmax8(st))
                if do_c:
                    acc = acc + jnp.dot(vt_ref[vrow(gc):vrow(gc) + dv, pl.ds(start, size)],
                                        p_ref[gc % 2, pl.ds(start, size), :], preferred_element_type=F32)
                if do_b:
                    p = jnp.exp2(s_ref[gb % 2, pl.ds(start, size), :] - m_b)
                    p_ref[gb % 2, pl.ds(start, size), :] = p.astype(BF16)
                    l8 = l8 + colsum8(p)
                return m8, l8, acc

            carry = stage(0, TILE, (neg, zero8, zacc))

            def body(c, carry, stage=stage):
                return stage(pl.multiple_of(TILE + c * KV_CHUNK, TILE), KV_CHUNK, carry)

            m8, l8, acc = lax.fori_loop(0, n_lat_chunks, body, carry, unroll=True)
            if do_c:
                finish(gc, acc, state["l8"])
            if do_b:
                state["l8"] = l8
            if do_a:
                state["m8"] = m8


def _attention(q, k, vt, *, batch, n_tiles, n_steps, group, dq, dv, shared_kv):
    r = q.shape[0]
    t_len = n_tiles * TILE
    kvw = 1 if shared_kv else group
    return pl.pallas_call(
        functools.partial(_attn_body, group=group, dq=dq, dv=dv, shared_kv=shared_kv,
                          n_lat_chunks=(t_len - TILE) // KV_CHUNK),
        grid=(batch, n_steps, n_tiles),
        in_specs=[pl.BlockSpec((TILE, group * dq), lambda b, h, i: (b * n_tiles + i, h)),
                  pl.BlockSpec((t_len, kvw * dq), lambda b, h, i: (b, h), pipeline_mode=pl.Buffered(1)),
                  pl.BlockSpec((None, kvw * dv, t_len), lambda b, h, i: (b, h, 0), pipeline_mode=pl.Buffered(1))],
        out_specs=pl.BlockSpec((TILE, group * dv), lambda b, h, i: (b * n_tiles + i, h)),
        out_shape=jax.ShapeDtypeStruct((r, n_steps * group * dv), BF16),
        scratch_shapes=[pltpu.VMEM((2, t_len, TILE), F32), pltpu.VMEM((2, t_len, TILE), BF16)],
        compiler_params=_cparams(("arbitrary", "arbitrary", "arbitrary")),
        name="attention",
    )(q, k, vt)


def _conv_body(x_ref, prev_ref, next_ref, w_ref, b_ref, u_ref, ext_ref, *, n_tiles):
    i = pl.program_id(0)
    seg = i % n_tiles
    has_prev = seg > 1
    has_next = jnp.logical_and(seg > 0, seg < n_tiles - 1)
    ext_ref[0:8, :] = jnp.where(has_prev, prev_ref[...], 0.0)
    ext_ref[8:8 + TILE, :] = x_ref[...]
    ext_ref[8 + TILE:16 + TILE, :] = jnp.where(has_next, next_ref[...], 0.0)
    left = CONV_W // 2
    acc = b_ref[...] + ext_ref[pl.ds(8 - left, TILE), :] * w_ref[0:1, :]
    for j in range(1, CONV_W):
        acc = acc + ext_ref[pl.ds(8 - left + j, TILE), :] * w_ref[j:j + 1, :]
    u_ref[...] = acc


def _dwconv(xw, conv_w, conv_b, n_tiles):
    r, w = xw.shape
    sub = TILE // 8
    n_sub = r // 8
    return pl.pallas_call(
        functools.partial(_conv_body, n_tiles=n_tiles),
        grid=(r // TILE,),
        in_specs=[pl.BlockSpec((TILE, w), lambda i: (i, 0)),
                  pl.BlockSpec((8, w), lambda i: (jnp.maximum(i * sub - 1, 0), 0)),
                  pl.BlockSpec((8, w), lambda i: (jnp.minimum((i + 1) * sub, n_sub - 1), 0)),
                  pl.BlockSpec((CONV_W, w), lambda i: (0, 0)),
                  pl.BlockSpec((1, w), lambda i: (0, 0))],
        out_specs=pl.BlockSpec((TILE, w), lambda i: (i, 0)),
        out_shape=jax.ShapeDtypeStruct((r, w), F32),
        scratch_shapes=[pltpu.VMEM((TILE + 16, w), F32)],
        compiler_params=_cparams(("arbitrary",)),
        name="dwconv",
    )(xw, xw, xw, conv_w, conv_b.reshape(1, w))


def _log_sigmoid(x):
    return jnp.minimum(x, 0.0) - jnp.log1p(jnp.exp(-jnp.abs(x)))


def _scan_tile(a, b, reverse):
    n = a.shape[0]
    row = lax.broadcasted_iota(jnp.int32, a.shape, 0)
    s = 1
    while s < n:
        if reverse:
            a_sh = pltpu.roll(a, n - s, 0)
            b_sh = pltpu.roll(b, n - s, 0)
            valid = row < n - s
        else:
            a_sh = pltpu.roll(a, s, 0)
            b_sh = pltpu.roll(b, s, 0)
            valid = row >= s
        b = jnp.where(valid, b + a * b_sh, b)
        a = jnp.where(valid, a * a_sh, a)
        s *= 2
    return a, b


def _lru_body(uf_ref, ub_ref, raw_ref, ixw_ref, rab_ref, ixb_ref, lam_ref,
              hf_ref, hb_ref, carry_ref, *, blk):
    j = pl.program_id(1)

    @pl.when(j == 0)
    def _():
        carry_ref[...] = jnp.zeros_like(carry_ref)

    for d, (u_ref, h_ref) in enumerate(((uf_ref, hf_ref), (ub_ref, hb_ref))):
        lsl = LRU_C * _log_sigmoid(lam_ref[d])
        for n in range(LRU_BLOCKS):
            cs = slice(n * blk, (n + 1) * blk)
            u = u_ref[:, cs]
            ub16 = u.astype(BF16)
            r = jax.nn.sigmoid(jnp.dot(ub16, raw_ref[d, n], preferred_element_type=F32) + rab_ref[d][:, cs])
            gi = jax.nn.sigmoid(jnp.dot(ub16, ixw_ref[d, n], preferred_element_type=F32) + ixb_ref[d][:, cs])
            log_a = r * lsl[:, cs]
            a = jnp.exp(log_a)
            bx = jnp.sqrt(1.0 - a * a) * (gi * u)
            a_cum, h_loc = _scan_tile(a, bx, reverse=(d == 1))
            h = h_loc + a_cum * carry_ref[d, 0:1, cs]
            h_ref[:, cs] = h
            last = h[0:1, :] if d == 1 else h[TILE - 1:TILE, :]
            carry_ref[d, :, cs] = jnp.broadcast_to(last, (8, blk))


def _lru_scan(u, ra_w, ix_w, ra_b, ix_b, lam, *, batch, n_tiles):
    r, w = u.shape
    blk = w // LRU_BLOCKS

    def fwd_map(b, j):
        return (b * n_tiles + j, 0)

    def bwd_map(b, j):
        return (b * n_tiles + jnp.where(j == 0, 0, n_tiles - j), 0)

    full5 = pl.BlockSpec((2, LRU_BLOCKS, blk, blk), lambda b, j: (0, 0, 0, 0))
    vec = pl.BlockSpec((2, 1, w), lambda b, j: (0, 0, 0))
    return pl.pallas_call(
        functools.partial(_lru_body, blk=blk),
        grid=(batch, n_tiles),
        in_specs=[pl.BlockSpec((TILE, w), fwd_map), pl.BlockSpec((TILE, w), bwd_map),
                  full5, full5, vec, vec, vec],
        out_specs=[pl.BlockSpec((TILE, w), fwd_map), pl.BlockSpec((TILE, w), bwd_map)],
        out_shape=[jax.ShapeDtypeStruct((r, w), F32), jax.ShapeDtypeStruct((r, w), F32)],
        scratch_shapes=[pltpu.VMEM((2, 8, w), F32)],
        compiler_params=_cparams(("arbitrary", "arbitrary")),
        name="lru_scan",
    )(u, u, ra_w, ix_w, ra_b.reshape(2, 1, w), ix_b.reshape(2, 1, w), lam.reshape(2, 1, w))


def _rope_tables(n_lat, rot_dim):
    m = rot_dim // 2
    half = m // 2
    t = np.arange(n_lat)
    pos = np.stack([t // GRID_W, t % GRID_W], axis=1).astype(np.float32)
    inv = (ROPE_THETA ** (-(np.arange(half, dtype=np.float32) * 2.0) / m)).astype(np.float32)
    lane = np.arange(LANE)
    axis = np.minimum(lane // m, 1)
    freq = inv[lane % half]
    ang = pos[:, axis] * freq[None, :]
    live = (lane < rot_dim)[None, :]
    first = ((lane % m) < half)[None, :]
    cos = np.where(live, np.cos(ang), 1.0)
    sin = np.where(live, np.sin(ang), 0.0)
    s_hi = np.where(first, -sin, 0.0)
    s_lo = np.where(first, 0.0, sin)
    ident = np.concatenate([np.ones((TILE, LANE)), np.zeros((TILE, LANE)), np.zeros((TILE, LANE))], 1)
    tab = np.concatenate([cos, s_hi, s_lo], axis=1)
    tab = np.concatenate([ident, tab], axis=0).astype(np.float32)
    return (jnp.asarray(tab[:, :LANE]), jnp.asarray(tab[:, LANE:2 * LANE]), jnp.asarray(tab[:, 2 * LANE:]))


def kernel(x, c, ctx, c_ctx, ada_w, ada_b, ln_g, ln_b, mlp_w1, mlp_w2, gqa_wq, gqa_wk, gqa_wv, gqa_wo, gqa_q_g, gqa_k_g, lru_wx, lru_wy, lru_conv_w, lru_conv_b, lru_ra_w, lru_ra_b, lru_ix_w, lru_ix_b, lru_lam, lru_wo, mla_wq_a, mla_q_a_g, mla_wq_b, mla_wkv_a, mla_kv_a_g, mla_wkv_b, mla_wo):
    batch, n_lat, d = x.shape
    depth = ada_w.shape[0]
    assert ctx.shape[1] == TILE and n_lat % KV_CHUNK == 0 and n_lat % GRID_W == 0
    t_len = TILE + n_lat
    n_tiles = t_len // TILE
    rows = batch * t_len
    n_groups = rows // TILE
    alpha = (2 * depth) ** 0.25

    xs = jnp.concatenate([ctx, x], axis=1).reshape(rows, d)

    cond_rows = -(-(batch + 1) // 8) * 8
    cond = jnp.concatenate([c, c_ctx[None, :], jnp.zeros((cond_rows - batch - 1, d), F32)], axis=0)
    mods = _ada_mods(cond, ada_w, ada_b)
    m_lat = jnp.broadcast_to(mods[:, :batch, None, :], (depth, batch, n_tiles - 1, 6 * d))
    m_ctx = jnp.broadcast_to(mods[:, batch:batch + 1, None, :], (depth, batch, 1, 6 * d))
    mods = jnp.concatenate([m_ctx, m_lat], axis=2).reshape(depth, n_groups, 6, d)

    def mod3(gate, scale, shift):
        return jnp.stack([gate, scale, shift], axis=1)

    tabs_gqa = _rope_tables(n_lat, A_HEAD_DIM)
    tabs_mla = _rope_tables(n_lat, MLA_ROPE)

    def tab_specs():
        return [pl.BlockSpec((TILE, LANE), lambda j, i, k: (i % n_tiles, 0)) for _ in range(3)]

    def vt_spec(n):
        return pl.BlockSpec((None, n, TILE), lambda j, i, k: (i // n_tiles, 0, i % n_tiles))

    def mod_spec(tm):
        return pl.BlockSpec((tm // TILE, 3, d), lambda j, i, k: (i, 0, 0))

    def resid_ln(a_list, w, x_old, mod, lng, lnb, *, tm, tk, prologue=None, name=None):
        ex = [(x_old, _row_spec(tm, d)), (mod, mod_spec(tm)),
              (lng.reshape(1, d), _full_spec((1, d))), (lnb.reshape(1, d), _full_spec((1, d)))]
        outs = [(jax.ShapeDtypeStruct((rows, d), F32), _out_spec(tm, d)),
                (jax.ShapeDtypeStruct((rows, d), BF16), _out_spec(tm, d))]
        return _matmul(a_list, w, ex, outs, tm=tm, tn=d, tk=tk, prologue=prologue, name=name,
                       epilogue=functools.partial(_ep_resid_ln_mod, ngroups=tm // TILE, alpha=alpha))

    def plain(a, w, *, tm, tn, epilogue=_ep_cast, dtype=BF16, name=None):
        n = w.shape[1]
        outs = [(jax.ShapeDtypeStruct((a.shape[0], n), dtype), _out_spec(tm, tn))]
        return _matmul([a], w, [], outs, tm=tm, tn=tn, tk=a.shape[1], epilogue=epilogue, name=name)[0]

    h = _modulate(xs, mods[0])

    for i in range(depth):
        kind = i % 3
        slot = i // 3
        m_i = mods[i]
        if kind == 0:
            wq, wk, wv, wo = (gqa_wq[slot].astype(BF16), gqa_wk[slot].astype(BF16),
                              gqa_wv[slot].astype(BF16), gqa_wo[slot].astype(BF16))

            def qk_proj(w, gain, scale, name):
                n = w.shape[1]
                ex = [(gain.reshape(1, LANE), _full_spec((1, LANE)))] + list(zip(tabs_gqa, tab_specs()))
                outs = [(jax.ShapeDtypeStruct((rows, n), BF16), _out_spec(TILE, n))]
                ep = functools.partial(_ep_headnorm_rope, nblk=n // LANE, off=A_HEAD_DIM // 4, scale=scale)
                return _matmul([h], w, ex, outs, tm=TILE, tn=n, tk=d, epilogue=ep, name=name)[0]

            q = qk_proj(wq, gqa_q_g[slot], A_HEAD_DIM ** -0.5 * LOG2E, "gqa_q")
            k = qk_proj(wk, gqa_k_g[slot], 1.0, "gqa_k")
            nv = wv.shape[1]
            vt = _matmul([h], wv, [], [(jax.ShapeDtypeStruct((batch, nv, t_len), BF16), vt_spec(nv))],
                         tm=TILE, tn=nv, tk=d, epilogue=_ep_cast_t, name="gqa_v")[0]
            o = _attention(q, k, vt, batch=batch, n_tiles=n_tiles, n_steps=A_KV_HEADS, group=A_GROUP,
                           dq=A_HEAD_DIM, dv=A_HEAD_DIM, shared_kv=True)
            a_list, w_out, prologue = [o], wo, None
        elif kind == 1:
            wx, wy, wo = lru_wx[slot].astype(BF16), lru_wy[slot].astype(BF16), lru_wo[slot].astype(BF16)
            xw = plain(h, wx, tm=2 * TILE, tn=wx.shape[1], dtype=F32, name="lru_x")
            y = plain(h, wy, tm=2 * TILE, tn=wy.shape[1], epilogue=_ep_gelu, dtype=F32, name="lru_y")
            u = _dwconv(xw, lru_conv_w[slot], lru_conv_b[slot], n_tiles)
            hf, hb = _lru_scan(u, lru_ra_w[slot].astype(BF16), lru_ix_w[slot].astype(BF16),
                               lru_ra_b[slot], lru_ix_b[slot], lru_lam[slot],
                               batch=batch, n_tiles=n_tiles)
            a_list, w_out, prologue = [hf, hb, y], wo, _pro_lru_gate
        else:
            nq = MLA_NOPE + MLA_ROPE
            wq_a = mla_wq_a[slot].astype(BF16)
            wq_b = mla_wq_b[slot].reshape(MLA_LORA, MLA_HEADS, nq)
            wq_b = jnp.concatenate([wq_b, jnp.zeros((MLA_LORA, MLA_HEADS, 2 * LANE - nq), F32)], axis=-1)
            wq_b = wq_b.reshape(MLA_LORA, MLA_HEADS * 2 * LANE).astype(BF16)
            wkv_a = jnp.concatenate([mla_wkv_a[slot], jnp.zeros((d, LANE - MLA_ROPE), F32)], axis=-1).astype(BF16)
            wkv_b = mla_wkv_b[slot].reshape(MLA_LORA, MLA_HEADS, MLA_NOPE + MLA_V)
            wkv_b = jnp.concatenate([wkv_b[:, :, :MLA_NOPE].reshape(MLA_LORA, -1),
                                     wkv_b[:, :, MLA_NOPE:].reshape(MLA_LORA, -1)], axis=-1).astype(BF16)
            wo = mla_wo[slot].astype(BF16)

            qa = _matmul([h], wq_a, [(mla_q_a_g[slot].reshape(1, MLA_LORA), _full_spec((1, MLA_LORA)))],
                         [(jax.ShapeDtypeStruct((rows, MLA_LORA), BF16), _out_spec(2 * TILE, MLA_LORA))],
                         tm=2 * TILE, tn=MLA_LORA, tk=d, epilogue=_ep_rmsnorm, name="mla_qa")[0]
            nqb = wq_b.shape[1]
            q = _matmul([qa], wq_b, list(zip(tabs_mla, tab_specs())),
                        [(jax.ShapeDtypeStruct((rows, nqb), BF16), _out_spec(TILE, nqb))],
                        tm=TILE, tn=nqb, tk=MLA_LORA, name="mla_q",
                        epilogue=functools.partial(_ep_mla_q, nheads=MLA_HEADS, off=MLA_ROPE // 4,
                                                   scale=float(nq) ** -0.5 * LOG2E))[0]
            nkva = wkv_a.shape[1]
            ex = [(mla_kv_a_g[slot].reshape(1, MLA_LORA), _full_spec((1, MLA_LORA)))] + list(zip(tabs_mla, tab_specs()))
            kva = _matmul([h], wkv_a, ex,
                          [(jax.ShapeDtypeStruct((rows, nkva), BF16), _out_spec(TILE, nkva))],
                          tm=TILE, tn=nkva, tk=d, name="mla_kva",
                          epilogue=functools.partial(_ep_mla_kva, lora=MLA_LORA, off=MLA_ROPE // 4))[0]
            nv = MLA_HEADS * MLA_V
            kfull, vt = _matmul(
                [kva], wkv_b,
                [(kva, pl.BlockSpec((TILE, LANE), lambda j, ii, kk: (ii, MLA_LORA // LANE)))],
                [(jax.ShapeDtypeStruct((rows, MLA_HEADS * 2 * LANE), BF16), _out_spec(TILE, MLA_HEADS * 2 * LANE)),
                 (jax.ShapeDtypeStruct((batch, nv, t_len), BF16), vt_spec(nv))],
                tm=TILE, tn=wkv_b.shape[1], tk=MLA_LORA, kdim=MLA_LORA, name="mla_kv",
                epilogue=functools.partial(_ep_mla_kv, nheads=MLA_HEADS))
            o = _attention(q, kfull, vt, batch=batch, n_tiles=n_tiles, n_steps=MLA_HEADS // MLA_STEP_HEADS,
                           group=MLA_STEP_HEADS, dq=2 * LANE, dv=MLA_V, shared_kv=False)
            a_list, w_out, prologue = [o], wo, None

        xs, h2 = resid_ln(a_list, w_out, xs, mod3(m_i[:, 2], m_i[:, 4], m_i[:, 3]),
                          ln_g[i, 0], ln_b[i, 0], tm=TILE, tk=w_out.shape[0], prologue=prologue,
                          name="mix_out")
        w1 = mlp_w1[i].astype(BF16)
        w2 = mlp_w2[i].astype(BF16)
        if i + 1 < depth:
            nxt = mod3(m_i[:, 5], mods[i + 1][:, 1], mods[i + 1][:, 0])
        else:
            zero = jnp.zeros_like(m_i[:, 5])
            nxt = mod3(m_i[:, 5], zero, zero)
        xs, h = _mlp(h2, w1, w2, xs, nxt, ln_g[i, 1], ln_b[i, 1], tm=2 * TILE, tf=1024, alpha=alpha)

    return xs.reshape(batch, t_len, d)[:, TILE:, :]
```

```python
import functools
import math

import numpy as np
import jax
import jax.numpy as jnp
from jax import lax
from jax.experimental import pallas as pl
from jax.experimental.pallas import tpu as pltpu

F32 = jnp.float32
BF16 = jnp.bfloat16

TILE = 256
GRID_W = 64
ROPE_THETA = 10000.0
EPS = 1e-6
LANE = 128
VMEM_LIMIT_MB = 56

A_HEADS = 16
A_KV_HEADS = 4
A_GROUP = A_HEADS // A_KV_HEADS
A_HEAD_DIM = 128

LRU_BLOCKS = 8
CONV_W = 4
LRU_C = 8.0

MLA_HEADS = 16
MLA_LORA = 512
MLA_NOPE = 128
MLA_ROPE = 64
MLA_V = 128

KV_CHUNK = 512
GQA_STEP_HEADS = 8
MLA_STEP_HEADS = 4
LOG2E = math.log2(math.e)


def _cparams(sem, vmem_mb=VMEM_LIMIT_MB):
    return pltpu.CompilerParams(dimension_semantics=sem,
                                vmem_limit_bytes=vmem_mb * 1024 * 1024)


def _mm_body(*refs, n_a, n_ex, n_out, nk, prologue, epilogue):
    a_refs = refs[:n_a]
    w_ref = refs[n_a]
    ex = refs[n_a + 1:n_a + 1 + n_ex]
    outs = refs[n_a + 1 + n_ex:n_a + 1 + n_ex + n_out]
    scratch = refs[n_a + 1 + n_ex + n_out:]
    if prologue is None:
        a = a_refs[0][...]
    else:
        a = prologue(a_refs)
    part = jnp.dot(a, w_ref[...], preferred_element_type=F32)
    if nk == 1:
        epilogue(part, ex, outs)
    else:
        acc_ref = scratch[0]
        k = pl.program_id(2)

        @pl.when(k == 0)
        def _():
            acc_ref[...] = part

        @pl.when(k > 0)
        def _():
            acc_ref[...] += part

        @pl.when(k == nk - 1)
        def _():
            epilogue(acc_ref, ex, outs)


def _matmul(a_list, w, extras, outs, *, tm, tn, tk, epilogue, prologue=None, name=None, kdim=None):
    m = a_list[0].shape[0]
    kdim = a_list[0].shape[1] if kdim is None else kdim
    n = w.shape[1]
    assert m % tm == 0 and n % tn == 0 and kdim % tk == 0
    nk = kdim // tk
    grid = (n // tn, m // tm, nk)
    in_specs = [pl.BlockSpec((tm, tk), lambda j, i, k: (i, k)) for _ in a_list]
    in_specs.append(pl.BlockSpec((tk, tn), lambda j, i, k: (k, j)))
    in_specs += [s for _, s in extras]
    scratch = [pltpu.VMEM((tm, tn), F32)] if nk > 1 else []
    body = functools.partial(_mm_body, n_a=len(a_list), n_ex=len(extras), n_out=len(outs),
                             nk=nk, prologue=prologue, epilogue=epilogue)
    res = pl.pallas_call(
        body,
        grid=grid,
        in_specs=in_specs,
        out_specs=[s for _, s in outs],
        out_shape=[s for s, _ in outs],
        scratch_shapes=scratch,
        compiler_params=_cparams(("arbitrary", "arbitrary", "arbitrary")),
        name=name,
    )(*a_list, w, *[arr for arr, _ in extras])
    return res


def _row_spec(tm, width):
    return pl.BlockSpec((tm, width), lambda j, i, k: (i, 0))


def _out_spec(tm, tn):
    return pl.BlockSpec((tm, tn), lambda j, i, k: (i, j))


def _full_spec(shape):
    zeros = (0,) * len(shape)
    return pl.BlockSpec(shape, lambda j, i, k: zeros)


def _ep_cast(acc, ex, outs):
    outs[0][...] = acc[...].astype(outs[0].dtype)


def _ep_cast_t(acc, ex, outs):
    outs[0][...] = acc[...].T.astype(outs[0].dtype)


def _ep_gelu(acc, ex, outs):
    outs[0][...] = jax.nn.gelu(acc[...], approximate=True).astype(outs[0].dtype)


def _rope(x, cos, s_hi, s_lo, off):
    return x * cos + pltpu.roll(x, LANE - off, 1) * s_hi + pltpu.roll(x, off, 1) * s_lo


def _ep_headnorm_rope(acc, ex, outs, *, nblk, off, scale):
    g_ref, cos_ref, shi_ref, slo_ref = ex
    g = g_ref[...]
    cos, shi, slo = cos_ref[...], shi_ref[...], slo_ref[...]
    for j in range(nblk):
        x = acc[:, j * LANE:(j + 1) * LANE]
        ms = jnp.mean(x * x, axis=-1, keepdims=True)
        xn = x * lax.rsqrt(ms + EPS) * g
        y = _rope(xn, cos, shi, slo, off)
        if scale != 1.0:
            y = y * scale
        outs[0][:, j * LANE:(j + 1) * LANE] = y.astype(outs[0].dtype)


def _ep_rmsnorm(acc, ex, outs):
    g_ref, = ex
    x = acc[...]
    ms = jnp.mean(x * x, axis=-1, keepdims=True)
    outs[0][...] = (x * lax.rsqrt(ms + EPS) * g_ref[...]).astype(outs[0].dtype)


def _ep_mla_kva(acc, ex, outs, *, lora, off):
    g_ref, cos_ref, shi_ref, slo_ref = ex
    x = acc[:, 0:lora]
    ms = jnp.mean(x * x, axis=-1, keepdims=True)
    outs[0][:, 0:lora] = (x * lax.rsqrt(ms + EPS) * g_ref[...]).astype(outs[0].dtype)
    pe = acc[:, lora:lora + LANE]
    outs[0][:, lora:lora + LANE] = _rope(pe, cos_ref[...], shi_ref[...], slo_ref[...], off).astype(outs[0].dtype)


def _ep_mla_q(acc, ex, outs, *, nheads, off, scale):
    cos_ref, shi_ref, slo_ref = ex
    cos, shi, slo = cos_ref[...], shi_ref[...], slo_ref[...]
    for h in range(nheads):
        c0 = h * 2 * LANE
        outs[0][:, c0:c0 + LANE] = (acc[:, c0:c0 + LANE] * scale).astype(outs[0].dtype)
        pe = acc[:, c0 + LANE:c0 + 2 * LANE]
        outs[0][:, c0 + LANE:c0 + 2 * LANE] = (_rope(pe, cos, shi, slo, off) * scale).astype(outs[0].dtype)


def _ep_mla_kv(acc, ex, outs, *, nheads):
    kpe_ref, = ex
    k_out, vt_out = outs
    kpe = kpe_ref[...]
    for h in range(nheads):
        k_out[:, 2 * h * LANE:(2 * h + 1) * LANE] = acc[:, h * LANE:(h + 1) * LANE].astype(k_out.dtype)
        k_out[:, (2 * h + 1) * LANE:(2 * h + 2) * LANE] = kpe
    vt_out[...] = acc[:, nheads * LANE:2 * nheads * LANE].T.astype(vt_out.dtype)


def _ep_resid_ln_mod(acc, ex, outs, *, ngroups, alpha):
    xold_ref, mod_ref, lng_ref, lnb_ref = ex
    x_out, h_out = outs
    lng, lnb = lng_ref[...], lnb_ref[...]
    for g in range(ngroups):
        r0 = g * TILE
        gate = mod_ref[g, 0:1, :]
        scale = mod_ref[g, 1:2, :]
        shift = mod_ref[g, 2:3, :]
        y = alpha * xold_ref[r0:r0 + TILE, :] + (1.0 + gate) * acc[r0:r0 + TILE, :]
        mu = jnp.mean(y, axis=-1, keepdims=True)
        yc = y - mu
        var = jnp.mean(yc * yc, axis=-1, keepdims=True)
        xn = yc * lax.rsqrt(var + EPS) * lng + lnb
        x_out[r0:r0 + TILE, :] = xn
        h_out[r0:r0 + TILE, :] = (xn * (1.0 + scale) + shift).astype(h_out.dtype)


def _pro_lru_gate(a_refs):
    hf_ref, hb_ref, y_ref = a_refs
    return ((hf_ref[...] + hb_ref[...]) * y_ref[...]).astype(BF16)


def _mlp_body(h_ref, w1_ref, w2_ref, xold_ref, mod_ref, lng_ref, lnb_ref, x_out, h_out, acc_ref,
              *, nf, ngroups, alpha):
    f = pl.program_id(1)

    @pl.when(f == 0)
    def _():
        acc_ref[...] = jnp.zeros_like(acc_ref)

    u = jnp.maximum(jnp.dot(h_ref[...], w1_ref[...], preferred_element_type=F32), 0.0)
    acc_ref[...] += jnp.dot((u * u).astype(BF16), w2_ref[...], preferred_element_type=F32)

    @pl.when(f == nf - 1)
    def _():
        _ep_resid_ln_mod(acc_ref, (xold_ref, mod_ref, lng_ref, lnb_ref), (x_out, h_out),
                         ngroups=ngroups, alpha=alpha)


def _mlp(h, w1, w2, x_old, mod, lng, lnb, *, tm, tf, alpha):
    rows, d = h.shape
    ff = w1.shape[1]
    nf = ff // tf
    row = pl.BlockSpec((tm, d), lambda i, f: (i, 0))
    vec = pl.BlockSpec((1, d), lambda i, f: (0, 0))
    return pl.pallas_call(
        functools.partial(_mlp_body, nf=nf, ngroups=tm // TILE, alpha=alpha),
        grid=(rows // tm, nf),
        in_specs=[row,
                  pl.BlockSpec((d, tf), lambda i, f: (0, f)),
                  pl.BlockSpec((tf, d), lambda i, f: (f, 0)),
                  row,
                  pl.BlockSpec((tm // TILE, 3, d), lambda i, f: (i, 0, 0)),
                  vec, vec],
        out_specs=[row, row],
        out_shape=[jax.ShapeDtypeStruct((rows, d), F32), jax.ShapeDtypeStruct((rows, d), BF16)],
        scratch_shapes=[pltpu.VMEM((tm, d), F32)],
        compiler_params=_cparams(("arbitrary", "arbitrary")),
        name="mlp",
    )(h, w1, w2, x_old, mod, lng.reshape(1, d), lnb.reshape(1, d))


def _ada_body(c_ref, w_ref, b_ref, o_ref):
    cond = c_ref[...]
    sc = (cond * jax.nn.sigmoid(cond)).astype(BF16)
    o_ref[0] = jnp.dot(sc, w_ref[0].astype(BF16), preferred_element_type=F32) + b_ref[0]


def _ada_mods(cond, ada_w, ada_b, tn=1024):
    depth, d, n = ada_w.shape
    rows = cond.shape[0]
    return pl.pallas_call(
        _ada_body,
        grid=(depth, n // tn),
        in_specs=[pl.BlockSpec((rows, d), lambda l, j: (0, 0)),
                  pl.BlockSpec((1, d, tn), lambda l, j: (l, 0, j)),
                  pl.BlockSpec((1, 1, tn), lambda l, j: (l, 0, j))],
        out_specs=pl.BlockSpec((1, rows, tn), lambda l, j: (l, 0, j)),
        out_shape=jax.ShapeDtypeStruct((depth, rows, n), F32),
        compiler_params=_cparams(("arbitrary", "arbitrary")),
        name="ada_mods",
    )(cond, ada_w, ada_b.reshape(depth, 1, n))


def _modulate_body(x_ref, mod_ref, h_ref):
    h_ref[...] = (x_ref[...] * (1.0 + mod_ref[0, 1:2, :]) + mod_ref[0, 0:1, :]).astype(h_ref.dtype)


def _modulate(xs, mod):
    r, d = xs.shape
    return pl.pallas_call(
        _modulate_body,
        grid=(r // TILE,),
        in_specs=[pl.BlockSpec((TILE, d), lambda i: (i, 0)),
                  pl.BlockSpec((1, mod.shape[1], d), lambda i: (i, 0, 0))],
        out_specs=pl.BlockSpec((TILE, d), lambda i: (i, 0)),
        out_shape=jax.ShapeDtypeStruct((r, d), BF16),
        compiler_params=_cparams(("arbitrary",)),
        name="modulate",
    )(xs, mod)


def _dot_nt(a, b):
    return lax.dot_general(a, b, (((1,), (1,)), ((), ())), preferred_element_type=F32)


def _attn_body(q_ref, k_ref, vt_ref, o_ref, s_ref, p_ref, *, group, dq, dv, q_per_kv, n_lat_chunks):
    i = pl.program_id(2)

    def kcol(g):
        return (g // q_per_kv) * dq

    def vrow(g):
        return (g // q_per_kv) * dv

    def finish(g, acc, l8):
        l = jnp.sum(l8, axis=0, keepdims=True)
        out = acc * (1.0 / l)
        o_ref[:, g * dv:(g + 1) * dv] = out.T.astype(o_ref.dtype)

    def colmax8(x):
        return jnp.max(x.reshape(x.shape[0] // 8, 8, x.shape[1]), axis=0)

    def colsum8(x):
        return jnp.sum(x.reshape(x.shape[0] // 8, 8, x.shape[1]), axis=0)

    @pl.when(i == 0)
    def _():
        for g in range(group):
            st = _dot_nt(k_ref[0:TILE, kcol(g):kcol(g) + dq], q_ref[:, g * dq:(g + 1) * dq])
            m = jnp.max(st, axis=0, keepdims=True)
            p = jnp.exp2(st - m)
            acc = jnp.dot(vt_ref[vrow(g):vrow(g) + dv, 0:TILE], p.astype(BF16), preferred_element_type=F32)
            finish(g, acc, colsum8(p))

    @pl.when(i > 0)
    def _():
        neg = jnp.full((8, TILE), -jnp.inf, F32)
        zero8 = jnp.zeros((8, TILE), F32)
        zacc = jnp.zeros((dv, TILE), F32)
        state = {}
        for ph in range(group + 2):
            ga, gb, gc = ph, ph - 1, ph - 2
            do_a, do_b, do_c = ga < group, 0 <= gb < group, 0 <= gc < group
            q_a = q_ref[:, ga * dq:(ga + 1) * dq] if do_a else None
            m_b = jnp.max(state["m8"], axis=0, keepdims=True) if do_b else None

            def stage(start, size, carry, q_a=q_a, m_b=m_b, ga=ga, gb=gb, gc=gc,
                      do_a=do_a, do_b=do_b, do_c=do_c):
                m8, l8, acc = carry
                if do_a:
                    st = _dot_nt(k_ref[pl.ds(start, size), kcol(ga):kcol(ga) + dq], q_a)
                    s_ref[ga % 2, pl.ds(start, size), :] = st
                    m8 = jnp.maximum(m8, colmax8(st))
                if do_c:
                    acc = acc + jnp.dot(vt_ref[vrow(gc):vrow(gc) + dv, pl.ds(start, size)],
                                        p_ref[gc % 2, pl.ds(start, size), :], preferred_element_type=F32)
                if do_b:
                    p = jnp.exp2(s_ref[gb % 2, pl.ds(start, size), :] - m_b)
                    p_ref[gb % 2, pl.ds(start, size), :] = p.astype(BF16)
                    l8 = l8 + colsum8(p)
                return m8, l8, acc

            carry = stage(0, TILE, (neg, zero8, zacc))

            def body(c, carry, stage=stage):
                return stage(pl.multiple_of(TILE + c * KV_CHUNK, TILE), KV_CHUNK, carry)

            m8, l8, acc = lax.fori_loop(0, n_lat_chunks, body, carry, unroll=True)
            if do_c:
                finish(gc, acc, state["l8"])
            if do_b:
                state["l8"] = l8
            if do_a:
                state["m8"] = m8


def _attention(q, k, vt, *, batch, n_tiles, n_steps, group, dq, dv, q_per_kv):
    r = q.shape[0]
    t_len = n_tiles * TILE
    kvw = group // q_per_kv
    return pl.pallas_call(
        functools.partial(_attn_body, group=group, dq=dq, dv=dv, q_per_kv=q_per_kv,
                          n_lat_chunks=(t_len - TILE) // KV_CHUNK),
        grid=(batch, n_steps, n_tiles),
        in_specs=[pl.BlockSpec((TILE, group * dq), lambda b, h, i: (b * n_tiles + i, h)),
                  pl.BlockSpec((t_len, kvw * dq), lambda b, h, i: (b, h), pipeline_mode=pl.Buffered(1)),
                  pl.BlockSpec((None, kvw * dv, t_len), lambda b, h, i: (b, h, 0), pipeline_mode=pl.Buffered(1))],
        out_specs=pl.BlockSpec((TILE, group * dv), lambda b, h, i: (b * n_tiles + i, h)),
        out_shape=jax.ShapeDtypeStruct((r, n_steps * group * dv), BF16),
        scratch_shapes=[pltpu.VMEM((2, t_len, TILE), F32), pltpu.VMEM((2, t_len, TILE), BF16)],
        compiler_params=_cparams(("arbitrary", "arbitrary", "arbitrary")),
        name="attention",
    )(q, k, vt)


def _conv_body(x_ref, prev_ref, next_ref, w_ref, b_ref, u_ref, ext_ref, *, n_tiles):
    i = pl.program_id(0)
    seg = i % n_tiles
    has_prev = seg > 1
    has_next = jnp.logical_and(seg > 0, seg < n_tiles - 1)
    ext_ref[0:8, :] = jnp.where(has_prev, prev_ref[...], 0.0)
    ext_ref[8:8 + TILE, :] = x_ref[...]
    ext_ref[8 + TILE:16 + TILE, :] = jnp.where(has_next, next_ref[...], 0.0)
    left = CONV_W // 2
    acc = b_ref[...] + ext_ref[pl.ds(8 - left, TILE), :] * w_ref[0:1, :]
    for j in range(1, CONV_W):
        acc = acc + ext_ref[pl.ds(8 - left + j, TILE), :] * w_ref[j:j + 1, :]
    u_ref[...] = acc


def _dwconv(xw, conv_w, conv_b, n_tiles):
    r, w = xw.shape
    sub = TILE // 8
    n_sub = r // 8
    return pl.pallas_call(
        functools.partial(_conv_body, n_tiles=n_tiles),
        grid=(r // TILE,),
        in_specs=[pl.BlockSpec((TILE, w), lambda i: (i, 0)),
                  pl.BlockSpec((8, w), lambda i: (jnp.maximum(i * sub - 1, 0), 0)),
                  pl.BlockSpec((8, w), lambda i: (jnp.minimum((i + 1) * sub, n_sub - 1), 0)),
                  pl.BlockSpec((CONV_W, w), lambda i: (0, 0)),
                  pl.BlockSpec((1, w), lambda i: (0, 0))],
        out_specs=pl.BlockSpec((TILE, w), lambda i: (i, 0)),
        out_shape=jax.ShapeDtypeStruct((r, w), F32),
        scratch_shapes=[pltpu.VMEM((TILE + 16, w), F32)],
        compiler_params=_cparams(("arbitrary",)),
        name="dwconv",
    )(xw, xw, xw, conv_w, conv_b.reshape(1, w))


def _log_sigmoid(x):
    return jnp.minimum(x, 0.0) - jnp.log1p(jnp.exp(-jnp.abs(x)))


def _sigmoid(x):
    return 0.5 * jnp.tanh(0.5 * x) + 0.5


def _scan_tile(a, b, carry, reverse):
    n, w = a.shape
    groups = n // 8
    a3 = a.reshape(groups, 8, w)
    b3 = b.reshape(groups, 8, w)
    sub = lax.broadcasted_iota(jnp.int32, (groups, 8, w), 1)
    for s in (1, 2, 4):
        shift = 8 - s if reverse else s
        valid = (sub < 8 - s) if reverse else (sub >= s)
        a_sh = pltpu.roll(a3, shift, 1)
        b_sh = pltpu.roll(b3, shift, 1)
        b3 = jnp.where(valid, b3 + a3 * b_sh, b3)
        a3 = jnp.where(valid, a3 * a_sh, a3)
    outs = [None] * groups
    for r in (range(groups - 1, -1, -1) if reverse else range(groups)):
        hv = b3[r] + a3[r] * carry
        outs[r] = hv
        edge = hv[0:1, :] if reverse else hv[7:8, :]
        carry = jnp.broadcast_to(edge, (8, w))
    return jnp.concatenate(outs, axis=0), carry


def _lru_body(uf_ref, ub_ref, raw_ref, ixw_ref, rab_ref, ixb_ref, lam_ref,
              hf_ref, hb_ref, carry_ref, *, blk):
    j = pl.program_id(1)

    @pl.when(j == 0)
    def _():
        carry_ref[...] = jnp.zeros_like(carry_ref)

    for d, (u_ref, h_ref) in enumerate(((uf_ref, hf_ref), (ub_ref, hb_ref))):
        lsl = LRU_C * _log_sigmoid(lam_ref[d])
        for n in range(LRU_BLOCKS):
            cs = slice(n * blk, (n + 1) * blk)
            u = u_ref[:, cs]
            ub16 = u.astype(BF16)
            r = _sigmoid(jnp.dot(ub16, raw_ref[d, n], preferred_element_type=F32) + rab_ref[d][:, cs])
            gi = _sigmoid(jnp.dot(ub16, ixw_ref[d, n], preferred_element_type=F32) + ixb_ref[d][:, cs])
            log_a = r * lsl[:, cs]
            a = jnp.exp(log_a)
            om = 1.0 - a * a
            root = jnp.where(om > 0.0, om * lax.rsqrt(om), 0.0)
            bx = root * (gi * u)
            h, carry = _scan_tile(a, bx, carry_ref[d, :, cs], reverse=(d == 1))
            h_ref[:, cs] = h
            carry_ref[d, :, cs] = carry


def _lru_scan(u, ra_w, ix_w, ra_b, ix_b, lam, *, batch, n_tiles):
    r, w = u.shape
    blk = w // LRU_BLOCKS

    def fwd_map(b, j):
        return (b * n_tiles + j, 0)

    def bwd_map(b, j):
        return (b * n_tiles + jnp.where(j == 0, 0, n_tiles - j), 0)

    full5 = pl.BlockSpec((2, LRU_BLOCKS, blk, blk), lambda b, j: (0, 0, 0, 0))
    vec = pl.BlockSpec((2, 1, w), lambda b, j: (0, 0, 0))
    return pl.pallas_call(
        functools.partial(_lru_body, blk=blk),
        grid=(batch, n_tiles),
        in_specs=[pl.BlockSpec((TILE, w), fwd_map), pl.BlockSpec((TILE, w), bwd_map),
                  full5, full5, vec, vec, vec],
        out_specs=[pl.BlockSpec((TILE, w), fwd_map), pl.BlockSpec((TILE, w), bwd_map)],
        out_shape=[jax.ShapeDtypeStruct((r, w), F32), jax.ShapeDtypeStruct((r, w), F32)],
        scratch_shapes=[pltpu.VMEM((2, 8, w), F32)],
        compiler_params=_cparams(("arbitrary", "arbitrary")),
        name="lru_scan",
    )(u, u, ra_w, ix_w, ra_b.reshape(2, 1, w), ix_b.reshape(2, 1, w), lam.reshape(2, 1, w))


def _rope_tables(n_lat, rot_dim):
    m = rot_dim // 2
    half = m // 2
    t = np.arange(n_lat)
    pos = np.stack([t // GRID_W, t % GRID_W], axis=1).astype(np.float32)
    inv = (ROPE_THETA ** (-(np.arange(half, dtype=np.float32) * 2.0) / m)).astype(np.float32)
    lane = np.arange(LANE)
    axis = np.minimum(lane // m, 1)
    freq = inv[lane % half]
    ang = pos[:, axis] * freq[None, :]
    live = (lane < rot_dim)[None, :]
    first = ((lane % m) < half)[None, :]
    cos = np.where(live, np.cos(ang), 1.0)
    sin = np.where(live, np.sin(ang), 0.0)
    s_hi = np.where(first, -sin, 0.0)
    s_lo = np.where(first, 0.0, sin)
    ident = np.concatenate([np.ones((TILE, LANE)), np.zeros((TILE, LANE)), np.zeros((TILE, LANE))], 1)
    tab = np.concatenate([cos, s_hi, s_lo], axis=1)
    tab = np.concatenate([ident, tab], axis=0).astype(np.float32)
    return (jnp.asarray(tab[:, :LANE]), jnp.asarray(tab[:, LANE:2 * LANE]), jnp.asarray(tab[:, 2 * LANE:]))


def kernel(x, c, ctx, c_ctx, ada_w, ada_b, ln_g, ln_b, mlp_w1, mlp_w2, gqa_wq, gqa_wk, gqa_wv, gqa_wo, gqa_q_g, gqa_k_g, lru_wx, lru_wy, lru_conv_w, lru_conv_b, lru_ra_w, lru_ra_b, lru_ix_w, lru_ix_b, lru_lam, lru_wo, mla_wq_a, mla_q_a_g, mla_wq_b, mla_wkv_a, mla_kv_a_g, mla_wkv_b, mla_wo):
    batch, n_lat, d = x.shape
    depth = ada_w.shape[0]
    assert ctx.shape[1] == TILE and n_lat % KV_CHUNK == 0 and n_lat % GRID_W == 0
    t_len = TILE + n_lat
    n_tiles = t_len // TILE
    rows = batch * t_len
    n_groups = rows // TILE
    alpha = (2 * depth) ** 0.25

    xs = jnp.concatenate([ctx, x], axis=1).reshape(rows, d)

    cond_rows = -(-(batch + 1) // 8) * 8
    cond = jnp.concatenate([c, c_ctx[None, :], jnp.zeros((cond_rows - batch - 1, d), F32)], axis=0)
    mods = _ada_mods(cond, ada_w, ada_b)
    m_lat = jnp.broadcast_to(mods[:, :batch, None, :], (depth, batch, n_tiles - 1, 6 * d))
    m_ctx = jnp.broadcast_to(mods[:, batch:batch + 1, None, :], (depth, batch, 1, 6 * d))
    mods = jnp.concatenate([m_ctx, m_lat], axis=2).reshape(depth, n_groups, 6, d)

    def mod3(gate, scale, shift):
        return jnp.stack([gate, scale, shift], axis=1)

    tabs_gqa = _rope_tables(n_lat, A_HEAD_DIM)
    tabs_mla = _rope_tables(n_lat, MLA_ROPE)

    def tab_specs():
        return [pl.BlockSpec((TILE, LANE), lambda j, i, k: (i % n_tiles, 0)) for _ in range(3)]

    def vt_spec(n):
        return pl.BlockSpec((None, n, TILE), lambda j, i, k: (i // n_tiles, 0, i % n_tiles))

    def mod_spec(tm):
        return pl.BlockSpec((tm // TILE, 3, d), lambda j, i, k: (i, 0, 0))

    def resid_ln(a_list, w, x_old, mod, lng, lnb, *, tm, tk, prologue=None, name=None):
        ex = [(x_old, _row_spec(tm, d)), (mod, mod_spec(tm)),
              (lng.reshape(1, d), _full_spec((1, d))), (lnb.reshape(1, d), _full_spec((1, d)))]
        outs = [(jax.ShapeDtypeStruct((rows, d), F32), _out_spec(tm, d)),
                (jax.ShapeDtypeStruct((rows, d), BF16), _out_spec(tm, d))]
        return _matmul(a_list, w, ex, outs, tm=tm, tn=d, tk=tk, prologue=prologue, name=name,
                       epilogue=functools.partial(_ep_resid_ln_mod, ngroups=tm // TILE, alpha=alpha))

    def plain(a, w, *, tm, tn, epilogue=_ep_cast, dtype=BF16, name=None):
        n = w.shape[1]
        outs = [(jax.ShapeDtypeStruct((a.shape[0], n), dtype), _out_spec(tm, tn))]
        return _matmul([a], w, [], outs, tm=tm, tn=tn, tk=a.shape[1], epilogue=epilogue, name=name)[0]

    h = _modulate(xs, mods[0])

    for i in range(depth):
        kind = i % 3
        slot = i // 3
        m_i = mods[i]
        if kind == 0:
            wq, wk, wv, wo = (gqa_wq[slot].astype(BF16), gqa_wk[slot].astype(BF16),
                              gqa_wv[slot].astype(BF16), gqa_wo[slot].astype(BF16))

            def qk_proj(w, gain, scale, name):
                n = w.shape[1]
                ex = [(gain.reshape(1, LANE), _full_spec((1, LANE)))] + list(zip(tabs_gqa, tab_specs()))
                outs = [(jax.ShapeDtypeStruct((rows, n), BF16), _out_spec(TILE, n))]
                ep = functools.partial(_ep_headnorm_rope, nblk=n // LANE, off=A_HEAD_DIM // 4, scale=scale)
                return _matmul([h], w, ex, outs, tm=TILE, tn=n, tk=d, epilogue=ep, name=name)[0]

            q = qk_proj(wq, gqa_q_g[slot], A_HEAD_DIM ** -0.5 * LOG2E, "gqa_q")
            k = qk_proj(wk, gqa_k_g[slot], 1.0, "gqa_k")
            nv = wv.shape[1]
            vt = _matmul([h], wv, [], [(jax.ShapeDtypeStruct((batch, nv, t_len), BF16), vt_spec(nv))],
                         tm=TILE, tn=nv, tk=d, epilogue=_ep_cast_t, name="gqa_v")[0]
            o = _attention(q, k, vt, batch=batch, n_tiles=n_tiles, n_steps=A_HEADS // GQA_STEP_HEADS,
                           group=GQA_STEP_HEADS, dq=A_HEAD_DIM, dv=A_HEAD_DIM, q_per_kv=A_GROUP)
            a_list, w_out, prologue = [o], wo, None
        elif kind == 1:
            wx, wy, wo = lru_wx[slot].astype(BF16), lru_wy[slot].astype(BF16), lru_wo[slot].astype(BF16)
            xw = plain(h, wx, tm=2 * TILE, tn=wx.shape[1], dtype=F32, name="lru_x")
            y = plain(h, wy, tm=2 * TILE, tn=wy.shape[1], epilogue=_ep_gelu, dtype=F32, name="lru_y")
            u = _dwconv(xw, lru_conv_w[slot], lru_conv_b[slot], n_tiles)
            hf, hb = _lru_scan(u, lru_ra_w[slot].astype(BF16), lru_ix_w[slot].astype(BF16),
                               lru_ra_b[slot], lru_ix_b[slot], lru_lam[slot],
                               batch=batch, n_tiles=n_tiles)
            a_list, w_out, prologue = [hf, hb, y], wo, _pro_lru_gate
        else:
            nq = MLA_NOPE + MLA_ROPE
            wq_a = mla_wq_a[slot].astype(BF16)
            wq_b = mla_wq_b[slot].reshape(MLA_LORA, MLA_HEADS, nq)
            wq_b = jnp.concatenate([wq_b, jnp.zeros((MLA_LORA, MLA_HEADS, 2 * LANE - nq), F32)], axis=-1)
            wq_b = wq_b.reshape(MLA_LORA, MLA_HEADS * 2 * LANE).astype(BF16)
            wkv_a = jnp.concatenate([mla_wkv_a[slot], jnp.zeros((d, LANE - MLA_ROPE), F32)], axis=-1).astype(BF16)
            wkv_b = mla_wkv_b[slot].reshape(MLA_LORA, MLA_HEADS, MLA_NOPE + MLA_V)
            wkv_b = jnp.concatenate([wkv_b[:, :, :MLA_NOPE].reshape(MLA_LORA, -1),
                                     wkv_b[:, :, MLA_NOPE:].reshape(MLA_LORA, -1)], axis=-1).astype(BF16)
            wo = mla_wo[slot].astype(BF16)

            qa = _matmul([h], wq_a, [(mla_q_a_g[slot].reshape(1, MLA_LORA), _full_spec((1, MLA_LORA)))],
                         [(jax.ShapeDtypeStruct((rows, MLA_LORA), BF16), _out_spec(2 * TILE, MLA_LORA))],
                         tm=2 * TILE, tn=MLA_LORA, tk=d, epilogue=_ep_rmsnorm, name="mla_qa")[0]
            nqb = wq_b.shape[1]
            q = _matmul([qa], wq_b, list(zip(tabs_mla, tab_specs())),
                        [(jax.ShapeDtypeStruct((rows, nqb), BF16), _out_spec(TILE, nqb))],
                        tm=TILE, tn=nqb, tk=MLA_LORA, name="mla_q",
                        epilogue=functools.partial(_ep_mla_q, nheads=MLA_HEADS, off=MLA_ROPE // 4,
                                                   scale=float(nq) ** -0.5 * LOG2E))[0]
            nkva = wkv_a.shape[1]
            ex = [(mla_kv_a_g[slot].reshape(1, MLA_LORA), _full_spec((1, MLA_LORA)))] + list(zip(tabs_mla, tab_specs()))
            kva = _matmul([h], wkv_a, ex,
                          [(jax.ShapeDtypeStruct((rows, nkva), BF16), _out_spec(TILE, nkva))],
                          tm=TILE, tn=nkva, tk=d, name="mla_kva",
                          epilogue=functools.partial(_ep_mla_kva, lora=MLA_LORA, off=MLA_ROPE // 4))[0]
            nv = MLA_HEADS * MLA_V
            kfull, vt = _matmul(
                [kva], wkv_b,
                [(kva, pl.BlockSpec((TILE, LANE), lambda j, ii, kk: (ii, MLA_LORA // LANE)))],
                [(jax.ShapeDtypeStruct((rows, MLA_HEADS * 2 * LANE), BF16), _out_spec(TILE, MLA_HEADS * 2 * LANE)),
                 (jax.ShapeDtypeStruct((batch, nv, t_len), BF16), vt_spec(nv))],
                tm=TILE, tn=wkv_b.shape[1], tk=MLA_LORA, kdim=MLA_LORA, name="mla_kv",
                epilogue=functools.partial(_ep_mla_kv, nheads=MLA_HEADS))
            o = _attention(q, kfull, vt, batch=batch, n_tiles=n_tiles, n_steps=MLA_HEADS // MLA_STEP_HEADS,
                           group=MLA_STEP_HEADS, dq=2 * LANE, dv=MLA_V, q_per_kv=1)
            a_list, w_out, prologue = [o], wo, None

        xs, h2 = resid_ln(a_list, w_out, xs, mod3(m_i[:, 2], m_i[:, 4], m_i[:, 3]),
                          ln_g[i, 0], ln_b[i, 0], tm=TILE, tk=w_out.shape[0], prologue=prologue,
                          name="mix_out")
        w1 = mlp_w1[i].astype(BF16)
        w2 = mlp_w2[i].astype(BF16)
        if i + 1 < depth:
            nxt = mod3(m_i[:, 5], mods[i + 1][:, 1], mods[i + 1][:, 0])
        else:
            zero = jnp.zeros_like(m_i[:, 5])
            nxt = mod3(m_i[:, 5], zero, zero)
        xs, h = _mlp(h2, w1, w2, xs, nxt, ln_g[i, 1], ln_b[i, 1], tm=2 * TILE, tf=1024, alpha=alpha)

    return xs.reshape(batch, t_len, d)[:, TILE:, :]
```

```python
import functools
import math

import numpy as np
import jax
import jax.numpy as jnp
from jax import lax
from jax.experimental import pallas as pl
from jax.experimental.pallas import tpu as pltpu

F32 = jnp.float32
BF16 = jnp.bfloat16

TILE = 256
GRID_W = 64
ROPE_THETA = 10000.0
EPS = 1e-6
LANE = 128
VMEM_LIMIT_MB = 56

A_HEADS = 16
A_KV_HEADS = 4
A_GROUP = A_HEADS // A_KV_HEADS
A_HEAD_DIM = 128

LRU_BLOCKS = 8
CONV_W = 4
LRU_C = 8.0

MLA_HEADS = 16
MLA_LORA = 512
MLA_NOPE = 128
MLA_ROPE = 64
MLA_V = 128

KV_CHUNK = 512
GQA_STEP_HEADS = 8
MLA_STEP_HEADS = 8
LOG2E = math.log2(math.e)


def _cparams(sem, vmem_mb=VMEM_LIMIT_MB):
    return pltpu.CompilerParams(dimension_semantics=sem,
                                vmem_limit_bytes=vmem_mb * 1024 * 1024)


def _mm_body(*refs, n_a, n_ex, n_out, nk, prologue, epilogue):
    a_refs = refs[:n_a]
    w_ref = refs[n_a]
    ex = refs[n_a + 1:n_a + 1 + n_ex]
    outs = refs[n_a + 1 + n_ex:n_a + 1 + n_ex + n_out]
    scratch = refs[n_a + 1 + n_ex + n_out:]
    if prologue is None:
        a = a_refs[0][...]
    else:
        a = prologue(a_refs)
    part = jnp.dot(a, w_ref[...], preferred_element_type=F32)
    if nk == 1:
        epilogue(part, ex, outs)
    else:
        acc_ref = scratch[0]
        k = pl.program_id(2)

        @pl.when(k == 0)
        def _():
            acc_ref[...] = part

        @pl.when(k > 0)
        def _():
            acc_ref[...] += part

        @pl.when(k == nk - 1)
        def _():
            epilogue(acc_ref, ex, outs)


def _matmul(a_list, w, extras, outs, *, tm, tn, tk, epilogue, prologue=None, name=None, kdim=None,
            m_tiles=None, row_map=lambda i: i):
    m = a_list[0].shape[0]
    kdim = a_list[0].shape[1] if kdim is None else kdim
    n = w.shape[1]
    assert m % tm == 0 and n % tn == 0 and kdim % tk == 0
    nk = kdim // tk
    grid = (n // tn, m // tm if m_tiles is None else m_tiles, nk)
    in_specs = [pl.BlockSpec((tm, tk), lambda j, i, k: (row_map(i), k)) for _ in a_list]
    in_specs.append(pl.BlockSpec((tk, tn), lambda j, i, k: (k, j)))
    in_specs += [s for _, s in extras]
    scratch = [pltpu.VMEM((tm, tn), F32)] if nk > 1 else []
    body = functools.partial(_mm_body, n_a=len(a_list), n_ex=len(extras), n_out=len(outs),
                             nk=nk, prologue=prologue, epilogue=epilogue)
    res = pl.pallas_call(
        body,
        grid=grid,
        in_specs=in_specs,
        out_specs=[s for _, s in outs],
        out_shape=[s for s, _ in outs],
        scratch_shapes=scratch,
        compiler_params=_cparams(("arbitrary", "arbitrary", "arbitrary")),
        name=name,
    )(*a_list, w, *[arr for arr, _ in extras])
    return res


def _out_spec(tm, tn):
    return pl.BlockSpec((tm, tn), lambda j, i, k: (i, j))


def _full_spec(shape):
    zeros = (0,) * len(shape)
    return pl.BlockSpec(shape, lambda j, i, k: zeros)


def _ep_cast(acc, ex, outs):
    outs[0][...] = acc[...].astype(outs[0].dtype)


def _ep_cast_t(acc, ex, outs):
    outs[0][...] = acc[...].T.astype(outs[0].dtype)


def _ep_gelu(acc, ex, outs):
    outs[0][...] = jax.nn.gelu(acc[...], approximate=True).astype(outs[0].dtype)


def _rope(x, cos, s_hi, s_lo, off):
    return x * cos + pltpu.roll(x, LANE - off, 1) * s_hi + pltpu.roll(x, off, 1) * s_lo


def _ep_headnorm_rope(acc, ex, outs, *, nblk, off, scale):
    g_ref, cos_ref, shi_ref, slo_ref = ex
    g = g_ref[...]
    cos, shi, slo = cos_ref[...], shi_ref[...], slo_ref[...]
    for j in range(nblk):
        x = acc[:, j * LANE:(j + 1) * LANE]
        ms = jnp.mean(x * x, axis=-1, keepdims=True)
        xn = x * lax.rsqrt(ms + EPS) * g
        y = _rope(xn, cos, shi, slo, off)
        if scale != 1.0:
            y = y * scale
        outs[0][:, j * LANE:(j + 1) * LANE] = y.astype(outs[0].dtype)


def _ep_rmsnorm(acc, ex, outs):
    g_ref, = ex
    x = acc[...]
    ms = jnp.mean(x * x, axis=-1, keepdims=True)
    outs[0][...] = (x * lax.rsqrt(ms + EPS) * g_ref[...]).astype(outs[0].dtype)


def _ep_mla_kva(acc, ex, outs, *, lora, off):
    g_ref, cos_ref, shi_ref, slo_ref = ex
    x = acc[:, 0:lora]
    ms = jnp.mean(x * x, axis=-1, keepdims=True)
    outs[0][:, 0:lora] = (x * lax.rsqrt(ms + EPS) * g_ref[...]).astype(outs[0].dtype)
    pe = acc[:, lora:lora + LANE]
    outs[0][:, lora:lora + LANE] = _rope(pe, cos_ref[...], shi_ref[...], slo_ref[...], off).astype(outs[0].dtype)


def _ep_mla_q(acc, ex, outs, *, nheads, off, scale):
    cos_ref, shi_ref, slo_ref = ex
    cos, shi, slo = cos_ref[...], shi_ref[...], slo_ref[...]
    for h in range(nheads):
        c0 = h * 2 * LANE
        outs[0][:, c0:c0 + LANE] = (acc[:, c0:c0 + LANE] * scale).astype(outs[0].dtype)
        pe = acc[:, c0 + LANE:c0 + 2 * LANE]
        outs[0][:, c0 + LANE:c0 + 2 * LANE] = (_rope(pe, cos, shi, slo, off) * scale).astype(outs[0].dtype)


def _ep_mla_kv(acc, ex, outs, *, nheads):
    kpe_ref, = ex
    k_out, vt_out = outs
    kpe = kpe_ref[...]
    for h in range(nheads):
        k_out[:, 2 * h * LANE:(2 * h + 1) * LANE] = acc[:, h * LANE:(h + 1) * LANE].astype(k_out.dtype)
        k_out[:, (2 * h + 1) * LANE:(2 * h + 2) * LANE] = kpe
    vt_out[...] = acc[:, nheads * LANE:2 * nheads * LANE].T.astype(vt_out.dtype)


def _ep_resid_ln_mod(acc, ex, outs, *, ngroups, alpha):
    xold_ref, mod_ref, lng_ref, lnb_ref = ex
    x_out, h_out = outs
    lng, lnb = lng_ref[...], lnb_ref[...]
    for g in range(ngroups):
        r0 = g * TILE
        gate = mod_ref[g, 0:1, :]
        scale = mod_ref[g, 1:2, :]
        shift = mod_ref[g, 2:3, :]
        y = alpha * xold_ref[r0:r0 + TILE, :] + (1.0 + gate) * acc[r0:r0 + TILE, :]
        mu = jnp.mean(y, axis=-1, keepdims=True)
        yc = y - mu
        var = jnp.mean(yc * yc, axis=-1, keepdims=True)
        xn = yc * lax.rsqrt(var + EPS) * lng + lnb
        x_out[r0:r0 + TILE, :] = xn
        h_out[r0:r0 + TILE, :] = (xn * (1.0 + scale) + shift).astype(h_out.dtype)


def _pro_lru_gate(a_refs):
    hf_ref, hb_ref, y_ref = a_refs
    hsum = hf_ref[...].astype(F32) + hb_ref[...].astype(F32)
    return (hsum * y_ref[...].astype(F32)).astype(BF16)


def _mlp_body(h_ref, w1_ref, w2_ref, xold_ref, mod_ref, lng_ref, lnb_ref, x_out, h_out, acc_ref,
              *, nf, ngroups, alpha):
    f = pl.program_id(1)

    @pl.when(f == 0)
    def _():
        acc_ref[...] = jnp.zeros_like(acc_ref)

    u = jnp.maximum(jnp.dot(h_ref[...], w1_ref[...], preferred_element_type=F32), 0.0)
    acc_ref[...] += jnp.dot((u * u).astype(BF16), w2_ref[...], preferred_element_type=F32)

    @pl.when(f == nf - 1)
    def _():
        _ep_resid_ln_mod(acc_ref, (xold_ref, mod_ref, lng_ref, lnb_ref), (x_out, h_out),
                         ngroups=ngroups, alpha=alpha)


def _mlp(h, w1, w2, layer, x_old, mod, lng, lnb, *, tm, tf, alpha):
    rows, d = h.shape
    ff = w1.shape[2]
    nf = ff // tf
    row = pl.BlockSpec((tm, d), lambda i, f: (i, 0))
    vec = pl.BlockSpec((1, d), lambda i, f: (0, 0))
    return pl.pallas_call(
        functools.partial(_mlp_body, nf=nf, ngroups=tm // TILE, alpha=alpha),
        grid=(rows // tm, nf),
        in_specs=[row,
                  pl.BlockSpec((None, d, tf), lambda i, f: (layer, 0, f)),
                  pl.BlockSpec((None, tf, d), lambda i, f: (layer, f, 0)),
                  row,
                  pl.BlockSpec((tm // TILE, 3, d), lambda i, f: (i, 0, 0)),
                  vec, vec],
        out_specs=[row, row],
        out_shape=[jax.ShapeDtypeStruct((rows, d), F32), jax.ShapeDtypeStruct((rows, d), BF16)],
        scratch_shapes=[pltpu.VMEM((tm, d), F32)],
        compiler_params=_cparams(("arbitrary", "arbitrary")),
        name="mlp",
    )(h, w1, w2, x_old, mod, lng.reshape(1, d), lnb.reshape(1, d))


def _ada_body(c_ref, w_ref, b_ref, o_ref):
    cond = c_ref[...]
    sc = (cond * jax.nn.sigmoid(cond)).astype(BF16)
    o_ref[0] = jnp.dot(sc, w_ref[0].astype(BF16), preferred_element_type=F32) + b_ref[0]


def _ada_mods(cond, ada_w, ada_b, tn=1024):
    depth, d, n = ada_w.shape
    rows = cond.shape[0]
    return pl.pallas_call(
        _ada_body,
        grid=(depth, n // tn),
        in_specs=[pl.BlockSpec((rows, d), lambda l, j: (0, 0)),
                  pl.BlockSpec((1, d, tn), lambda l, j: (l, 0, j)),
                  pl.BlockSpec((1, 1, tn), lambda l, j: (l, 0, j))],
        out_specs=pl.BlockSpec((1, rows, tn), lambda l, j: (l, 0, j)),
        out_shape=jax.ShapeDtypeStruct((depth, rows, n), F32),
        compiler_params=_cparams(("arbitrary", "arbitrary")),
        name="ada_mods",
    )(cond, ada_w, ada_b.reshape(depth, 1, n))


def _modulate_body(x_ref, mod_ref, h_ref):
    h_ref[...] = (x_ref[...] * (1.0 + mod_ref[0, 1:2, :]) + mod_ref[0, 0:1, :]).astype(h_ref.dtype)


def _modulate(xs, mod):
    r, d = xs.shape
    return pl.pallas_call(
        _modulate_body,
        grid=(r // TILE,),
        in_specs=[pl.BlockSpec((TILE, d), lambda i: (i, 0)),
                  pl.BlockSpec((1, mod.shape[1], d), lambda i: (i, 0, 0))],
        out_specs=pl.BlockSpec((TILE, d), lambda i: (i, 0)),
        out_shape=jax.ShapeDtypeStruct((r, d), BF16),
        compiler_params=_cparams(("arbitrary",)),
        name="modulate",
    )(xs, mod)


def _dot_nt(a, b):
    return lax.dot_general(a, b, (((1,), (1,)), ((), ())), preferred_element_type=F32)


def _attn_body(q_ref, k_ref, vt_ref, o_ref, s_ref, p_ref, *, group, dq, dv, q_per_kv, n_lat_chunks):
    i = pl.program_id(2)

    def kcol(g):
        return (g // q_per_kv) * dq

    def vrow(g):
        return (g // q_per_kv) * dv

    def finish(g, acc, l8):
        l = jnp.sum(l8, axis=0, keepdims=True)
        out = acc * (1.0 / l)
        o_ref[:, g * dv:(g + 1) * dv] = out.T.astype(o_ref.dtype)

    def colmax8(x):
        return jnp.max(x.reshape(x.shape[0] // 8, 8, x.shape[1]), axis=0)

    def colsum8(x):
        return jnp.sum(x.reshape(x.shape[0] // 8, 8, x.shape[1]), axis=0)

    @pl.when(i == 0)
    def _():
        for g in range(group):
            st = _dot_nt(k_ref[0:TILE, kcol(g):kcol(g) + dq], q_ref[:, g * dq:(g + 1) * dq])
            m = jnp.max(st, axis=0, keepdims=True)
            p = jnp.exp2(st - m)
            acc = jnp.dot(vt_ref[vrow(g):vrow(g) + dv, 0:TILE], p.astype(BF16), preferred_element_type=F32)
            finish(g, acc, colsum8(p))

    @pl.when(i > 0)
    def _():
        neg = jnp.full((8, TILE), -jnp.inf, F32)
        zero8 = jnp.zeros((8, TILE), F32)
        zacc = jnp.zeros((dv, TILE), F32)
        state = {}
        for ph in range(group + 2):
            ga, gb, gc = ph, ph - 1, ph - 2
            do_a, do_b, do_c = ga < group, 0 <= gb < group, 0 <= gc < group
            q_a = q_ref[:, ga * dq:(ga + 1) * dq] if do_a else None
            m_b = jnp.max(state["m8"], axis=0, keepdims=True) if do_b else None

            def stage(start, size, carry, q_a=q_a, m_b=m_b, ga=ga, gb=gb, gc=gc,
                      do_a=do_a, do_b=do_b, do_c=do_c):
                m8, l8, acc = carry
                if do_a:
                    st = _dot_nt(k_ref[pl.ds(start, size), kcol(ga):kcol(ga) + dq], q_a)
                    s_ref[ga % 2, pl.ds(start, size), :] = st
                    m8 = jnp.maximum(m8, colmax8(st))
                if do_c:
                    acc = acc + jnp.dot(vt_ref[vrow(gc):vrow(gc) + dv, pl.ds(start, size)],
                                        p_ref[gc % 2, pl.ds(start, size), :], preferred_element_type=F32)
                if do_b:
                    p = jnp.exp2(s_ref[gb % 2, pl.ds(start, size), :] - m_b)
                    p_ref[gb % 2, pl.ds(start, size), :] = p.astype(BF16)
                    l8 = l8 + colsum8(p)
                return m8, l8, acc

            carry = stage(0, TILE, (neg, zero8, zacc))

            def body(c, carry, stage=stage):
                return stage(pl.multiple_of(TILE + c * KV_CHUNK, TILE), KV_CHUNK, carry)

            m8, l8, acc = lax.fori_loop(0, n_lat_chunks, body, carry, unroll=True)
            if do_c:
                finish(gc, acc, state["l8"])
            if do_b:
                state["l8"] = l8
            if do_a:
                state["m8"] = m8


def _attention(q, k, vt, *, batch, n_tiles, n_steps, group, dq, dv, q_per_kv):
    r = q.shape[0]
    t_len = n_tiles * TILE
    kvw = group // q_per_kv
    return pl.pallas_call(
        functools.partial(_attn_body, group=group, dq=dq, dv=dv, q_per_kv=q_per_kv,
                          n_lat_chunks=(t_len - TILE) // KV_CHUNK),
        grid=(batch, n_steps, n_tiles),
        in_specs=[pl.BlockSpec((TILE, group * dq), lambda b, h, i: (b * n_tiles + i, h)),
                  pl.BlockSpec((t_len, kvw * dq), lambda b, h, i: (b, h), pipeline_mode=pl.Buffered(1)),
                  pl.BlockSpec((None, kvw * dv, t_len), lambda b, h, i: (b, h, 0), pipeline_mode=pl.Buffered(1))],
        out_specs=pl.BlockSpec((TILE, group * dv), lambda b, h, i: (b * n_tiles + i, h)),
        out_shape=jax.ShapeDtypeStruct((r, n_steps * group * dv), BF16),
        scratch_shapes=[pltpu.VMEM((2, t_len, TILE), F32), pltpu.VMEM((2, t_len, TILE), BF16)],
        compiler_params=_cparams(("arbitrary", "arbitrary", "arbitrary")),
        name="attention",
    )(q, k, vt)


def _conv_body(x_ref, prev_ref, next_ref, w_ref, b_ref, u_ref, ext_ref, *, n_tiles):
    i = pl.program_id(0)
    seg = i % n_tiles
    has_prev = seg > 1
    has_next = jnp.logical_and(seg > 0, seg < n_tiles - 1)
    ext_ref[0:8, :] = jnp.where(has_prev, prev_ref[...], 0.0)
    ext_ref[8:8 + TILE, :] = x_ref[...]
    ext_ref[8 + TILE:16 + TILE, :] = jnp.where(has_next, next_ref[...], 0.0)
    left = CONV_W // 2
    acc = b_ref[...] + ext_ref[pl.ds(8 - left, TILE), :] * w_ref[0:1, :]
    for j in range(1, CONV_W):
        acc = acc + ext_ref[pl.ds(8 - left + j, TILE), :] * w_ref[j:j + 1, :]
    u_ref[...] = acc


def _dwconv(xw, conv_w, conv_b, n_tiles):
    r, w = xw.shape
    sub = TILE // 8
    n_sub = r // 8
    return pl.pallas_call(
        functools.partial(_conv_body, n_tiles=n_tiles),
        grid=(r // TILE,),
        in_specs=[pl.BlockSpec((TILE, w), lambda i: (i, 0)),
                  pl.BlockSpec((8, w), lambda i: (jnp.maximum(i * sub - 1, 0), 0)),
                  pl.BlockSpec((8, w), lambda i: (jnp.minimum((i + 1) * sub, n_sub - 1), 0)),
                  pl.BlockSpec((CONV_W, w), lambda i: (0, 0)),
                  pl.BlockSpec((1, w), lambda i: (0, 0))],
        out_specs=pl.BlockSpec((TILE, w), lambda i: (i, 0)),
        out_shape=jax.ShapeDtypeStruct((r, w), F32),
        scratch_shapes=[pltpu.VMEM((TILE + 16, w), F32)],
        compiler_params=_cparams(("arbitrary",)),
        name="dwconv",
    )(xw, xw, xw, conv_w, conv_b.reshape(1, w))


def _log_sigmoid(x):
    return jnp.minimum(x, 0.0) - jnp.log1p(jnp.exp(-jnp.abs(x)))


def _sigmoid(x):
    return 0.5 * jnp.tanh(0.5 * x) + 0.5


def _scan_tile(a, b, carry, reverse):
    n, w = a.shape
    groups = n // 8
    a3 = a.reshape(groups, 8, w)
    b3 = b.reshape(groups, 8, w)
    sub = lax.broadcasted_iota(jnp.int32, (groups, 8, w), 1)
    for s in (1, 2, 4):
        shift = 8 - s if reverse else s
        valid = (sub < 8 - s) if reverse else (sub >= s)
        a_sh = pltpu.roll(a3, shift, 1)
        b_sh = pltpu.roll(b3, shift, 1)
        b3 = jnp.where(valid, b3 + a3 * b_sh, b3)
        a3 = jnp.where(valid, a3 * a_sh, a3)
    outs = [None] * groups
    for r in (range(groups - 1, -1, -1) if reverse else range(groups)):
        hv = b3[r] + a3[r] * carry
        outs[r] = hv
        edge = hv[0:1, :] if reverse else hv[7:8, :]
        carry = jnp.broadcast_to(edge, (8, w))
    return jnp.concatenate(outs, axis=0), carry


def _lru_body(uf_ref, ub_ref, raw_ref, ixw_ref, rab_ref, ixb_ref, lam_ref,
              hf_ref, hb_ref, carry_ref, *, blk):
    j = pl.program_id(1)

    @pl.when(j == 0)
    def _():
        carry_ref[...] = jnp.zeros_like(carry_ref)

    for d, (u_ref, h_ref) in enumerate(((uf_ref, hf_ref), (ub_ref, hb_ref))):
        lsl = LRU_C * _log_sigmoid(lam_ref[d])
        for n in range(LRU_BLOCKS):
            cs = slice(n * blk, (n + 1) * blk)
            u = u_ref[:, cs]
            ub16 = u.astype(BF16)
            r = _sigmoid(jnp.dot(ub16, raw_ref[d, n], preferred_element_type=F32) + rab_ref[d][:, cs])
            gi = _sigmoid(jnp.dot(ub16, ixw_ref[d, n], preferred_element_type=F32) + ixb_ref[d][:, cs])
            log_a = r * lsl[:, cs]
            a = jnp.exp(log_a)
            om = 1.0 - a * a
            root = jnp.where(om > 0.0, om * lax.rsqrt(om), 0.0)
            bx = root * (gi * u)
            h, carry = _scan_tile(a, bx, carry_ref[d, :, cs], reverse=(d == 1))
            h_ref[:, cs] = h.astype(h_ref.dtype)
            carry_ref[d, :, cs] = carry


def _lru_scan(u, ra_w, ix_w, ra_b, ix_b, lam, *, batch, n_tiles):
    r, w = u.shape
    blk = w // LRU_BLOCKS

    def fwd_map(b, j):
        return (b * n_tiles + j, 0)

    def bwd_map(b, j):
        return (b * n_tiles + jnp.where(j == 0, 0, n_tiles - j), 0)

    full5 = pl.BlockSpec((2, LRU_BLOCKS, blk, blk), lambda b, j: (0, 0, 0, 0))
    vec = pl.BlockSpec((2, 1, w), lambda b, j: (0, 0, 0))
    return pl.pallas_call(
        functools.partial(_lru_body, blk=blk),
        grid=(batch, n_tiles),
        in_specs=[pl.BlockSpec((TILE, w), fwd_map), pl.BlockSpec((TILE, w), bwd_map),
                  full5, full5, vec, vec, vec],
        out_specs=[pl.BlockSpec((TILE, w), fwd_map), pl.BlockSpec((TILE, w), bwd_map)],
        out_shape=[jax.ShapeDtypeStruct((r, w), BF16), jax.ShapeDtypeStruct((r, w), BF16)],
        scratch_shapes=[pltpu.VMEM((2, 8, w), F32)],
        compiler_params=_cparams(("arbitrary", "arbitrary")),
        name="lru_scan",
    )(u, u, ra_w, ix_w, ra_b.reshape(2, 1, w), ix_b.reshape(2, 1, w), lam.reshape(2, 1, w))


def _rope_tables(n_lat, rot_dim):
    m = rot_dim // 2
    half = m // 2
    t = np.arange(n_lat)
    pos = np.stack([t // GRID_W, t % GRID_W], axis=1).astype(np.float32)
    inv = (ROPE_THETA ** (-(np.arange(half, dtype=np.float32) * 2.0) / m)).astype(np.float32)
    lane = np.arange(LANE)
    axis = np.minimum(lane // m, 1)
    freq = inv[lane % half]
    ang = pos[:, axis] * freq[None, :]
    live = (lane < rot_dim)[None, :]
    first = ((lane % m) < half)[None, :]
    cos = np.where(live, np.cos(ang), 1.0)
    sin = np.where(live, np.sin(ang), 0.0)
    s_hi = np.where(first, -sin, 0.0)
    s_lo = np.where(first, 0.0, sin)
    ident = np.concatenate([np.ones((TILE, LANE)), np.zeros((TILE, LANE)), np.zeros((TILE, LANE))], 1)
    tab = np.concatenate([cos, s_hi, s_lo], axis=1)
    tab = np.concatenate([ident, tab], axis=0).astype(np.float32)
    return (jnp.asarray(tab[:, :LANE]), jnp.asarray(tab[:, LANE:2 * LANE]), jnp.asarray(tab[:, 2 * LANE:]))


def kernel(x, c, ctx, c_ctx, ada_w, ada_b, ln_g, ln_b, mlp_w1, mlp_w2, gqa_wq, gqa_wk, gqa_wv, gqa_wo, gqa_q_g, gqa_k_g, lru_wx, lru_wy, lru_conv_w, lru_conv_b, lru_ra_w, lru_ra_b, lru_ix_w, lru_ix_b, lru_lam, lru_wo, mla_wq_a, mla_q_a_g, mla_wq_b, mla_wkv_a, mla_kv_a_g, mla_wkv_b, mla_wo):
    batch, n_lat, d = x.shape
    depth = ada_w.shape[0]
    assert ctx.shape[1] == TILE and n_lat % KV_CHUNK == 0 and n_lat % GRID_W == 0
    t_len = TILE + n_lat
    n_tiles = t_len // TILE
    rows = batch * t_len
    n_groups = rows // TILE
    alpha = (2 * depth) ** 0.25

    xs = jnp.concatenate([ctx, x], axis=1).reshape(rows, d)

    cond_rows = -(-(batch + 1) // 8) * 8
    cond = jnp.concatenate([c, c_ctx[None, :], jnp.zeros((cond_rows - batch - 1, d), F32)], axis=0)
    mods = _ada_mods(cond, ada_w, ada_b)
    m_lat = jnp.broadcast_to(mods[:, :batch, None, :], (depth, batch, n_tiles - 1, 6 * d))
    m_ctx = jnp.broadcast_to(mods[:, batch:batch + 1, None, :], (depth, batch, 1, 6 * d))
    mods = jnp.concatenate([m_ctx, m_lat], axis=2).reshape(depth, n_groups, 6, d)

    def mod3(gate, scale, shift):
        return jnp.stack([gate, scale, shift], axis=1)

    tabs_gqa = _rope_tables(n_lat, A_HEAD_DIM)
    tabs_mla = _rope_tables(n_lat, MLA_ROPE)

    def tab_specs():
        return [pl.BlockSpec((TILE, LANE), lambda j, i, k: (i % n_tiles, 0)) for _ in range(3)]

    def vt_spec(n):
        return pl.BlockSpec((None, n, TILE), lambda j, i, k: (i // n_tiles, 0, i % n_tiles))

    def lat_tile(i):
        return i + i // (n_tiles - 1) + 1

    def resid_ln(a_list, w, x_old, mod, lng, lnb, *, tk, prologue=None, name=None, latent_only=False):
        rmap = lat_tile if latent_only else (lambda i: i)
        out_rows = batch * n_lat if latent_only else rows
        ex = [(x_old, pl.BlockSpec((TILE, d), lambda j, i, k: (rmap(i), 0))),
              (mod, pl.BlockSpec((1, 3, d), lambda j, i, k: (rmap(i), 0, 0))),
              (lng.reshape(1, d), _full_spec((1, d))), (lnb.reshape(1, d), _full_spec((1, d)))]
        outs = [(jax.ShapeDtypeStruct((out_rows, d), F32), _out_spec(TILE, d)),
                (jax.ShapeDtypeStruct((out_rows, d), BF16), _out_spec(TILE, d))]
        return _matmul(a_list, w, ex, outs, tm=TILE, tn=d, tk=tk, prologue=prologue, name=name,
                       m_tiles=out_rows // TILE, row_map=rmap,
                       epilogue=functools.partial(_ep_resid_ln_mod, ngroups=1, alpha=alpha))

    def plain(a, w, *, tm, tn, epilogue=_ep_cast, dtype=BF16, name=None):
        n = w.shape[1]
        outs = [(jax.ShapeDtypeStruct((a.shape[0], n), dtype), _out_spec(tm, tn))]
        return _matmul([a], w, [], outs, tm=tm, tn=tn, tk=a.shape[1], epilogue=epilogue, name=name)[0]

    h = _modulate(xs, mods[0])
    w1_all = mlp_w1.astype(BF16)
    w2_all = mlp_w2.astype(BF16)

    for i in range(depth):
        kind = i % 3
        slot = i // 3
        m_i = mods[i]
        if kind == 0:
            wq, wk, wv, wo = (gqa_wq[slot].astype(BF16), gqa_wk[slot].astype(BF16),
                              gqa_wv[slot].astype(BF16), gqa_wo[slot].astype(BF16))

            def qk_proj(w, gain, scale, name):
                n = w.shape[1]
                ex = [(gain.reshape(1, LANE), _full_spec((1, LANE)))] + list(zip(tabs_gqa, tab_specs()))
                outs = [(jax.ShapeDtypeStruct((rows, n), BF16), _out_spec(TILE, n))]
                ep = functools.partial(_ep_headnorm_rope, nblk=n // LANE, off=A_HEAD_DIM // 4, scale=scale)
                return _matmul([h], w, ex, outs, tm=TILE, tn=n, tk=d, epilogue=ep, name=name)[0]

            q = qk_proj(wq, gqa_q_g[slot], A_HEAD_DIM ** -0.5 * LOG2E, "gqa_q")
            k = qk_proj(wk, gqa_k_g[slot], 1.0, "gqa_k")
            nv = wv.shape[1]
            vt = _matmul([h], wv, [], [(jax.ShapeDtypeStruct((batch, nv, t_len), BF16), vt_spec(nv))],
                         tm=TILE, tn=nv, tk=d, epilogue=_ep_cast_t, name="gqa_v")[0]
            o = _attention(q, k, vt, batch=batch, n_tiles=n_tiles, n_steps=A_HEADS // GQA_STEP_HEADS,
                           group=GQA_STEP_HEADS, dq=A_HEAD_DIM, dv=A_HEAD_DIM, q_per_kv=A_GROUP)
            a_list, w_out, prologue = [o], wo, None
        elif kind == 1:
            wx, wy, wo = lru_wx[slot].astype(BF16), lru_wy[slot].astype(BF16), lru_wo[slot].astype(BF16)
            xw = plain(h, wx, tm=2 * TILE, tn=wx.shape[1], dtype=F32, name="lru_x")
            y = plain(h, wy, tm=2 * TILE, tn=wy.shape[1], epilogue=_ep_gelu, name="lru_y")
            u = _dwconv(xw, lru_conv_w[slot], lru_conv_b[slot], n_tiles)
            hf, hb = _lru_scan(u, lru_ra_w[slot].astype(BF16), lru_ix_w[slot].astype(BF16),
                               lru_ra_b[slot], lru_ix_b[slot], lru_lam[slot],
                               batch=batch, n_tiles=n_tiles)
            a_list, w_out, prologue = [hf, hb, y], wo, _pro_lru_gate
        else:
            nq = MLA_NOPE + MLA_ROPE
            wq_a = mla_wq_a[slot].astype(BF16)
            wq_b = mla_wq_b[slot].reshape(MLA_LORA, MLA_HEADS, nq)
            wq_b = jnp.concatenate([wq_b, jnp.zeros((MLA_LORA, MLA_HEADS, 2 * LANE - nq), F32)], axis=-1)
            wq_b = wq_b.reshape(MLA_LORA, MLA_HEADS * 2 * LANE).astype(BF16)
            wkv_a = jnp.concatenate([mla_wkv_a[slot], jnp.zeros((d, LANE - MLA_ROPE), F32)], axis=-1).astype(BF16)
            wkv_b = mla_wkv_b[slot].reshape(MLA_LORA, MLA_HEADS, MLA_NOPE + MLA_V)
            wkv_b = jnp.concatenate([wkv_b[:, :, :MLA_NOPE].reshape(MLA_LORA, -1),
                                     wkv_b[:, :, MLA_NOPE:].reshape(MLA_LORA, -1)], axis=-1).astype(BF16)
            wo = mla_wo[slot].astype(BF16)

            qa = _matmul([h], wq_a, [(mla_q_a_g[slot].reshape(1, MLA_LORA), _full_spec((1, MLA_LORA)))],
                         [(jax.ShapeDtypeStruct((rows, MLA_LORA), BF16), _out_spec(2 * TILE, MLA_LORA))],
                         tm=2 * TILE, tn=MLA_LORA, tk=d, epilogue=_ep_rmsnorm, name="mla_qa")[0]
            nqb = wq_b.shape[1]
            q = _matmul([qa], wq_b, list(zip(tabs_mla, tab_specs())),
                        [(jax.ShapeDtypeStruct((rows, nqb), BF16), _out_spec(TILE, nqb))],
                        tm=TILE, tn=nqb, tk=MLA_LORA, name="mla_q",
                        epilogue=functools.partial(_ep_mla_q, nheads=MLA_HEADS, off=MLA_ROPE // 4,
                                                   scale=float(nq) ** -0.5 * LOG2E))[0]
            nkva = wkv_a.shape[1]
            ex = [(mla_kv_a_g[slot].reshape(1, MLA_LORA), _full_spec((1, MLA_LORA)))] + list(zip(tabs_mla, tab_specs()))
            kva = _matmul([h], wkv_a, ex,
                          [(jax.ShapeDtypeStruct((rows, nkva), BF16), _out_spec(TILE, nkva))],
                          tm=TILE, tn=nkva, tk=d, name="mla_kva",
                          epilogue=functools.partial(_ep_mla_kva, lora=MLA_LORA, off=MLA_ROPE // 4))[0]
            nv = MLA_HEADS * MLA_V
            kfull, vt = _matmul(
                [kva], wkv_b,
                [(kva, pl.BlockSpec((TILE, LANE), lambda j, ii, kk: (ii, MLA_LORA // LANE)))],
                [(jax.ShapeDtypeStruct((rows, MLA_HEADS * 2 * LANE), BF16), _out_spec(TILE, MLA_HEADS * 2 * LANE)),
                 (jax.ShapeDtypeStruct((batch, nv, t_len), BF16), vt_spec(nv))],
                tm=TILE, tn=wkv_b.shape[1], tk=MLA_LORA, kdim=MLA_LORA, name="mla_kv",
                epilogue=functools.partial(_ep_mla_kv, nheads=MLA_HEADS))
            o = _attention(q, kfull, vt, batch=batch, n_tiles=n_tiles, n_steps=MLA_HEADS // MLA_STEP_HEADS,
                           group=MLA_STEP_HEADS, dq=2 * LANE, dv=MLA_V, q_per_kv=1)
            a_list, w_out, prologue = [o], wo, None

        last = i == depth - 1
        xs, h2 = resid_ln(a_list, w_out, xs, mod3(m_i[:, 2], m_i[:, 4], m_i[:, 3]),
                          ln_g[i, 0], ln_b[i, 0], tk=w_out.shape[0], prologue=prologue,
                          name="mix_out", latent_only=last)
        if not last:
            nxt = mod3(m_i[:, 5], mods[i + 1][:, 1], mods[i + 1][:, 0])
        else:
            zero = jnp.zeros_like(m_i[:, 5])
            nxt = mod3(m_i[:, 5], zero, zero)
            nxt = nxt.reshape(batch, n_tiles, 3, d)[:, 1:].reshape(batch * (n_tiles - 1), 3, d)
        xs, h = _mlp(h2, w1_all, w2_all, i, xs, nxt, ln_g[i, 1], ln_b[i, 1], tm=2 * TILE, tf=1024,
                     alpha=alpha)

    return xs.reshape(batch, n_lat, d)
```

```python
import functools
import math

import numpy as np
import jax
import jax.numpy as jnp
from jax import lax
from jax.experimental import pallas as pl
from jax.experimental.pallas import tpu as pltpu

F32 = jnp.float32
BF16 = jnp.bfloat16

TILE = 256
GRID_W = 64
ROPE_THETA = 10000.0
EPS = 1e-6
LANE = 128
VMEM_LIMIT_MB = 56

A_HEADS = 16
A_KV_HEADS = 4
A_GROUP = A_HEADS // A_KV_HEADS
A_HEAD_DIM = 128

LRU_BLOCKS = 8
CONV_W = 4
LRU_C = 8.0

MLA_HEADS = 16
MLA_LORA = 512
MLA_NOPE = 128
MLA_ROPE = 64
MLA_V = 128

KV_CHUNK = 512
GQA_STEP_HEADS = 8
MLA_STEP_HEADS = 8
LOG2E = math.log2(math.e)


def _cparams(sem, vmem_mb=VMEM_LIMIT_MB):
    return pltpu.CompilerParams(dimension_semantics=sem,
                                vmem_limit_bytes=vmem_mb * 1024 * 1024)


def _mm_body(*refs, n_a, n_ex, n_out, nk, prologue, epilogue):
    a_refs = refs[:n_a]
    w_ref = refs[n_a]
    ex = refs[n_a + 1:n_a + 1 + n_ex]
    outs = refs[n_a + 1 + n_ex:n_a + 1 + n_ex + n_out]
    scratch = refs[n_a + 1 + n_ex + n_out:]
    if prologue is None:
        a = a_refs[0][...]
    else:
        a = prologue(a_refs)
    part = jnp.dot(a, w_ref[...], preferred_element_type=F32)
    if nk == 1:
        epilogue(part, ex, outs)
    else:
        acc_ref = scratch[0]
        k = pl.program_id(2)

        @pl.when(k == 0)
        def _():
            acc_ref[...] = part

        @pl.when(k > 0)
        def _():
            acc_ref[...] += part

        @pl.when(k == nk - 1)
        def _():
            epilogue(acc_ref, ex, outs)


def _matmul(a_list, w, extras, outs, *, tm, tn, tk, epilogue, prologue=None, name=None, kdim=None,
            m_tiles=None, row_map=lambda i: i):
    m = a_list[0].shape[0]
    kdim = a_list[0].shape[1] if kdim is None else kdim
    n = w.shape[1]
    assert m % tm == 0 and n % tn == 0 and kdim % tk == 0
    nk = kdim // tk
    grid = (n // tn, m // tm if m_tiles is None else m_tiles, nk)
    in_specs = [pl.BlockSpec((tm, tk), lambda j, i, k: (row_map(i), k)) for _ in a_list]
    in_specs.append(pl.BlockSpec((tk, tn), lambda j, i, k: (k, j)))
    in_specs += [s for _, s in extras]
    scratch = [pltpu.VMEM((tm, tn), F32)] if nk > 1 else []
    body = functools.partial(_mm_body, n_a=len(a_list), n_ex=len(extras), n_out=len(outs),
                             nk=nk, prologue=prologue, epilogue=epilogue)
    res = pl.pallas_call(
        body,
        grid=grid,
        in_specs=in_specs,
        out_specs=[s for _, s in outs],
        out_shape=[s for s, _ in outs],
        scratch_shapes=scratch,
        compiler_params=_cparams(("arbitrary", "arbitrary", "arbitrary")),
        name=name,
    )(*a_list, w, *[arr for arr, _ in extras])
    return res


def _out_spec(tm, tn):
    return pl.BlockSpec((tm, tn), lambda j, i, k: (i, j))


def _full_spec(shape):
    zeros = (0,) * len(shape)
    return pl.BlockSpec(shape, lambda j, i, k: zeros)


def _ep_cast(acc, ex, outs):
    outs[0][...] = acc[...].astype(outs[0].dtype)


def _ep_cast_t(acc, ex, outs):
    outs[0][...] = acc[...].T.astype(outs[0].dtype)


def _ep_gelu(acc, ex, outs):
    outs[0][...] = jax.nn.gelu(acc[...], approximate=True).astype(outs[0].dtype)


def _rope(x, cos, s_hi, s_lo, off):
    return x * cos + pltpu.roll(x, LANE - off, 1) * s_hi + pltpu.roll(x, off, 1) * s_lo


def _ep_headnorm_rope(acc, ex, outs, *, nblk, off, scale):
    g_ref, cos_ref, shi_ref, slo_ref = ex
    g = g_ref[...]
    cos, shi, slo = cos_ref[...], shi_ref[...], slo_ref[...]
    for j in range(nblk):
        x = acc[:, j * LANE:(j + 1) * LANE]
        ms = jnp.mean(x * x, axis=-1, keepdims=True)
        xn = x * lax.rsqrt(ms + EPS) * g
        y = _rope(xn, cos, shi, slo, off)
        if scale != 1.0:
            y = y * scale
        outs[0][:, j * LANE:(j + 1) * LANE] = y.astype(outs[0].dtype)


def _ep_rmsnorm(acc, ex, outs):
    g_ref, = ex
    x = acc[...]
    ms = jnp.mean(x * x, axis=-1, keepdims=True)
    outs[0][...] = (x * lax.rsqrt(ms + EPS) * g_ref[...]).astype(outs[0].dtype)


def _ep_mla_kva(acc, ex, outs, *, lora, off):
    g_ref, cos_ref, shi_ref, slo_ref = ex
    x = acc[:, 0:lora]
    ms = jnp.mean(x * x, axis=-1, keepdims=True)
    outs[0][:, 0:lora] = (x * lax.rsqrt(ms + EPS) * g_ref[...]).astype(outs[0].dtype)
    pe = acc[:, lora:lora + LANE]
    outs[0][:, lora:lora + LANE] = _rope(pe, cos_ref[...], shi_ref[...], slo_ref[...], off).astype(outs[0].dtype)


def _ep_mla_q(acc, ex, outs, *, nheads, off, scale):
    cos_ref, shi_ref, slo_ref = ex
    cos, shi, slo = cos_ref[...], shi_ref[...], slo_ref[...]
    for h in range(nheads):
        c0 = h * 2 * LANE
        outs[0][:, c0:c0 + LANE] = (acc[:, c0:c0 + LANE] * scale).astype(outs[0].dtype)
        pe = acc[:, c0 + LANE:c0 + 2 * LANE]
        outs[0][:, c0 + LANE:c0 + 2 * LANE] = (_rope(pe, cos, shi, slo, off) * scale).astype(outs[0].dtype)


def _ep_mla_kv(acc, ex, outs, *, nheads):
    kpe_ref, = ex
    k_out, vt_out = outs
    kpe = kpe_ref[...]
    for h in range(nheads):
        k_out[:, 2 * h * LANE:(2 * h + 1) * LANE] = acc[:, h * LANE:(h + 1) * LANE].astype(k_out.dtype)
        k_out[:, (2 * h + 1) * LANE:(2 * h + 2) * LANE] = kpe
    vt_out[...] = acc[:, nheads * LANE:2 * nheads * LANE].T.astype(vt_out.dtype)


def _resid_ln_rows(f_out, r0, xold_ref, mod_ref, lng_ref, lnb_ref, x_out, h_out, alpha):
    n = f_out.shape[0]
    g = r0 // TILE
    assert r0 % TILE + n <= TILE
    gate = mod_ref[g, 0:1, :]
    scale = mod_ref[g, 1:2, :]
    shift = mod_ref[g, 2:3, :]
    y = alpha * xold_ref[r0:r0 + n, :] + (1.0 + gate) * f_out
    mu = jnp.mean(y, axis=-1, keepdims=True)
    yc = y - mu
    var = jnp.mean(yc * yc, axis=-1, keepdims=True)
    xn = yc * lax.rsqrt(var + EPS) * lng_ref[...] + lnb_ref[...]
    x_out[r0:r0 + n, :] = xn
    h_out[r0:r0 + n, :] = (xn * (1.0 + scale) + shift).astype(h_out.dtype)


def _ep_resid_ln_mod(acc, ex, outs, *, ngroups, alpha):
    for g in range(ngroups):
        _resid_ln_rows(acc[g * TILE:(g + 1) * TILE, :], g * TILE, *ex, *outs, alpha)


def _pro_lru_gate(a_refs):
    hf_ref, hb_ref, y_ref = a_refs
    hsum = hf_ref[...].astype(F32) + hb_ref[...].astype(F32)
    return (hsum * y_ref[...].astype(F32)).astype(BF16)


def _mlp_body(h_ref, w1_ref, w2_ref, xold_ref, mod_ref, lng_ref, lnb_ref, x_out, h_out, acc_ref,
              *, nf, ngroups, alpha):
    f = pl.program_id(1)

    @pl.when(f == 0)
    def _():
        acc_ref[...] = jnp.zeros_like(acc_ref)

    def hidden():
        u = jnp.maximum(jnp.dot(h_ref[...], w1_ref[...], preferred_element_type=F32), 0.0)
        return (u * u).astype(BF16)

    @pl.when(f < nf - 1)
    def _():
        acc_ref[...] += jnp.dot(hidden(), w2_ref[...], preferred_element_type=F32)

    @pl.when(f == nf - 1)
    def _():
        u2 = hidden()
        for g in range(ngroups):
            r0 = g * TILE
            tot = acc_ref[r0:r0 + TILE, :] + jnp.dot(u2[r0:r0 + TILE, :], w2_ref[...],
                                                   preferred_element_type=F32)
            _resid_ln_rows(tot, r0, xold_ref, mod_ref, lng_ref, lnb_ref, x_out, h_out, alpha)


def _mlp(h, w1, w2, layer, x_old, mod, lng, lnb, *, tm, tf, alpha):
    rows, d = h.shape
    ff = w1.shape[2]
    nf = ff // tf
    row = pl.BlockSpec((tm, d), lambda i, f: (i, 0))
    vec = pl.BlockSpec((1, d), lambda i, f: (0, 0))
    return pl.pallas_call(
        functools.partial(_mlp_body, nf=nf, ngroups=tm // TILE, alpha=alpha),
        grid=(rows // tm, nf),
        in_specs=[row,
                  pl.BlockSpec((None, d, tf), lambda i, f: (layer, 0, f)),
                  pl.BlockSpec((None, tf, d), lambda i, f: (layer, f, 0)),
                  row,
                  pl.BlockSpec((tm // TILE, 3, d), lambda i, f: (i, 0, 0)),
                  vec, vec],
        out_specs=[row, row],
        out_shape=[jax.ShapeDtypeStruct((rows, d), F32), jax.ShapeDtypeStruct((rows, d), BF16)],
        scratch_shapes=[pltpu.VMEM((tm, d), F32)],
        compiler_params=_cparams(("arbitrary", "arbitrary")),
        name="mlp",
    )(h, w1, w2, x_old, mod, lng.reshape(1, d), lnb.reshape(1, d))


def _ada_body(c_ref, w_ref, b_ref, o_ref):
    cond = c_ref[...]
    sc = (cond * jax.nn.sigmoid(cond)).astype(BF16)
    o_ref[0] = jnp.dot(sc, w_ref[0].astype(BF16), preferred_element_type=F32) + b_ref[0]


def _ada_mods(cond, ada_w, ada_b, tn=1024):
    depth, d, n = ada_w.shape
    rows = cond.shape[0]
    return pl.pallas_call(
        _ada_body,
        grid=(depth, n // tn),
        in_specs=[pl.BlockSpec((rows, d), lambda l, j: (0, 0)),
                  pl.BlockSpec((1, d, tn), lambda l, j: (l, 0, j)),
                  pl.BlockSpec((1, 1, tn), lambda l, j: (l, 0, j))],
        out_specs=pl.BlockSpec((1, rows, tn), lambda l, j: (l, 0, j)),
        out_shape=jax.ShapeDtypeStruct((depth, rows, n), F32),
        compiler_params=_cparams(("arbitrary", "arbitrary")),
        name="ada_mods",
    )(cond, ada_w, ada_b.reshape(depth, 1, n))


def _modulate_body(ctx_ref, x_ref, mod_ref, xs_ref, h_ref):
    j = pl.program_id(1)

    def emit(src):
        xs_ref[...] = src
        h_ref[...] = (src * (1.0 + mod_ref[0, 1:2, :]) + mod_ref[0, 0:1, :]).astype(h_ref.dtype)

    @pl.when(j == 0)
    def _():
        emit(ctx_ref[...])

    @pl.when(j > 0)
    def _():
        emit(x_ref[...])


def _modulate(x, ctx, mod):
    batch, n_lat, d = x.shape
    n_tiles = n_lat // TILE + 1
    rows = batch * n_tiles * TILE
    row = pl.BlockSpec((TILE, d), lambda b, j: (b * n_tiles + j, 0))
    return pl.pallas_call(
        _modulate_body,
        grid=(batch, n_tiles),
        in_specs=[pl.BlockSpec((None, TILE, d), lambda b, j: (b, 0, 0)),
                  pl.BlockSpec((None, TILE, d), lambda b, j: (b, jnp.maximum(j - 1, 0), 0)),
                  pl.BlockSpec((1, mod.shape[1], d), lambda b, j: (b * n_tiles + j, 0, 0))],
        out_specs=[row, row],
        out_shape=[jax.ShapeDtypeStruct((rows, d), F32), jax.ShapeDtypeStruct((rows, d), BF16)],
        compiler_params=_cparams(("arbitrary", "arbitrary")),
        name="modulate",
    )(ctx, x, mod)


def _dot_nt(a, b):
    return lax.dot_general(a, b, (((1,), (1,)), ((), ())), preferred_element_type=F32)


def _attn_body(q_ref, k_ref, vt_ref, o_ref, s_ref, p_ref, *, group, dq, dv, q_per_kv, n_lat_chunks):
    i = pl.program_id(2)

    def kcol(g):
        return (g // q_per_kv) * dq

    def vrow(g):
        return (g // q_per_kv) * dv

    def finish(g, acc, l8):
        l = jnp.sum(l8, axis=0, keepdims=True)
        out = acc * (1.0 / l)
        o_ref[:, g * dv:(g + 1) * dv] = out.T.astype(o_ref.dtype)

    def colmax8(x):
        return jnp.max(x.reshape(x.shape[0] // 8, 8, x.shape[1]), axis=0)

    def colsum8(x):
        return jnp.sum(x.reshape(x.shape[0] // 8, 8, x.shape[1]), axis=0)

    @pl.when(i == 0)
    def _():
        for g in range(group):
            st = _dot_nt(k_ref[0:TILE, kcol(g):kcol(g) + dq], q_ref[:, g * dq:(g + 1) * dq])
            m = jnp.max(st, axis=0, keepdims=True)
            p = jnp.exp2(st - m)
            acc = jnp.dot(vt_ref[vrow(g):vrow(g) + dv, 0:TILE], p.astype(BF16), preferred_element_type=F32)
            finish(g, acc, colsum8(p))

    @pl.when(i > 0)
    def _():
        neg = jnp.full((8, TILE), -jnp.inf, F32)
        zero8 = jnp.zeros((8, TILE), F32)
        zacc = jnp.zeros((dv, TILE), F32)
        state = {}
        for ph in range(group + 2):
            ga, gb, gc = ph, ph - 1, ph - 2
            do_a, do_b, do_c = ga < group, 0 <= gb < group, 0 <= gc < group
            q_a = q_ref[:, ga * dq:(ga + 1) * dq] if do_a else None
            m_b = jnp.max(state["m8"], axis=0, keepdims=True) if do_b else None

            def stage(start, size, carry, q_a=q_a, m_b=m_b, ga=ga, gb=gb, gc=gc,
                      do_a=do_a, do_b=do_b, do_c=do_c):
                m8, l8, acc = carry
                if do_a:
                    st = _dot_nt(k_ref[pl.ds(start, size), kcol(ga):kcol(ga) + dq], q_a)
                    s_ref[ga % 2, pl.ds(start, size), :] = st
                    m8 = jnp.maximum(m8, colmax8(st))
                if do_c:
                    acc = acc + jnp.dot(vt_ref[vrow(gc):vrow(gc) + dv, pl.ds(start, size)],
                                        p_ref[gc % 2, pl.ds(start, size), :], preferred_element_type=F32)
                if do_b:
                    p = jnp.exp2(s_ref[gb % 2, pl.ds(start, size), :] - m_b)
                    p_ref[gb % 2, pl.ds(start, size), :] = p.astype(BF16)
                    l8 = l8 + colsum8(p)
                return m8, l8, acc

            carry = stage(0, TILE, (neg, zero8, zacc))

            def body(c, carry, stage=stage):
                return stage(pl.multiple_of(TILE + c * KV_CHUNK, TILE), KV_CHUNK, carry)

            m8, l8, acc = lax.fori_loop(0, n_lat_chunks, body, carry, unroll=True)
            if do_c:
                finish(gc, acc, state["l8"])
            if do_b:
                state["l8"] = l8
            if do_a:
                state["m8"] = m8


def _attention(q, k, vt, *, batch, n_tiles, n_steps, group, dq, dv, q_per_kv):
    r = q.shape[0]
    t_len = n_tiles * TILE
    kvw = group // q_per_kv
    return pl.pallas_call(
        functools.partial(_attn_body, group=group, dq=dq, dv=dv, q_per_kv=q_per_kv,
                          n_lat_chunks=(t_len - TILE) // KV_CHUNK),
        grid=(batch, n_steps, n_tiles),
        in_specs=[pl.BlockSpec((TILE, group * dq), lambda b, h, i: (b * n_tiles + i, h)),
                  pl.BlockSpec((t_len, kvw * dq), lambda b, h, i: (b, h), pipeline_mode=pl.Buffered(1)),
                  pl.BlockSpec((None, kvw * dv, t_len), lambda b, h, i: (b, h, 0), pipeline_mode=pl.Buffered(1))],
        out_specs=pl.BlockSpec((TILE, group * dv), lambda b, h, i: (b * n_tiles + i, h)),
        out_shape=jax.ShapeDtypeStruct((r, n_steps * group * dv), BF16),
        scratch_shapes=[pltpu.VMEM((2, t_len, TILE), F32), pltpu.VMEM((2, t_len, TILE), BF16)],
        compiler_params=_cparams(("arbitrary", "arbitrary", "arbitrary")),
        name="attention",
    )(q, k, vt)


def _conv_body(x_ref, prev_ref, next_ref, w_ref, b_ref, u_ref, ext_ref, *, n_tiles):
    i = pl.program_id(0)
    seg = i % n_tiles
    has_prev = seg > 1
    has_next = jnp.logical_and(seg > 0, seg < n_tiles - 1)
    ext_ref[0:8, :] = jnp.where(has_prev, prev_ref[...], 0.0)
    ext_ref[8:8 + TILE, :] = x_ref[...]
    ext_ref[8 + TILE:16 + TILE, :] = jnp.where(has_next, next_ref[...], 0.0)
    left = CONV_W // 2
    acc = b_ref[...] + ext_ref[pl.ds(8 - left, TILE), :] * w_ref[0:1, :]
    for j in range(1, CONV_W):
        acc = acc + ext_ref[pl.ds(8 - left + j, TILE), :] * w_ref[j:j + 1, :]
    u_ref[...] = acc


def _dwconv(xw, conv_w, conv_b, n_tiles):
    r, w = xw.shape
    sub = TILE // 8
    n_sub = r // 8
    return pl.pallas_call(
        functools.partial(_conv_body, n_tiles=n_tiles),
        grid=(r // TILE,),
        in_specs=[pl.BlockSpec((TILE, w), lambda i: (i, 0)),
                  pl.BlockSpec((8, w), lambda i: (jnp.maximum(i * sub - 1, 0), 0)),
                  pl.BlockSpec((8, w), lambda i: (jnp.minimum((i + 1) * sub, n_sub - 1), 0)),
                  pl.BlockSpec((CONV_W, w), lambda i: (0, 0)),
                  pl.BlockSpec((1, w), lambda i: (0, 0))],
        out_specs=pl.BlockSpec((TILE, w), lambda i: (i, 0)),
        out_shape=jax.ShapeDtypeStruct((r, w), F32),
        scratch_shapes=[pltpu.VMEM((TILE + 16, w), F32)],
        compiler_params=_cparams(("arbitrary",)),
        name="dwconv",
    )(xw, xw, xw, conv_w, conv_b.reshape(1, w))


def _log_sigmoid(x):
    return jnp.minimum(x, 0.0) - jnp.log1p(jnp.exp(-jnp.abs(x)))


def _scan_tile(a, b, carry, reverse):
    n, w = a.shape
    groups = n // 8
    a3 = a.reshape(groups, 8, w)
    b3 = b.reshape(groups, 8, w)
    sub = lax.broadcasted_iota(jnp.int32, (groups, 8, w), 1)
    for s in (1, 2, 4):
        shift = 8 - s if reverse else s
        valid = (sub < 8 - s) if reverse else (sub >= s)
        a_sh = pltpu.roll(a3, shift, 1)
        b_sh = pltpu.roll(b3, shift, 1)
        b3 = jnp.where(valid, b3 + a3 * b_sh, b3)
        a3 = jnp.where(valid, a3 * a_sh, a3)
    outs = [None] * groups
    for r in (range(groups - 1, -1, -1) if reverse else range(groups)):
        hv = b3[r] + a3[r] * carry
        outs[r] = hv
        edge = hv[0:1, :] if reverse else hv[7:8, :]
        carry = jnp.broadcast_to(edge, (8, w))
    return jnp.concatenate(outs, axis=0), carry


def _lru_body(uf_ref, ub_ref, raw_ref, ixw_ref, rab_ref, ixb_ref, lam_ref,
              hf_ref, hb_ref, carry_ref, *, blk):
    j = pl.program_id(1)

    @pl.when(j == 0)
    def _():
        carry_ref[...] = jnp.zeros_like(carry_ref)

    for d, (u_ref, h_ref) in enumerate(((uf_ref, hf_ref), (ub_ref, hb_ref))):
        half_l2 = (0.5 * LRU_C * LOG2E) * _log_sigmoid(lam_ref[d])
        for n in range(LRU_BLOCKS):
            cs = slice(n * blk, (n + 1) * blk)
            u = u_ref[:, cs]
            ub16 = u.astype(BF16)
            t_r = jnp.tanh(jnp.dot(ub16, raw_ref[d, n], preferred_element_type=F32) + rab_ref[d][:, cs])
            t_g = jnp.tanh(jnp.dot(ub16, ixw_ref[d, n], preferred_element_type=F32) + ixb_ref[d][:, cs])
            a = jnp.exp2((t_r + 1.0) * half_l2[:, cs])
            om = 1.0 - a * a
            root = jnp.where(om > 0.0, om * lax.rsqrt(om), 0.0)
            bx = root * ((t_g + 1.0) * (0.5 * u))
            h, carry = _scan_tile(a, bx, carry_ref[d, :, cs], reverse=(d == 1))
            h_ref[:, cs] = h.astype(h_ref.dtype)
            carry_ref[d, :, cs] = carry


def _lru_scan(u, ra_w, ix_w, ra_b, ix_b, lam, *, batch, n_tiles):
    r, w = u.shape
    blk = w // LRU_BLOCKS

    def fwd_map(b, j):
        return (b * n_tiles + j, 0)

    def bwd_map(b, j):
        return (b * n_tiles + jnp.where(j == 0, 0, n_tiles - j), 0)

    full5 = pl.BlockSpec((2, LRU_BLOCKS, blk, blk), lambda b, j: (0, 0, 0, 0))
    vec = pl.BlockSpec((2, 1, w), lambda b, j: (0, 0, 0))
    return pl.pallas_call(
        functools.partial(_lru_body, blk=blk),
        grid=(batch, n_tiles),
        in_specs=[pl.BlockSpec((TILE, w), fwd_map), pl.BlockSpec((TILE, w), bwd_map),
                  full5, full5, vec, vec, vec],
        out_specs=[pl.BlockSpec((TILE, w), fwd_map), pl.BlockSpec((TILE, w), bwd_map)],
        out_shape=[jax.ShapeDtypeStruct((r, w), BF16), jax.ShapeDtypeStruct((r, w), BF16)],
        scratch_shapes=[pltpu.VMEM((2, 8, w), F32)],
        compiler_params=_cparams(("arbitrary", "arbitrary")),
        name="lru_scan",
    )(u, u, ra_w, ix_w, ra_b.reshape(2, 1, w), ix_b.reshape(2, 1, w), lam.reshape(2, 1, w))


def _rope_tables(n_lat, rot_dim):
    m = rot_dim // 2
    half = m // 2
    t = np.arange(n_lat)
    pos = np.stack([t // GRID_W, t % GRID_W], axis=1).astype(np.float32)
    inv = (ROPE_THETA ** (-(np.arange(half, dtype=np.float32) * 2.0) / m)).astype(np.float32)
    lane = np.arange(LANE)
    axis = np.minimum(lane // m, 1)
    freq = inv[lane % half]
    ang = pos[:, axis] * freq[None, :]
    live = (lane < rot_dim)[None, :]
    first = ((lane % m) < half)[None, :]
    cos = np.where(live, np.cos(ang), 1.0)
    sin = np.where(live, np.sin(ang), 0.0)
    s_hi = np.where(first, -sin, 0.0)
    s_lo = np.where(first, 0.0, sin)
    ident = np.concatenate([np.ones((TILE, LANE)), np.zeros((TILE, LANE)), np.zeros((TILE, LANE))], 1)
    tab = np.concatenate([cos, s_hi, s_lo], axis=1)
    tab = np.concatenate([ident, tab], axis=0).astype(np.float32)
    return (jnp.asarray(tab[:, :LANE]), jnp.asarray(tab[:, LANE:2 * LANE]), jnp.asarray(tab[:, 2 * LANE:]))


def kernel(x, c, ctx, c_ctx, ada_w, ada_b, ln_g, ln_b, mlp_w1, mlp_w2, gqa_wq, gqa_wk, gqa_wv, gqa_wo, gqa_q_g, gqa_k_g, lru_wx, lru_wy, lru_conv_w, lru_conv_b, lru_ra_w, lru_ra_b, lru_ix_w, lru_ix_b, lru_lam, lru_wo, mla_wq_a, mla_q_a_g, mla_wq_b, mla_wkv_a, mla_kv_a_g, mla_wkv_b, mla_wo):
    batch, n_lat, d = x.shape
    depth = ada_w.shape[0]
    assert ctx.shape[1] == TILE and n_lat % KV_CHUNK == 0 and n_lat % GRID_W == 0
    t_len = TILE + n_lat
    n_tiles = t_len // TILE
    rows = batch * t_len
    n_groups = rows // TILE
    alpha = (2 * depth) ** 0.25


    cond_rows = -(-(batch + 1) // 8) * 8
    cond = jnp.concatenate([c, c_ctx[None, :], jnp.zeros((cond_rows - batch - 1, d), F32)], axis=0)
    mods = _ada_mods(cond, ada_w, ada_b)
    m_lat = jnp.broadcast_to(mods[:, :batch, None, :], (depth, batch, n_tiles - 1, 6 * d))
    m_ctx = jnp.broadcast_to(mods[:, batch:batch + 1, None, :], (depth, batch, 1, 6 * d))
    mods = jnp.concatenate([m_ctx, m_lat], axis=2).reshape(depth, n_groups, 6, d)

    def mod3(gate, scale, shift):
        return jnp.stack([gate, scale, shift], axis=1)

    tabs_gqa = _rope_tables(n_lat, A_HEAD_DIM)
    tabs_mla = _rope_tables(n_lat, MLA_ROPE)

    def tab_specs():
        return [pl.BlockSpec((TILE, LANE), lambda j, i, k: (i % n_tiles, 0)) for _ in range(3)]

    def vt_spec(n):
        return pl.BlockSpec((None, n, TILE), lambda j, i, k: (i // n_tiles, 0, i % n_tiles))

    def lat_tile(i):
        return i + i // (n_tiles - 1) + 1

    def resid_ln(a_list, w, x_old, mod, lng, lnb, *, tk, prologue=None, name=None, latent_only=False):
        rmap = lat_tile if latent_only else (lambda i: i)
        out_rows = batch * n_lat if latent_only else rows
        ex = [(x_old, pl.BlockSpec((TILE, d), lambda j, i, k: (rmap(i), 0))),
              (mod, pl.BlockSpec((1, 3, d), lambda j, i, k: (rmap(i), 0, 0))),
              (lng.reshape(1, d), _full_spec((1, d))), (lnb.reshape(1, d), _full_spec((1, d)))]
        outs = [(jax.ShapeDtypeStruct((out_rows, d), F32), _out_spec(TILE, d)),
                (jax.ShapeDtypeStruct((out_rows, d), BF16), _out_spec(TILE, d))]
        return _matmul(a_list, w, ex, outs, tm=TILE, tn=d, tk=tk, prologue=prologue, name=name,
                       m_tiles=out_rows // TILE, row_map=rmap,
                       epilogue=functools.partial(_ep_resid_ln_mod, ngroups=1, alpha=alpha))

    def plain(a, w, *, tm, tn, epilogue=_ep_cast, dtype=BF16, name=None):
        n = w.shape[1]
        outs = [(jax.ShapeDtypeStruct((a.shape[0], n), dtype), _out_spec(tm, tn))]
        return _matmul([a], w, [], outs, tm=tm, tn=tn, tk=a.shape[1], epilogue=epilogue, name=name)[0]

    xs, h = _modulate(x, ctx, mods[0])
    w1_all = mlp_w1.astype(BF16)
    w2_all = mlp_w2.astype(BF16)

    for i in range(depth):
        kind = i % 3
        slot = i // 3
        m_i = mods[i]
        if kind == 0:
            wq, wk, wv, wo = (gqa_wq[slot].astype(BF16), gqa_wk[slot].astype(BF16),
                              gqa_wv[slot].astype(BF16), gqa_wo[slot].astype(BF16))

            def qk_proj(w, gain, scale, name):
                n = w.shape[1]
                ex = [(gain.reshape(1, LANE), _full_spec((1, LANE)))] + list(zip(tabs_gqa, tab_specs()))
                outs = [(jax.ShapeDtypeStruct((rows, n), BF16), _out_spec(TILE, n))]
                ep = functools.partial(_ep_headnorm_rope, nblk=n // LANE, off=A_HEAD_DIM // 4, scale=scale)
                return _matmul([h], w, ex, outs, tm=TILE, tn=n, tk=d, epilogue=ep, name=name)[0]

            q = qk_proj(wq, gqa_q_g[slot], A_HEAD_DIM ** -0.5 * LOG2E, "gqa_q")
            k = qk_proj(wk, gqa_k_g[slot], 1.0, "gqa_k")
            nv = wv.shape[1]
            vt = _matmul([h], wv, [], [(jax.ShapeDtypeStruct((batch, nv, t_len), BF16), vt_spec(nv))],
                         tm=TILE, tn=nv, tk=d, epilogue=_ep_cast_t, name="gqa_v")[0]
            o = _attention(q, k, vt, batch=batch, n_tiles=n_tiles, n_steps=A_HEADS // GQA_STEP_HEADS,
                           group=GQA_STEP_HEADS, dq=A_HEAD_DIM, dv=A_HEAD_DIM, q_per_kv=A_GROUP)
            a_list, w_out, prologue = [o], wo, None
        elif kind == 1:
            wx, wy, wo = lru_wx[slot].astype(BF16), lru_wy[slot].astype(BF16), lru_wo[slot].astype(BF16)
            xw = plain(h, wx, tm=2 * TILE, tn=wx.shape[1], dtype=F32, name="lru_x")
            y = plain(h, wy, tm=2 * TILE, tn=wy.shape[1], epilogue=_ep_gelu, name="lru_y")
            u = _dwconv(xw, lru_conv_w[slot], lru_conv_b[slot], n_tiles)
            hf, hb = _lru_scan(u, (0.5 * lru_ra_w[slot]).astype(BF16), (0.5 * lru_ix_w[slot]).astype(BF16),
                               0.5 * lru_ra_b[slot], 0.5 * lru_ix_b[slot], lru_lam[slot],
                               batch=batch, n_tiles=n_tiles)
            a_list, w_out, prologue = [hf, hb, y], wo, _pro_lru_gate
        else:
            nq = MLA_NOPE + MLA_ROPE
            wq_a = mla_wq_a[slot].astype(BF16)
            wq_b = mla_wq_b[slot].reshape(MLA_LORA, MLA_HEADS, nq)
            wq_b = jnp.concatenate([wq_b, jnp.zeros((MLA_LORA, MLA_HEADS, 2 * LANE - nq), F32)], axis=-1)
            wq_b = wq_b.reshape(MLA_LORA, MLA_HEADS * 2 * LANE).astype(BF16)
            wkv_a = jnp.concatenate([mla_wkv_a[slot], jnp.zeros((d, LANE - MLA_ROPE), F32)], axis=-1).astype(BF16)
            wkv_b = mla_wkv_b[slot].reshape(MLA_LORA, MLA_HEADS, MLA_NOPE + MLA_V)
            wkv_b = jnp.concatenate([wkv_b[:, :, :MLA_NOPE].reshape(MLA_LORA, -1),
                                     wkv_b[:, :, MLA_NOPE:].reshape(MLA_LORA, -1)], axis=-1).astype(BF16)
            wo = mla_wo[slot].astype(BF16)

            qa = _matmul([h], wq_a, [(mla_q_a_g[slot].reshape(1, MLA_LORA), _full_spec((1, MLA_LORA)))],
                         [(jax.ShapeDtypeStruct((rows, MLA_LORA), BF16), _out_spec(2 * TILE, MLA_LORA))],
                         tm=2 * TILE, tn=MLA_LORA, tk=d, epilogue=_ep_rmsnorm, name="mla_qa")[0]
            nqb = wq_b.shape[1]
            q = _matmul([qa], wq_b, list(zip(tabs_mla, tab_specs())),
                        [(jax.ShapeDtypeStruct((rows, nqb), BF16), _out_spec(TILE, nqb))],
                        tm=TILE, tn=nqb, tk=MLA_LORA, name="mla_q",
                        epilogue=functools.partial(_ep_mla_q, nheads=MLA_HEADS, off=MLA_ROPE // 4,
                                                   scale=float(nq) ** -0.5 * LOG2E))[0]
            nkva = wkv_a.shape[1]
            ex = [(mla_kv_a_g[slot].reshape(1, MLA_LORA), _full_spec((1, MLA_LORA)))] + list(zip(tabs_mla, tab_specs()))
            kva = _matmul([h], wkv_a, ex,
                          [(jax.ShapeDtypeStruct((rows, nkva), BF16), _out_spec(TILE, nkva))],
                          tm=TILE, tn=nkva, tk=d, name="mla_kva",
                          epilogue=functools.partial(_ep_mla_kva, lora=MLA_LORA, off=MLA_ROPE // 4))[0]
            nv = MLA_HEADS * MLA_V
            kfull, vt = _matmul(
                [kva], wkv_b,
                [(kva, pl.BlockSpec((TILE, LANE), lambda j, ii, kk: (ii, MLA_LORA // LANE)))],
                [(jax.ShapeDtypeStruct((rows, MLA_HEADS * 2 * LANE), BF16), _out_spec(TILE, MLA_HEADS * 2 * LANE)),
                 (jax.ShapeDtypeStruct((batch, nv, t_len), BF16), vt_spec(nv))],
                tm=TILE, tn=wkv_b.shape[1], tk=MLA_LORA, kdim=MLA_LORA, name="mla_kv",
                epilogue=functools.partial(_ep_mla_kv, nheads=MLA_HEADS))
            o = _attention(q, kfull, vt, batch=batch, n_tiles=n_tiles, n_steps=MLA_HEADS // MLA_STEP_HEADS,
                           group=MLA_STEP_HEADS, dq=2 * LANE, dv=MLA_V, q_per_kv=1)
            a_list, w_out, prologue = [o], wo, None

        last = i == depth - 1
        xs, h2 = resid_ln(a_list, w_out, xs, mod3(m_i[:, 2], m_i[:, 4], m_i[:, 3]),
                          ln_g[i, 0], ln_b[i, 0], tk=w_out.shape[0], prologue=prologue,
                          name="mix_out", latent_only=last)
        if not last:
            nxt = mod3(m_i[:, 5], mods[i + 1][:, 1], mods[i + 1][:, 0])
        else:
            zero = jnp.zeros_like(m_i[:, 5])
            nxt = mod3(m_i[:, 5], zero, zero)
            nxt = nxt.reshape(batch, n_tiles, 3, d)[:, 1:].reshape(batch * (n_tiles - 1), 3, d)
        xs, h = _mlp(h2, w1_all, w2_all, i, xs, nxt, ln_g[i, 1], ln_b[i, 1], tm=2 * TILE, tf=1024,
                     alpha=alpha)

    return xs.reshape(batch, n_lat, d)
```

```python
import functools
import math

import numpy as np
import jax
import jax.numpy as jnp
from jax import lax
from jax.experimental import pallas as pl
from jax.experimental.pallas import tpu as pltpu

F32 = jnp.float32
BF16 = jnp.bfloat16

TILE = 256
GRID_W = 64
ROPE_THETA = 10000.0
EPS = 1e-6
LANE = 128
VMEM_LIMIT_MB = 56

A_HEADS = 16
A_KV_HEADS = 4
A_GROUP = A_HEADS // A_KV_HEADS
A_HEAD_DIM = 128

LRU_BLOCKS = 8
CONV_W = 4
LRU_C = 8.0

MLA_HEADS = 16
MLA_LORA = 512
MLA_NOPE = 128
MLA_ROPE = 64
MLA_V = 128

KV_CHUNK = 512
GQA_STEP_HEADS = 8
MLA_STEP_HEADS = 8
LOG2E = math.log2(math.e)


def _cparams(sem, vmem_mb=VMEM_LIMIT_MB):
    return pltpu.CompilerParams(dimension_semantics=sem,
                                vmem_limit_bytes=vmem_mb * 1024 * 1024)


def _mm_body(*refs, n_a, n_ex, n_out, nk, prologue, epilogue):
    a_refs = refs[:n_a]
    w_ref = refs[n_a]
    ex = refs[n_a + 1:n_a + 1 + n_ex]
    outs = refs[n_a + 1 + n_ex:n_a + 1 + n_ex + n_out]
    scratch = refs[n_a + 1 + n_ex + n_out:]
    if prologue is None:
        a = a_refs[0][...]
    else:
        a = prologue(a_refs)
    part = jnp.dot(a, w_ref[...], preferred_element_type=F32)
    if nk == 1:
        epilogue(part, ex, outs)
    else:
        acc_ref = scratch[0]
        k = pl.program_id(2)

        @pl.when(k == 0)
        def _():
            acc_ref[...] = part

        @pl.when(k > 0)
        def _():
            acc_ref[...] += part

        @pl.when(k == nk - 1)
        def _():
            epilogue(acc_ref, ex, outs)


def _matmul(a_list, w, extras, outs, *, tm, tn, tk, epilogue, prologue=None, name=None,
            m_tiles=None, row_map=lambda i: i):
    m = a_list[0].shape[0]
    kdim = a_list[0].shape[1]
    n = w.shape[1]
    assert m % tm == 0 and n % tn == 0 and kdim % tk == 0
    nk = kdim // tk
    grid = (n // tn, m // tm if m_tiles is None else m_tiles, nk)
    in_specs = [pl.BlockSpec((tm, tk), lambda j, i, k: (row_map(i), k)) for _ in a_list]
    in_specs.append(pl.BlockSpec((tk, tn), lambda j, i, k: (k, j)))
    in_specs += [s for _, s in extras]
    scratch = [pltpu.VMEM((tm, tn), F32)] if nk > 1 else []
    body = functools.partial(_mm_body, n_a=len(a_list), n_ex=len(extras), n_out=len(outs),
                             nk=nk, prologue=prologue, epilogue=epilogue)
    res = pl.pallas_call(
        body,
        grid=grid,
        in_specs=in_specs,
        out_specs=[s for _, s in outs],
        out_shape=[s for s, _ in outs],
        scratch_shapes=scratch,
        compiler_params=_cparams(("arbitrary", "arbitrary", "arbitrary")),
        name=name,
    )(*a_list, w, *[arr for arr, _ in extras])
    return res


def _out_spec(tm, tn):
    return pl.BlockSpec((tm, tn), lambda j, i, k: (i, j))


def _full_spec(shape):
    zeros = (0,) * len(shape)
    return pl.BlockSpec(shape, lambda j, i, k: zeros)


def _ep_cast(acc, ex, outs):
    outs[0][...] = acc[...].astype(outs[0].dtype)


def _ep_gelu(acc, ex, outs):
    outs[0][...] = jax.nn.gelu(acc[...], approximate=True).astype(outs[0].dtype)


def _rope(x, cos, s_hi, s_lo, off):
    return x * cos + pltpu.roll(x, LANE - off, 1) * s_hi + pltpu.roll(x, off, 1) * s_lo


def _ep_headnorm_rope(acc, ex, outs, *, nblk, off, scale):
    g_ref, cos_ref, shi_ref, slo_ref = ex
    g = g_ref[...]
    cos, shi, slo = cos_ref[...], shi_ref[...], slo_ref[...]
    for j in range(nblk):
        x = acc[:, j * LANE:(j + 1) * LANE]
        ms = jnp.mean(x * x, axis=-1, keepdims=True)
        xn = x * lax.rsqrt(ms + EPS) * g
        y = _rope(xn, cos, shi, slo, off)
        if scale != 1.0:
            y = y * scale
        outs[0][:, j * LANE:(j + 1) * LANE] = y.astype(outs[0].dtype)


def _ep_gqa_qkv(acc, ex, outs, *, nq, nkv, off, qscale):
    qg_ref, kg_ref, cos_ref, shi_ref, slo_ref = ex
    q_out, k_out, vt_out = outs
    tabs = (cos_ref, shi_ref, slo_ref)
    _ep_headnorm_rope(acc[:, :nq * LANE], (qg_ref,) + tabs, (q_out,), nblk=nq, off=off, scale=qscale)
    _ep_headnorm_rope(acc[:, nq * LANE:(nq + nkv) * LANE], (kg_ref,) + tabs, (k_out,), nblk=nkv, off=off,
                      scale=1.0)
    vt_out[...] = acc[:, (nq + nkv) * LANE:].T.astype(vt_out.dtype)


def _rms(x, g):
    ms = jnp.mean(x * x, axis=-1, keepdims=True)
    return x * lax.rsqrt(ms + EPS) * g


def _ep_mla_q(acc, ex, outs, *, nheads, off, scale):
    cos_ref, shi_ref, slo_ref = ex
    cos, shi, slo = cos_ref[...], shi_ref[...], slo_ref[...]
    for h in range(nheads):
        c0 = h * 2 * LANE
        outs[0][:, c0:c0 + LANE] = (acc[:, c0:c0 + LANE] * scale).astype(outs[0].dtype)
        pe = acc[:, c0 + LANE:c0 + 2 * LANE]
        outs[0][:, c0 + LANE:c0 + 2 * LANE] = (_rope(pe, cos, shi, slo, off) * scale).astype(outs[0].dtype)


def _ep_mla_kv(acc, ex, outs, *, nheads):
    kpe_ref, = ex
    k_out, vt_out = outs
    kpe = kpe_ref[...]
    for h in range(nheads):
        k_out[:, 2 * h * LANE:(2 * h + 1) * LANE] = acc[:, h * LANE:(h + 1) * LANE].astype(k_out.dtype)
        k_out[:, (2 * h + 1) * LANE:(2 * h + 2) * LANE] = kpe
    vt_out[...] = acc[:, nheads * LANE:2 * nheads * LANE].T.astype(vt_out.dtype)


def _resid_ln_rows(f_out, r0, xold_ref, mod_ref, lng_ref, lnb_ref, x_out, h_out, alpha):
    n = f_out.shape[0]
    g = r0 // TILE
    assert r0 % TILE + n <= TILE
    gate = mod_ref[g, 0:1, :]
    scale = mod_ref[g, 1:2, :]
    shift = mod_ref[g, 2:3, :]
    y = alpha * xold_ref[r0:r0 + n, :] + (1.0 + gate) * f_out
    mu = jnp.mean(y, axis=-1, keepdims=True)
    yc = y - mu
    var = jnp.mean(yc * yc, axis=-1, keepdims=True)
    xn = yc * lax.rsqrt(var + EPS) * lng_ref[...] + lnb_ref[...]
    x_out[r0:r0 + n, :] = xn
    h_out[r0:r0 + n, :] = (xn * (1.0 + scale) + shift).astype(h_out.dtype)


def _ep_resid_ln_mod(acc, ex, outs, *, ngroups, alpha):
    for g in range(ngroups):
        _resid_ln_rows(acc[g * TILE:(g + 1) * TILE, :], g * TILE, *ex, *outs, alpha)


def _pro_lru_gate(a_refs):
    hf_ref, hb_ref, y_ref = a_refs
    hsum = hf_ref[...].astype(F32) + hb_ref[...].astype(F32)
    return (hsum * y_ref[...].astype(F32)).astype(BF16)


def _mlp_body(h_ref, w1_ref, w2_ref, xold_ref, mod_ref, lng_ref, lnb_ref, x_out, h_out, acc_ref,
              *, nf, ngroups, alpha):
    f = pl.program_id(1)

    @pl.when(f == 0)
    def _():
        acc_ref[...] = jnp.zeros_like(acc_ref)

    def hidden():
        u = jnp.maximum(jnp.dot(h_ref[...], w1_ref[...], preferred_element_type=F32), 0.0)
        return (u * u).astype(BF16)

    @pl.when(f < nf - 1)
    def _():
        acc_ref[...] += jnp.dot(hidden(), w2_ref[...], preferred_element_type=F32)

    @pl.when(f == nf - 1)
    def _():
        u2 = hidden()
        for g in range(ngroups):
            r0 = g * TILE
            tot = acc_ref[r0:r0 + TILE, :] + jnp.dot(u2[r0:r0 + TILE, :], w2_ref[...],
                                                   preferred_element_type=F32)
            _resid_ln_rows(tot, r0, xold_ref, mod_ref, lng_ref, lnb_ref, x_out, h_out, alpha)


def _mlp(h, w1, w2, layer, x_old, mod, lng, lnb, *, tm, tf, alpha):
    rows, d = h.shape
    ff = w1.shape[2]
    nf = ff // tf
    row = pl.BlockSpec((tm, d), lambda i, f: (i, 0))
    vec = pl.BlockSpec((1, d), lambda i, f: (0, 0))
    return pl.pallas_call(
        functools.partial(_mlp_body, nf=nf, ngroups=tm // TILE, alpha=alpha),
        grid=(rows // tm, nf),
        in_specs=[row,
                  pl.BlockSpec((None, d, tf), lambda i, f: (layer, 0, f)),
                  pl.BlockSpec((None, tf, d), lambda i, f: (layer, f, 0)),
                  row,
                  pl.BlockSpec((tm // TILE, 3, d), lambda i, f: (i, 0, 0)),
                  vec, vec],
        out_specs=[row, row],
        out_shape=[jax.ShapeDtypeStruct((rows, d), F32), jax.ShapeDtypeStruct((rows, d), BF16)],
        scratch_shapes=[pltpu.VMEM((tm, d), F32)],
        compiler_params=_cparams(("arbitrary", "arbitrary")),
        name="mlp",
    )(h, w1, w2, x_old, mod, lng.reshape(1, d), lnb.reshape(1, d))


def _mla_proj_body(h_ref, wa_ref, wqb_ref, wkvb_ref, qg_ref, kvg_ref, cos_ref, shi_ref, slo_ref,
                   q_out, k_out, vt_out, *, lora, nheads, off, qscale):
    tabs = (cos_ref, shi_ref, slo_ref)
    t = jnp.dot(h_ref[...], wa_ref[...], preferred_element_type=F32)
    qa = _rms(t[:, 0:lora], qg_ref[...]).astype(BF16)
    ckv = _rms(t[:, lora:2 * lora], kvg_ref[...]).astype(BF16)
    kpe = _rope(t[:, 2 * lora:2 * lora + LANE], cos_ref[...], shi_ref[...], slo_ref[...], off).astype(BF16)
    _ep_mla_q(jnp.dot(qa, wqb_ref[...], preferred_element_type=F32), tabs, (q_out,),
              nheads=nheads, off=off, scale=qscale)
    _ep_mla_kv(jnp.dot(ckv, wkvb_ref[...], preferred_element_type=F32), (kpe,), (k_out, vt_out),
               nheads=nheads)


def _mla_proj(h, w_a, wq_b, wkv_b, q_g, kv_g, tabs, *, batch, n_tiles, qscale):
    rows, d = h.shape
    t_len = n_tiles * TILE
    nq, nkv = wq_b.shape[1], wkv_b.shape[1]
    nv = MLA_HEADS * MLA_V

    def const(shape):
        return pl.BlockSpec(shape, lambda i: (0,) * len(shape), pipeline_mode=pl.Buffered(1))

    tab = pl.BlockSpec((TILE, LANE), lambda i: (i % n_tiles, 0))
    return pl.pallas_call(
        functools.partial(_mla_proj_body, lora=MLA_LORA, nheads=MLA_HEADS, off=MLA_ROPE // 4, qscale=qscale),
        grid=(rows // TILE,),
        in_specs=[pl.BlockSpec((TILE, d), lambda i: (i, 0)),
                  const(w_a.shape), const(wq_b.shape), const(wkv_b.shape),
                  const((1, MLA_LORA)), const((1, MLA_LORA)), tab, tab, tab],
        out_specs=[pl.BlockSpec((TILE, nq), lambda i: (i, 0)),
                   pl.BlockSpec((TILE, nq), lambda i: (i, 0)),
                   pl.BlockSpec((None, nv, TILE), lambda i: (i // n_tiles, 0, i % n_tiles))],
        out_shape=[jax.ShapeDtypeStruct((rows, nq), BF16), jax.ShapeDtypeStruct((rows, nq), BF16),
                   jax.ShapeDtypeStruct((batch, nv, t_len), BF16)],
        compiler_params=_cparams(("arbitrary",)),
        name="mla_proj",
    )(h, w_a, wq_b, wkv_b, q_g.reshape(1, MLA_LORA), kv_g.reshape(1, MLA_LORA), *tabs)


def _ada_body(c_ref, w_ref, b_ref, o_ref):
    cond = c_ref[...]
    sc = (cond * jax.nn.sigmoid(cond)).astype(BF16)
    o_ref[0] = jnp.dot(sc, w_ref[0].astype(BF16), preferred_element_type=F32) + b_ref[0]


def _ada_mods(cond, ada_w, ada_b, tn=1024):
    depth, d, n = ada_w.shape
    rows = cond.shape[0]
    return pl.pallas_call(
        _ada_body,
        grid=(depth, n // tn),
        in_specs=[pl.BlockSpec((rows, d), lambda l, j: (0, 0)),
                  pl.BlockSpec((1, d, tn), lambda l, j: (l, 0, j)),
                  pl.BlockSpec((1, 1, tn), lambda l, j: (l, 0, j))],
        out_specs=pl.BlockSpec((1, rows, tn), lambda l, j: (l, 0, j)),
        out_shape=jax.ShapeDtypeStruct((depth, rows, n), F32),
        compiler_params=_cparams(("arbitrary", "arbitrary")),
        name="ada_mods",
    )(cond, ada_w, ada_b.reshape(depth, 1, n))


def _modulate_body(ctx_ref, x_ref, mod_ref, xs_ref, h_ref):
    j = pl.program_id(1)

    def emit(src):
        xs_ref[...] = src
        h_ref[...] = (src * (1.0 + mod_ref[0, 1:2, :]) + mod_ref[0, 0:1, :]).astype(h_ref.dtype)

    @pl.when(j == 0)
    def _():
        emit(ctx_ref[...])

    @pl.when(j > 0)
    def _():
        emit(x_ref[...])


def _modulate(x, ctx, mod):
    batch, n_lat, d = x.shape
    n_tiles = n_lat // TILE + 1
    rows = batch * n_tiles * TILE
    row = pl.BlockSpec((TILE, d), lambda b, j: (b * n_tiles + j, 0))
    return pl.pallas_call(
        _modulate_body,
        grid=(batch, n_tiles),
        in_specs=[pl.BlockSpec((None, TILE, d), lambda b, j: (b, 0, 0)),
                  pl.BlockSpec((None, TILE, d), lambda b, j: (b, jnp.maximum(j - 1, 0), 0)),
                  pl.BlockSpec((1, mod.shape[1], d), lambda b, j: (b * n_tiles + j, 0, 0))],
        out_specs=[row, row],
        out_shape=[jax.ShapeDtypeStruct((rows, d), F32), jax.ShapeDtypeStruct((rows, d), BF16)],
        compiler_params=_cparams(("arbitrary", "arbitrary")),
        name="modulate",
    )(ctx, x, mod)


def _dot_nt(a, b):
    return lax.dot_general(a, b, (((1,), (1,)), ((), ())), preferred_element_type=F32)


def _attn_body(q_ref, k_ref, vt_ref, o_ref, s_ref, p_ref, *, group, dq, dv, q_per_kv, n_lat_chunks):
    i = pl.program_id(2)

    def kcol(g):
        return (g // q_per_kv) * dq

    def vrow(g):
        return (g // q_per_kv) * dv

    def finish(g, acc, l8):
        l = jnp.sum(l8, axis=0, keepdims=True)
        out = acc * (1.0 / l)
        o_ref[:, g * dv:(g + 1) * dv] = out.T.astype(o_ref.dtype)

    def colmax8(x):
        return jnp.max(x.reshape(x.shape[0] // 8, 8, x.shape[1]), axis=0)

    def colsum8(x):
        return jnp.sum(x.reshape(x.shape[0] // 8, 8, x.shape[1]), axis=0)

    @pl.when(i == 0)
    def _():
        for g in range(group):
            st = _dot_nt(k_ref[0:TILE, kcol(g):kcol(g) + dq], q_ref[:, g * dq:(g + 1) * dq])
            m = jnp.max(st, axis=0, keepdims=True)
            p = jnp.exp2(st - m)
            acc = jnp.dot(vt_ref[vrow(g):vrow(g) + dv, 0:TILE], p.astype(BF16), preferred_element_type=F32)
            finish(g, acc, colsum8(p))

    @pl.when(i > 0)
    def _():
        neg = jnp.full((8, TILE), -jnp.inf, F32)
        zero8 = jnp.zeros((8, TILE), F32)
        zacc = jnp.zeros((dv, TILE), F32)
        state = {}
        for ph in range(group + 2):
            ga, gb, gc = ph, ph - 1, ph - 2
            do_a, do_b, do_c = ga < group, 0 <= gb < group, 0 <= gc < group
            q_a = q_ref[:, ga * dq:(ga + 1) * dq] if do_a else None
            m_b = jnp.max(state["m8"], axis=0, keepdims=True) if do_b else None

            def stage(start, size, carry, q_a=q_a, m_b=m_b, ga=ga, gb=gb, gc=gc,
                      do_a=do_a, do_b=do_b, do_c=do_c):
                m8, l8, acc = carry
                if do_a:
                    st = _dot_nt(k_ref[pl.ds(start, size), kcol(ga):kcol(ga) + dq], q_a)
                    s_ref[ga % 2, pl.ds(start, size), :] = st
                    m8 = jnp.maximum(m8, colmax8(st))
                if do_c:
                    acc = acc + jnp.dot(vt_ref[vrow(gc):vrow(gc) + dv, pl.ds(start, size)],
                                        p_ref[gc % 2, pl.ds(start, size), :], preferred_element_type=F32)
                if do_b:
                    p = jnp.exp2(s_ref[gb % 2, pl.ds(start, size), :] - m_b)
                    p_ref[gb % 2, pl.ds(start, size), :] = p.astype(BF16)
                    l8 = l8 + colsum8(p)
                return m8, l8, acc

            carry = stage(0, TILE, (neg, zero8, zacc))

            def body(c, carry, stage=stage):
                return stage(pl.multiple_of(TILE + c * KV_CHUNK, TILE), KV_CHUNK, carry)

            m8, l8, acc = lax.fori_loop(0, n_lat_chunks, body, carry, unroll=True)
            if do_c:
                finish(gc, acc, state["l8"])
            if do_b:
                state["l8"] = l8
            if do_a:
                state["m8"] = m8


def _attention(q, k, vt, *, batch, n_tiles, n_steps, group, dq, dv, q_per_kv):
    r = q.shape[0]
    t_len = n_tiles * TILE
    kvw = group // q_per_kv
    return pl.pallas_call(
        functools.partial(_attn_body, group=group, dq=dq, dv=dv, q_per_kv=q_per_kv,
                          n_lat_chunks=(t_len - TILE) // KV_CHUNK),
        grid=(batch, n_steps, n_tiles),
        in_specs=[pl.BlockSpec((TILE, group * dq), lambda b, h, i: (b * n_tiles + i, h)),
                  pl.BlockSpec((t_len, kvw * dq), lambda b, h, i: (b, h), pipeline_mode=pl.Buffered(1)),
                  pl.BlockSpec((None, kvw * dv, t_len), lambda b, h, i: (b, h, 0), pipeline_mode=pl.Buffered(1))],
        out_specs=pl.BlockSpec((TILE, group * dv), lambda b, h, i: (b * n_tiles + i, h)),
        out_shape=jax.ShapeDtypeStruct((r, n_steps * group * dv), BF16),
        scratch_shapes=[pltpu.VMEM((2, t_len, TILE), F32), pltpu.VMEM((2, t_len, TILE), BF16)],
        compiler_params=_cparams(("arbitrary", "arbitrary", "arbitrary")),
        name="attention",
    )(q, k, vt)


def _conv_body(x_ref, prev_ref, next_ref, w_ref, b_ref, u_ref, ext_ref, *, n_tiles):
    i = pl.program_id(0)
    seg = i % n_tiles
    has_prev = seg > 1
    has_next = jnp.logical_and(seg > 0, seg < n_tiles - 1)
    ext_ref[0:8, :] = jnp.where(has_prev, prev_ref[...], 0.0)
    ext_ref[8:8 + TILE, :] = x_ref[...]
    ext_ref[8 + TILE:16 + TILE, :] = jnp.where(has_next, next_ref[...], 0.0)
    left = CONV_W // 2
    acc = b_ref[...] + ext_ref[pl.ds(8 - left, TILE), :] * w_ref[0:1, :]
    for j in range(1, CONV_W):
        acc = acc + ext_ref[pl.ds(8 - left + j, TILE), :] * w_ref[j:j + 1, :]
    u_ref[...] = acc


def _dwconv(xw, conv_w, conv_b, n_tiles):
    r, w = xw.shape
    sub = TILE // 8
    n_sub = r // 8
    return pl.pallas_call(
        functools.partial(_conv_body, n_tiles=n_tiles),
        grid=(r // TILE,),
        in_specs=[pl.BlockSpec((TILE, w), lambda i: (i, 0)),
                  pl.BlockSpec((8, w), lambda i: (jnp.maximum(i * sub - 1, 0), 0)),
                  pl.BlockSpec((8, w), lambda i: (jnp.minimum((i + 1) * sub, n_sub - 1), 0)),
                  pl.BlockSpec((CONV_W, w), lambda i: (0, 0)),
                  pl.BlockSpec((1, w), lambda i: (0, 0))],
        out_specs=pl.BlockSpec((TILE, w), lambda i: (i, 0)),
        out_shape=jax.ShapeDtypeStruct((r, w), F32),
        scratch_shapes=[pltpu.VMEM((TILE + 16, w), F32)],
        compiler_params=_cparams(("arbitrary",)),
        name="dwconv",
    )(xw, xw, xw, conv_w, conv_b.reshape(1, w))


def _log_sigmoid(x):
    return jnp.minimum(x, 0.0) - jnp.log1p(jnp.exp(-jnp.abs(x)))


def _scan_tile(a, b, carry, reverse):
    n, w = a.shape
    groups = n // 8
    a3 = a.reshape(groups, 8, w)
    b3 = b.reshape(groups, 8, w)
    sub = lax.broadcasted_iota(jnp.int32, (groups, 8, w), 1)
    for s in (1, 2, 4):
        shift = 8 - s if reverse else s
        valid = (sub < 8 - s) if reverse else (sub >= s)
        a_sh = pltpu.roll(a3, shift, 1)
        b_sh = pltpu.roll(b3, shift, 1)
        b3 = jnp.where(valid, b3 + a3 * b_sh, b3)
        a3 = jnp.where(valid, a3 * a_sh, a3)
    outs = [None] * groups
    for r in (range(groups - 1, -1, -1) if reverse else range(groups)):
        hv = b3[r] + a3[r] * carry
        outs[r] = hv
        edge = hv[0:1, :] if reverse else hv[7:8, :]
        carry = jnp.broadcast_to(edge, (8, w))
    return jnp.concatenate(outs, axis=0), carry


def _lru_body(uf_ref, ub_ref, raw_ref, ixw_ref, rab_ref, ixb_ref, lam_ref,
              hf_ref, hb_ref, carry_ref, *, blk):
    j = pl.program_id(1)

    @pl.when(j == 0)
    def _():
        carry_ref[...] = jnp.zeros_like(carry_ref)

    for d, (u_ref, h_ref) in enumerate(((uf_ref, hf_ref), (ub_ref, hb_ref))):
        half_l2 = (0.5 * LRU_C * LOG2E) * _log_sigmoid(lam_ref[d])
        for n in range(LRU_BLOCKS):
            cs = slice(n * blk, (n + 1) * blk)
            u = u_ref[:, cs]
            ub16 = u.astype(BF16)
            t_r = jnp.tanh(jnp.dot(ub16, raw_ref[d, n], preferred_element_type=F32) + rab_ref[d][:, cs])
            t_g = jnp.tanh(jnp.dot(ub16, ixw_ref[d, n], preferred_element_type=F32) + ixb_ref[d][:, cs])
            a = jnp.exp2((t_r + 1.0) * half_l2[:, cs])
            om = 1.0 - a * a
            root = jnp.where(om > 0.0, om * lax.rsqrt(om), 0.0)
            bx = root * ((t_g + 1.0) * (0.5 * u))
            h, carry = _scan_tile(a, bx, carry_ref[d, :, cs], reverse=(d == 1))
            h_ref[:, cs] = h.astype(h_ref.dtype)
            carry_ref[d, :, cs] = carry


def _lru_scan(u, ra_w, ix_w, ra_b, ix_b, lam, *, batch, n_tiles):
    r, w = u.shape
    blk = w // LRU_BLOCKS

    def fwd_map(b, j):
        return (b * n_tiles + j, 0)

    def bwd_map(b, j):
        return (b * n_tiles + jnp.where(j == 0, 0, n_tiles - j), 0)

    full5 = pl.BlockSpec((2, LRU_BLOCKS, blk, blk), lambda b, j: (0, 0, 0, 0))
    vec = pl.BlockSpec((2, 1, w), lambda b, j: (0, 0, 0))
    return pl.pallas_call(
        functools.partial(_lru_body, blk=blk),
        grid=(batch, n_tiles),
        in_specs=[pl.BlockSpec((TILE, w), fwd_map), pl.BlockSpec((TILE, w), bwd_map),
                  full5, full5, vec, vec, vec],
        out_specs=[pl.BlockSpec((TILE, w), fwd_map), pl.BlockSpec((TILE, w), bwd_map)],
        out_shape=[jax.ShapeDtypeStruct((r, w), BF16), jax.ShapeDtypeStruct((r, w), BF16)],
        scratch_shapes=[pltpu.VMEM((2, 8, w), F32)],
        compiler_params=_cparams(("arbitrary", "arbitrary")),
        name="lru_scan",
    )(u, u, ra_w, ix_w, ra_b.reshape(2, 1, w), ix_b.reshape(2, 1, w), lam.reshape(2, 1, w))


def _rope_tables(n_lat, rot_dim):
    m = rot_dim // 2
    half = m // 2
    t = np.arange(n_lat)
    pos = np.stack([t // GRID_W, t % GRID_W], axis=1).astype(np.float32)
    inv = (ROPE_THETA ** (-(np.arange(half, dtype=np.float32) * 2.0) / m)).astype(np.float32)
    lane = np.arange(LANE)
    axis = np.minimum(lane // m, 1)
    freq = inv[lane % half]
    ang = pos[:, axis] * freq[None, :]
    live = (lane < rot_dim)[None, :]
    first = ((lane % m) < half)[None, :]
    cos = np.where(live, np.cos(ang), 1.0)
    sin = np.where(live, np.sin(ang), 0.0)
    s_hi = np.where(first, -sin, 0.0)
    s_lo = np.where(first, 0.0, sin)
    ident = np.concatenate([np.ones((TILE, LANE)), np.zeros((TILE, LANE)), np.zeros((TILE, LANE))], 1)
    tab = np.concatenate([cos, s_hi, s_lo], axis=1)
    tab = np.concatenate([ident, tab], axis=0).astype(np.float32)
    return (jnp.asarray(tab[:, :LANE]), jnp.asarray(tab[:, LANE:2 * LANE]), jnp.asarray(tab[:, 2 * LANE:]))


def kernel(x, c, ctx, c_ctx, ada_w, ada_b, ln_g, ln_b, mlp_w1, mlp_w2, gqa_wq, gqa_wk, gqa_wv, gqa_wo, gqa_q_g, gqa_k_g, lru_wx, lru_wy, lru_conv_w, lru_conv_b, lru_ra_w, lru_ra_b, lru_ix_w, lru_ix_b, lru_lam, lru_wo, mla_wq_a, mla_q_a_g, mla_wq_b, mla_wkv_a, mla_kv_a_g, mla_wkv_b, mla_wo):
    batch, n_lat, d = x.shape
    depth = ada_w.shape[0]
    assert ctx.shape[1] == TILE and n_lat % KV_CHUNK == 0 and n_lat % GRID_W == 0
    t_len = TILE + n_lat
    n_tiles = t_len // TILE
    rows = batch * t_len
    n_groups = rows // TILE
    alpha = (2 * depth) ** 0.25


    cond_rows = -(-(batch + 1) // 8) * 8
    cond = jnp.concatenate([c, c_ctx[None, :], jnp.zeros((cond_rows - batch - 1, d), F32)], axis=0)
    mods = _ada_mods(cond, ada_w, ada_b)
    m_lat = jnp.broadcast_to(mods[:, :batch, None, :], (depth, batch, n_tiles - 1, 6 * d))
    m_ctx = jnp.broadcast_to(mods[:, batch:batch + 1, None, :], (depth, batch, 1, 6 * d))
    mods = jnp.concatenate([m_ctx, m_lat], axis=2).reshape(depth, n_groups, 6, d)

    def mod3(gate, scale, shift):
        return jnp.stack([gate, scale, shift], axis=1)

    tabs_gqa = _rope_tables(n_lat, A_HEAD_DIM)
    tabs_mla = _rope_tables(n_lat, MLA_ROPE)

    def tab_specs():
        return [pl.BlockSpec((TILE, LANE), lambda j, i, k: (i % n_tiles, 0)) for _ in range(3)]

    def vt_spec(n):
        return pl.BlockSpec((None, n, TILE), lambda j, i, k: (i // n_tiles, 0, i % n_tiles))

    def lat_tile(i):
        return i + i // (n_tiles - 1) + 1

    def resid_ln(a_list, w, x_old, mod, lng, lnb, *, tk, prologue=None, name=None, latent_only=False):
        rmap = lat_tile if latent_only else (lambda i: i)
        out_rows = batch * n_lat if latent_only else rows
        ex = [(x_old, pl.BlockSpec((TILE, d), lambda j, i, k: (rmap(i), 0))),
              (mod, pl.BlockSpec((1, 3, d), lambda j, i, k: (rmap(i), 0, 0))),
              (lng.reshape(1, d), _full_spec((1, d))), (lnb.reshape(1, d), _full_spec((1, d)))]
        outs = [(jax.ShapeDtypeStruct((out_rows, d), F32), _out_spec(TILE, d)),
                (jax.ShapeDtypeStruct((out_rows, d), BF16), _out_spec(TILE, d))]
        return _matmul(a_list, w, ex, outs, tm=TILE, tn=d, tk=tk, prologue=prologue, name=name,
                       m_tiles=out_rows // TILE, row_map=rmap,
                       epilogue=functools.partial(_ep_resid_ln_mod, ngroups=1, alpha=alpha))

    def plain(a, w, *, tm, tn, epilogue=_ep_cast, dtype=BF16, name=None):
        n = w.shape[1]
        outs = [(jax.ShapeDtypeStruct((a.shape[0], n), dtype), _out_spec(tm, tn))]
        return _matmul([a], w, [], outs, tm=tm, tn=tn, tk=a.shape[1], epilogue=epilogue, name=name)[0]

    xs, h = _modulate(x, ctx, mods[0])
    w1_all = mlp_w1.astype(BF16)
    w2_all = mlp_w2.astype(BF16)

    for i in range(depth):
        kind = i % 3
        slot = i // 3
        m_i = mods[i]
        if kind == 0:
            wq, wk, wv, wo = (gqa_wq[slot].astype(BF16), gqa_wk[slot].astype(BF16),
                              gqa_wv[slot].astype(BF16), gqa_wo[slot].astype(BF16))

            nq, nkv = wq.shape[1], wk.shape[1]
            wqkv = jnp.concatenate([wq, wk, wv], axis=1)
            ex = [(gqa_q_g[slot].reshape(1, LANE), _full_spec((1, LANE))),
                  (gqa_k_g[slot].reshape(1, LANE), _full_spec((1, LANE)))] + list(zip(tabs_gqa, tab_specs()))
            outs = [(jax.ShapeDtypeStruct((rows, nq), BF16), pl.BlockSpec((TILE, nq), lambda j, ii, kk: (ii, 0))),
                    (jax.ShapeDtypeStruct((rows, nkv), BF16), pl.BlockSpec((TILE, nkv), lambda j, ii, kk: (ii, 0))),
                    (jax.ShapeDtypeStruct((batch, nkv, t_len), BF16), vt_spec(nkv))]
            q, k, vt = _matmul([h], wqkv, ex, outs, tm=TILE, tn=wqkv.shape[1], tk=d, name="gqa_qkv",
                               epilogue=functools.partial(_ep_gqa_qkv, nq=nq // LANE, nkv=nkv // LANE,
                                                          off=A_HEAD_DIM // 4,
                                                          qscale=A_HEAD_DIM ** -0.5 * LOG2E))
            o = _attention(q, k, vt, batch=batch, n_tiles=n_tiles, n_steps=A_HEADS // GQA_STEP_HEADS,
                           group=GQA_STEP_HEADS, dq=A_HEAD_DIM, dv=A_HEAD_DIM, q_per_kv=A_GROUP)
            a_list, w_out, prologue = [o], wo, None
        elif kind == 1:
            wx, wy, wo = lru_wx[slot].astype(BF16), lru_wy[slot].astype(BF16), lru_wo[slot].astype(BF16)
            xw = plain(h, wx, tm=2 * TILE, tn=wx.shape[1], dtype=F32, name="lru_x")
            y = plain(h, wy, tm=2 * TILE, tn=wy.shape[1], epilogue=_ep_gelu, name="lru_y")
            u = _dwconv(xw, lru_conv_w[slot], lru_conv_b[slot], n_tiles)
            hf, hb = _lru_scan(u, (0.5 * lru_ra_w[slot]).astype(BF16), (0.5 * lru_ix_w[slot]).astype(BF16),
                               0.5 * lru_ra_b[slot], 0.5 * lru_ix_b[slot], lru_lam[slot],
                               batch=batch, n_tiles=n_tiles)
            a_list, w_out, prologue = [hf, hb, y], wo, _pro_lru_gate
        else:
            nq = MLA_NOPE + MLA_ROPE
            wq_a = mla_wq_a[slot].astype(BF16)
            wq_b = mla_wq_b[slot].reshape(MLA_LORA, MLA_HEADS, nq)
            wq_b = jnp.concatenate([wq_b, jnp.zeros((MLA_LORA, MLA_HEADS, 2 * LANE - nq), F32)], axis=-1)
            wq_b = wq_b.reshape(MLA_LORA, MLA_HEADS * 2 * LANE).astype(BF16)
            wkv_a = jnp.concatenate([mla_wkv_a[slot], jnp.zeros((d, LANE - MLA_ROPE), F32)], axis=-1).astype(BF16)
            wkv_b = mla_wkv_b[slot].reshape(MLA_LORA, MLA_HEADS, MLA_NOPE + MLA_V)
            wkv_b = jnp.concatenate([wkv_b[:, :, :MLA_NOPE].reshape(MLA_LORA, -1),
                                     wkv_b[:, :, MLA_NOPE:].reshape(MLA_LORA, -1)], axis=-1).astype(BF16)
            wo = mla_wo[slot].astype(BF16)

            q, kfull, vt = _mla_proj(h, jnp.concatenate([wq_a, wkv_a], axis=1), wq_b, wkv_b,
                                     mla_q_a_g[slot], mla_kv_a_g[slot], tabs_mla,
                                     batch=batch, n_tiles=n_tiles, qscale=float(nq) ** -0.5 * LOG2E)
            o = _attention(q, kfull, vt, batch=batch, n_tiles=n_tiles, n_steps=MLA_HEADS // MLA_STEP_HEADS,
                           group=MLA_STEP_HEADS, dq=2 * LANE, dv=MLA_V, q_per_kv=1)
            a_list, w_out, prologue = [o], wo, None

        last = i == depth - 1
        xs, h2 = resid_ln(a_list, w_out, xs, mod3(m_i[:, 2], m_i[:, 4], m_i[:, 3]),
                          ln_g[i, 0], ln_b[i, 0], tk=w_out.shape[0], prologue=prologue,
                          name="mix_out", latent_only=last)
        if not last:
            nxt = mod3(m_i[:, 5], mods[i + 1][:, 1], mods[i + 1][:, 0])
        else:
            zero = jnp.zeros_like(m_i[:, 5])
            nxt = mod3(m_i[:, 5], zero, zero)
            nxt = nxt.reshape(batch, n_tiles, 3, d)[:, 1:].reshape(batch * (n_tiles - 1), 3, d)
        xs, h = _mlp(h2, w1_all, w2_all, i, xs, nxt, ln_g[i, 1], ln_b[i, 1], tm=2 * TILE, tf=1024,
                     alpha=alpha)

    return xs.reshape(batch, n_lat, d)
```

```python
import functools
import math

import numpy as np
import jax
import jax.numpy as jnp
from jax import lax
from jax.experimental import pallas as pl
from jax.experimental.pallas import tpu as pltpu

F32 = jnp.float32
BF16 = jnp.bfloat16

TILE = 256
GRID_W = 64
ROPE_THETA = 10000.0
EPS = 1e-6
LANE = 128
VMEM_LIMIT_MB = 56

A_HEADS = 16
A_KV_HEADS = 4
A_GROUP = A_HEADS // A_KV_HEADS
A_HEAD_DIM = 128

LRU_BLOCKS = 8
CONV_W = 4
LRU_C = 8.0

MLA_HEADS = 16
MLA_LORA = 512
MLA_NOPE = 128
MLA_ROPE = 64
MLA_V = 128

KV_CHUNK = 512
GQA_STEP_HEADS = 8
MLA_STEP_HEADS = 8
LOG2E = math.log2(math.e)


def _cparams(sem, vmem_mb=VMEM_LIMIT_MB):
    return pltpu.CompilerParams(dimension_semantics=sem,
                                vmem_limit_bytes=vmem_mb * 1024 * 1024)


def _proj_body(*refs, n_ex, epilogue):
    a_ref, w_ref = refs[0], refs[1]
    ex = refs[2:2 + n_ex]
    outs = refs[2 + n_ex:]
    epilogue(jnp.dot(a_ref[...], w_ref[...], preferred_element_type=F32), ex, outs)


def _const_spec(shape):
    zeros = (0,) * len(shape)
    return pl.BlockSpec(shape, lambda i: zeros, pipeline_mode=pl.Buffered(1))


def _proj(a, w, extras, outs, *, tm, epilogue, name=None):
    m, kdim = a.shape
    assert m % tm == 0 and w.shape[0] == kdim
    return pl.pallas_call(
        functools.partial(_proj_body, n_ex=len(extras), epilogue=epilogue),
        grid=(m // tm,),
        in_specs=[pl.BlockSpec((tm, kdim), lambda i: (i, 0)), _const_spec(w.shape)] + [s for _, s in extras],
        out_specs=[s for _, s in outs],
        out_shape=[s for s, _ in outs],
        compiler_params=_cparams(("arbitrary",)),
        name=name,
    )(a, w, *[arr for arr, _ in extras])


def _row_spec(tm, width):
    return pl.BlockSpec((tm, width), lambda i: (i, 0))


def _mix_out_body(*refs, n_a, ngroups, prologue, alpha):
    a_refs = refs[:n_a]
    w_ref, xold_ref, mod_ref, lng_ref, lnb_ref, x_out, h_out = refs[n_a:]
    for g in range(ngroups):
        r0 = g * TILE
        vals = [r[r0:r0 + TILE, :] for r in a_refs]
        a = vals[0] if prologue is None else prologue(vals)
        part = jnp.dot(a, w_ref[...], preferred_element_type=F32)
        _resid_ln_rows(part, r0, xold_ref, mod_ref, lng_ref, lnb_ref, x_out, h_out, alpha)


def _mix_out(a_list, w, x_old, mod, lng, lnb, *, ngroups, out_rows, row_map, prologue, alpha):
    kdim, d = w.shape
    tm = ngroups * TILE
    src = lambda i: (row_map(i), 0)
    dst = pl.BlockSpec((tm, d), lambda i: (i, 0))
    return pl.pallas_call(
        functools.partial(_mix_out_body, n_a=len(a_list), ngroups=ngroups, prologue=prologue, alpha=alpha),
        grid=(out_rows // tm,),
        in_specs=[pl.BlockSpec((tm, kdim), src) for _ in a_list]
        + [_const_spec(w.shape), pl.BlockSpec((tm, d), src),
           pl.BlockSpec((ngroups, 3, d), lambda i: (row_map(i), 0, 0)),
           _const_spec((1, d)), _const_spec((1, d))],
        out_specs=[dst, dst],
        out_shape=[jax.ShapeDtypeStruct((out_rows, d), F32), jax.ShapeDtypeStruct((out_rows, d), BF16)],
        compiler_params=_cparams(("arbitrary",)),
        name="mix_out",
    )(*a_list, w, x_old, mod, lng.reshape(1, d), lnb.reshape(1, d))


def _rope(x, cos, s_hi, s_lo, off):
    return x * cos + pltpu.roll(x, LANE - off, 1) * s_hi + pltpu.roll(x, off, 1) * s_lo


def _ep_headnorm_rope(acc, ex, outs, *, nblk, off, scale):
    g_ref, cos_ref, shi_ref, slo_ref = ex
    g = g_ref[...]
    cos, shi, slo = cos_ref[...], shi_ref[...], slo_ref[...]
    for j in range(nblk):
        x = acc[:, j * LANE:(j + 1) * LANE]
        ms = jnp.mean(x * x, axis=-1, keepdims=True)
        xn = x * lax.rsqrt(ms + EPS) * g
        y = _rope(xn, cos, shi, slo, off)
        if scale != 1.0:
            y = y * scale
        outs[0][:, j * LANE:(j + 1) * LANE] = y.astype(outs[0].dtype)


def _ep_gqa_qkv(acc, ex, outs, *, nq, nkv, off, qscale):
    qg_ref, kg_ref, cos_ref, shi_ref, slo_ref = ex
    q_out, k_out, vt_out = outs
    tabs = (cos_ref, shi_ref, slo_ref)
    _ep_headnorm_rope(acc[:, :nq * LANE], (qg_ref,) + tabs, (q_out,), nblk=nq, off=off, scale=qscale)
    _ep_headnorm_rope(acc[:, nq * LANE:(nq + nkv) * LANE], (kg_ref,) + tabs, (k_out,), nblk=nkv, off=off,
                      scale=1.0)
    vt_out[...] = acc[:, (nq + nkv) * LANE:].T.astype(vt_out.dtype)


def _rms(x, g):
    ms = jnp.mean(x * x, axis=-1, keepdims=True)
    return x * lax.rsqrt(ms + EPS) * g


def _ep_mla_q(acc, ex, outs, *, nheads, off, scale):
    cos_ref, shi_ref, slo_ref = ex
    cos, shi, slo = cos_ref[...], shi_ref[...], slo_ref[...]
    for h in range(nheads):
        c0 = h * 2 * LANE
        outs[0][:, c0:c0 + LANE] = (acc[:, c0:c0 + LANE] * scale).astype(outs[0].dtype)
        pe = acc[:, c0 + LANE:c0 + 2 * LANE]
        outs[0][:, c0 + LANE:c0 + 2 * LANE] = (_rope(pe, cos, shi, slo, off) * scale).astype(outs[0].dtype)


def _ep_mla_kv(acc, ex, outs, *, nheads):
    kpe_ref, = ex
    k_out, vt_out = outs
    kpe = kpe_ref[...]
    for h in range(nheads):
        k_out[:, 2 * h * LANE:(2 * h + 1) * LANE] = acc[:, h * LANE:(h + 1) * LANE].astype(k_out.dtype)
        k_out[:, (2 * h + 1) * LANE:(2 * h + 2) * LANE] = kpe
    vt_out[...] = acc[:, nheads * LANE:2 * nheads * LANE].T.astype(vt_out.dtype)


def _resid_ln_rows(f_out, r0, xold_ref, mod_ref, lng_ref, lnb_ref, x_out, h_out, alpha):
    n = f_out.shape[0]
    g = r0 // TILE
    assert r0 % TILE + n <= TILE
    gate = mod_ref[g, 0:1, :]
    scale = mod_ref[g, 1:2, :]
    shift = mod_ref[g, 2:3, :]
    y = alpha * xold_ref[r0:r0 + n, :] + (1.0 + gate) * f_out
    mu = jnp.mean(y, axis=-1, keepdims=True)
    yc = y - mu
    var = jnp.mean(yc * yc, axis=-1, keepdims=True)
    xn = yc * lax.rsqrt(var + EPS) * lng_ref[...] + lnb_ref[...]
    x_out[r0:r0 + n, :] = xn
    h_out[r0:r0 + n, :] = (xn * (1.0 + scale) + shift).astype(h_out.dtype)


def _pro_lru_gate(vals):
    hf, hb, y = vals
    return ((hf.astype(F32) + hb.astype(F32)) * y.astype(F32)).astype(BF16)


def _ep_lru_xy(acc, ex, outs):
    w = outs[0].shape[1]
    outs[0][...] = acc[:, 0:w]
    outs[1][...] = jax.nn.gelu(acc[:, w:2 * w], approximate=True).astype(outs[1].dtype)


def _mlp_body(h_ref, w1_ref, w2_ref, xold_ref, mod_ref, lng_ref, lnb_ref, x_out, h_out, acc_ref,
              *, nf, ngroups, alpha):
    f = pl.program_id(1)

    @pl.when(f == 0)
    def _():
        acc_ref[...] = jnp.zeros_like(acc_ref)

    def hidden():
        u = jnp.maximum(jnp.dot(h_ref[...], w1_ref[...], preferred_element_type=F32), 0.0)
        return (u * u).astype(BF16)

    @pl.when(f < nf - 1)
    def _():
        acc_ref[...] += jnp.dot(hidden(), w2_ref[...], preferred_element_type=F32)

    @pl.when(f == nf - 1)
    def _():
        u2 = hidden()
        for g in range(ngroups):
            r0 = g * TILE
            tot = acc_ref[r0:r0 + TILE, :] + jnp.dot(u2[r0:r0 + TILE, :], w2_ref[...],
                                                   preferred_element_type=F32)
            _resid_ln_rows(tot, r0, xold_ref, mod_ref, lng_ref, lnb_ref, x_out, h_out, alpha)


def _mlp(h, w1, w2, layer, x_old, mod, lng, lnb, *, tm, tf, alpha):
    rows, d = h.shape
    ff = w1.shape[2]
    nf = ff // tf
    row = pl.BlockSpec((tm, d), lambda i, f: (i, 0))
    vec = pl.BlockSpec((1, d), lambda i, f: (0, 0))
    return pl.pallas_call(
        functools.partial(_mlp_body, nf=nf, ngroups=tm // TILE, alpha=alpha),
        grid=(rows // tm, nf),
        in_specs=[row,
                  pl.BlockSpec((None, d, tf), lambda i, f: (layer, 0, f)),
                  pl.BlockSpec((None, tf, d), lambda i, f: (layer, f, 0)),
                  row,
                  pl.BlockSpec((tm // TILE, 3, d), lambda i, f: (i, 0, 0)),
                  vec, vec],
        out_specs=[row, row],
        out_shape=[jax.ShapeDtypeStruct((rows, d), F32), jax.ShapeDtypeStruct((rows, d), BF16)],
        scratch_shapes=[pltpu.VMEM((tm, d), F32)],
        compiler_params=_cparams(("arbitrary", "arbitrary")),
        name="mlp",
    )(h, w1, w2, x_old, mod, lng.reshape(1, d), lnb.reshape(1, d))


def _mla_proj_body(h_ref, wa_ref, wqb_ref, wkvb_ref, qg_ref, kvg_ref, cos_ref, shi_ref, slo_ref,
                   q_out, k_out, vt_out, *, lora, nheads, off, qscale):
    tabs = (cos_ref, shi_ref, slo_ref)
    t = jnp.dot(h_ref[...], wa_ref[...], preferred_element_type=F32)
    qa = _rms(t[:, 0:lora], qg_ref[...]).astype(BF16)
    ckv = _rms(t[:, lora:2 * lora], kvg_ref[...]).astype(BF16)
    kpe = _rope(t[:, 2 * lora:2 * lora + LANE], cos_ref[...], shi_ref[...], slo_ref[...], off).astype(BF16)
    _ep_mla_q(jnp.dot(qa, wqb_ref[...], preferred_element_type=F32), tabs, (q_out,),
              nheads=nheads, off=off, scale=qscale)
    _ep_mla_kv(jnp.dot(ckv, wkvb_ref[...], preferred_element_type=F32), (kpe,), (k_out, vt_out),
               nheads=nheads)


def _mla_proj(h, w_a, wq_b, wkv_b, q_g, kv_g, tabs, *, batch, n_tiles, qscale):
    rows, d = h.shape
    t_len = n_tiles * TILE
    nq, nkv = wq_b.shape[1], wkv_b.shape[1]
    nv = MLA_HEADS * MLA_V

    def const(shape):
        return pl.BlockSpec(shape, lambda i: (0,) * len(shape), pipeline_mode=pl.Buffered(1))

    tab = pl.BlockSpec((TILE, LANE), lambda i: (i % n_tiles, 0))
    return pl.pallas_call(
        functools.partial(_mla_proj_body, lora=MLA_LORA, nheads=MLA_HEADS, off=MLA_ROPE // 4, qscale=qscale),
        grid=(rows // TILE,),
        in_specs=[pl.BlockSpec((TILE, d), lambda i: (i, 0)),
                  const(w_a.shape), const(wq_b.shape), const(wkv_b.shape),
                  const((1, MLA_LORA)), const((1, MLA_LORA)), tab, tab, tab],
        out_specs=[pl.BlockSpec((TILE, nq), lambda i: (i, 0)),
                   pl.BlockSpec((TILE, nq), lambda i: (i, 0)),
                   pl.BlockSpec((None, nv, TILE), lambda i: (i // n_tiles, 0, i % n_tiles))],
        out_shape=[jax.ShapeDtypeStruct((rows, nq), BF16), jax.ShapeDtypeStruct((rows, nq), BF16),
                   jax.ShapeDtypeStruct((batch, nv, t_len), BF16)],
        compiler_params=_cparams(("arbitrary",)),
        name="mla_proj",
    )(h, w_a, wq_b, wkv_b, q_g.reshape(1, MLA_LORA), kv_g.reshape(1, MLA_LORA), *tabs)


def _ada_body(c_ref, w_ref, b_ref, o_ref):
    cond = c_ref[...]
    sc = (cond * jax.nn.sigmoid(cond)).astype(BF16)
    o_ref[0] = jnp.dot(sc, w_ref[0].astype(BF16), preferred_element_type=F32) + b_ref[0]


def _ada_mods(cond, ada_w, ada_b, tn=1024):
    depth, d, n = ada_w.shape
    rows = cond.shape[0]
    return pl.pallas_call(
        _ada_body,
        grid=(depth, n // tn),
        in_specs=[pl.BlockSpec((rows, d), lambda l, j: (0, 0)),
                  pl.BlockSpec((1, d, tn), lambda l, j: (l, 0, j)),
                  pl.BlockSpec((1, 1, tn), lambda l, j: (l, 0, j))],
        out_specs=pl.BlockSpec((1, rows, tn), lambda l, j: (l, 0, j)),
        out_shape=jax.ShapeDtypeStruct((depth, rows, n), F32),
        compiler_params=_cparams(("arbitrary", "arbitrary")),
        name="ada_mods",
    )(cond, ada_w, ada_b.reshape(depth, 1, n))


def _modulate_body(ctx_ref, x_ref, mod_ref, xs_ref, h_ref):
    j = pl.program_id(1)

    def emit(src):
        xs_ref[...] = src
        h_ref[...] = (src * (1.0 + mod_ref[0, 1:2, :]) + mod_ref[0, 0:1, :]).astype(h_ref.dtype)

    @pl.when(j == 0)
    def _():
        emit(ctx_ref[...])

    @pl.when(j > 0)
    def _():
        emit(x_ref[...])


def _modulate(x, ctx, mod):
    batch, n_lat, d = x.shape
    n_tiles = n_lat // TILE + 1
    rows = batch * n_tiles * TILE
    row = pl.BlockSpec((TILE, d), lambda b, j: (b * n_tiles + j, 0))
    return pl.pallas_call(
        _modulate_body,
        grid=(batch, n_tiles),
        in_specs=[pl.BlockSpec((None, TILE, d), lambda b, j: (b, 0, 0)),
                  pl.BlockSpec((None, TILE, d), lambda b, j: (b, jnp.maximum(j - 1, 0), 0)),
                  pl.BlockSpec((1, mod.shape[1], d), lambda b, j: (b * n_tiles + j, 0, 0))],
        out_specs=[row, row],
        out_shape=[jax.ShapeDtypeStruct((rows, d), F32), jax.ShapeDtypeStruct((rows, d), BF16)],
        compiler_params=_cparams(("arbitrary", "arbitrary")),
        name="modulate",
    )(ctx, x, mod)


def _dot_nt(a, b):
    return lax.dot_general(a, b, (((1,), (1,)), ((), ())), preferred_element_type=F32)


def _attn_body(q_ref, k_ref, vt_ref, o_ref, s_ref, p_ref, *, group, dq, dv, q_per_kv, n_lat_chunks):
    i = pl.program_id(2)

    def kcol(g):
        return (g // q_per_kv) * dq

    def vrow(g):
        return (g // q_per_kv) * dv

    def finish(g, acc, l8):
        l = jnp.sum(l8, axis=0, keepdims=True)
        out = acc * (1.0 / l)
        o_ref[:, g * dv:(g + 1) * dv] = out.T.astype(o_ref.dtype)

    def colmax8(x):
        return jnp.max(x.reshape(x.shape[0] // 8, 8, x.shape[1]), axis=0)

    def colsum8(x):
        return jnp.sum(x.reshape(x.shape[0] // 8, 8, x.shape[1]), axis=0)

    @pl.when(i == 0)
    def _():
        for g in range(group):
            st = _dot_nt(k_ref[0:TILE, kcol(g):kcol(g) + dq], q_ref[:, g * dq:(g + 1) * dq])
            m = jnp.max(st, axis=0, keepdims=True)
            p = jnp.exp2(st - m)
            acc = jnp.dot(vt_ref[vrow(g):vrow(g) + dv, 0:TILE], p.astype(BF16), preferred_element_type=F32)
            finish(g, acc, colsum8(p))

    @pl.when(i > 0)
    def _():
        neg = jnp.full((8, TILE), -jnp.inf, F32)
        zero8 = jnp.zeros((8, TILE), F32)
        zacc = jnp.zeros((dv, TILE), F32)
        state = {}
        for ph in range(group + 2):
            ga, gb, gc = ph, ph - 1, ph - 2
            do_a, do_b, do_c = ga < group, 0 <= gb < group, 0 <= gc < group
            q_a = q_ref[:, ga * dq:(ga + 1) * dq] if do_a else None
            m_b = jnp.max(state["m8"], axis=0, keepdims=True) if do_b else None

            def stage(start, size, carry, q_a=q_a, m_b=m_b, ga=ga, gb=gb, gc=gc,
                      do_a=do_a, do_b=do_b, do_c=do_c):
                m8, l8, acc = carry
                if do_c:
                    acc = acc + jnp.dot(vt_ref[vrow(gc):vrow(gc) + dv, pl.ds(start, size)],
                                        p_ref[gc % 2, pl.ds(start, size), :], preferred_element_type=F32)
                if do_a:
                    st = _dot_nt(k_ref[pl.ds(start, size), kcol(ga):kcol(ga) + dq], q_a)
                    s_ref[ga % 2, pl.ds(start, size), :] = st
                    m8 = jnp.maximum(m8, colmax8(st))
                if do_b:
                    p = jnp.exp2(s_ref[gb % 2, pl.ds(start, size), :] - m_b)
                    p_ref[gb % 2, pl.ds(start, size), :] = p.astype(BF16)
                    l8 = l8 + colsum8(p)
                return m8, l8, acc

            carry = stage(0, TILE, (neg, zero8, zacc))

            def body(c, carry, stage=stage):
                return stage(pl.multiple_of(TILE + c * KV_CHUNK, TILE), KV_CHUNK, carry)

            m8, l8, acc = lax.fori_loop(0, n_lat_chunks, body, carry, unroll=True)
            if do_c:
                finish(gc, acc, state["l8"])
            if do_b:
                state["l8"] = l8
            if do_a:
                state["m8"] = m8


def _attention(q, k, vt, *, batch, n_tiles, n_steps, group, dq, dv, q_per_kv):
    r = q.shape[0]
    t_len = n_tiles * TILE
    kvw = group // q_per_kv
    return pl.pallas_call(
        functools.partial(_attn_body, group=group, dq=dq, dv=dv, q_per_kv=q_per_kv,
                          n_lat_chunks=(t_len - TILE) // KV_CHUNK),
        grid=(batch, n_steps, n_tiles),
        in_specs=[pl.BlockSpec((TILE, group * dq), lambda b, h, i: (b * n_tiles + i, h)),
                  pl.BlockSpec((t_len, kvw * dq), lambda b, h, i: (b, h), pipeline_mode=pl.Buffered(1)),
                  pl.BlockSpec((None, kvw * dv, t_len), lambda b, h, i: (b, h, 0), pipeline_mode=pl.Buffered(1))],
        out_specs=pl.BlockSpec((TILE, group * dv), lambda b, h, i: (b * n_tiles + i, h)),
        out_shape=jax.ShapeDtypeStruct((r, n_steps * group * dv), BF16),
        scratch_shapes=[pltpu.VMEM((2, t_len, TILE), F32), pltpu.VMEM((2, t_len, TILE), BF16)],
        compiler_params=_cparams(("arbitrary", "arbitrary", "arbitrary")),
        name="attention",
    )(q, k, vt)


def _conv_body(x_ref, prev_ref, next_ref, w_ref, b_ref, u_ref, ext_ref, *, n_tiles):
    i = pl.program_id(0)
    seg = i % n_tiles
    has_prev = seg > 1
    has_next = jnp.logical_and(seg > 0, seg < n_tiles - 1)
    ext_ref[0:8, :] = jnp.where(has_prev, prev_ref[...], 0.0)
    ext_ref[8:8 + TILE, :] = x_ref[...]
    ext_ref[8 + TILE:16 + TILE, :] = jnp.where(has_next, next_ref[...], 0.0)
    left = CONV_W // 2
    acc = b_ref[...] + ext_ref[pl.ds(8 - left, TILE), :] * w_ref[0:1, :]
    for j in range(1, CONV_W):
        acc = acc + ext_ref[pl.ds(8 - left + j, TILE), :] * w_ref[j:j + 1, :]
    u_ref[...] = acc


def _dwconv(xw, conv_w, conv_b, n_tiles):
    r, w = xw.shape
    sub = TILE // 8
    n_sub = r // 8
    return pl.pallas_call(
        functools.partial(_conv_body, n_tiles=n_tiles),
        grid=(r // TILE,),
        in_specs=[pl.BlockSpec((TILE, w), lambda i: (i, 0)),
                  pl.BlockSpec((8, w), lambda i: (jnp.maximum(i * sub - 1, 0), 0)),
                  pl.BlockSpec((8, w), lambda i: (jnp.minimum((i + 1) * sub, n_sub - 1), 0)),
                  pl.BlockSpec((CONV_W, w), lambda i: (0, 0)),
                  pl.BlockSpec((1, w), lambda i: (0, 0))],
        out_specs=pl.BlockSpec((TILE, w), lambda i: (i, 0)),
        out_shape=jax.ShapeDtypeStruct((r, w), F32),
        scratch_shapes=[pltpu.VMEM((TILE + 16, w), F32)],
        compiler_params=_cparams(("arbitrary",)),
        name="dwconv",
    )(xw, xw, xw, conv_w, conv_b.reshape(1, w))


def _log_sigmoid(x):
    return jnp.minimum(x, 0.0) - jnp.log1p(jnp.exp(-jnp.abs(x)))


def _scan_tile(a, b, carry, reverse):
    n, w = a.shape
    groups = n // 8
    a3 = a.reshape(groups, 8, w)
    b3 = b.reshape(groups, 8, w)
    sub = lax.broadcasted_iota(jnp.int32, (groups, 8, w), 1)
    for s in (1, 2, 4):
        shift = 8 - s if reverse else s
        valid = (sub < 8 - s) if reverse else (sub >= s)
        a_sh = pltpu.roll(a3, shift, 1)
        b_sh = pltpu.roll(b3, shift, 1)
        b3 = jnp.where(valid, b3 + a3 * b_sh, b3)
        a3 = jnp.where(valid, a3 * a_sh, a3)
    outs = [None] * groups
    for r in (range(groups - 1, -1, -1) if reverse else range(groups)):
        hv = b3[r] + a3[r] * carry
        outs[r] = hv
        edge = hv[0:1, :] if reverse else hv[7:8, :]
        carry = jnp.broadcast_to(edge, (8, w))
    return jnp.concatenate(outs, axis=0), carry


def _lru_body(uf_ref, ub_ref, raw_ref, ixw_ref, rab_ref, ixb_ref, lam_ref,
              hf_ref, hb_ref, carry_ref, *, blk):
    j = pl.program_id(1)

    @pl.when(j == 0)
    def _():
        carry_ref[...] = jnp.zeros_like(carry_ref)

    for d, (u_ref, h_ref) in enumerate(((uf_ref, hf_ref), (ub_ref, hb_ref))):
        half_l2 = (0.5 * LRU_C * LOG2E) * _log_sigmoid(lam_ref[d])
        for n in range(LRU_BLOCKS):
            cs = slice(n * blk, (n + 1) * blk)
            u = u_ref[:, cs]
            ub16 = u.astype(BF16)
            t_r = jnp.tanh(jnp.dot(ub16, raw_ref[d, n], preferred_element_type=F32) + rab_ref[d][:, cs])
            t_g = jnp.tanh(jnp.dot(ub16, ixw_ref[d, n], preferred_element_type=F32) + ixb_ref[d][:, cs])
            a = jnp.exp2((t_r + 1.0) * half_l2[:, cs])
            om = 1.0 - a * a
            root = jnp.where(om > 0.0, om * lax.rsqrt(om), 0.0)
            bx = root * ((t_g + 1.0) * (0.5 * u))
            h, carry = _scan_tile(a, bx, carry_ref[d, :, cs], reverse=(d == 1))
            h_ref[:, cs] = h.astype(h_ref.dtype)
            carry_ref[d, :, cs] = carry


def _lru_scan(u, ra_w, ix_w, ra_b, ix_b, lam, *, batch, n_tiles):
    r, w = u.shape
    blk = w // LRU_BLOCKS

    def fwd_map(b, j):
        return (b * n_tiles + j, 0)

    def bwd_map(b, j):
        return (b * n_tiles + jnp.where(j == 0, 0, n_tiles - j), 0)

    full5 = pl.BlockSpec((2, LRU_BLOCKS, blk, blk), lambda b, j: (0, 0, 0, 0))
    vec = pl.BlockSpec((2, 1, w), lambda b, j: (0, 0, 0))
    return pl.pallas_call(
        functools.partial(_lru_body, blk=blk),
        grid=(batch, n_tiles),
        in_specs=[pl.BlockSpec((TILE, w), fwd_map), pl.BlockSpec((TILE, w), bwd_map),
                  full5, full5, vec, vec, vec],
        out_specs=[pl.BlockSpec((TILE, w), fwd_map), pl.BlockSpec((TILE, w), bwd_map)],
        out_shape=[jax.ShapeDtypeStruct((r, w), BF16), jax.ShapeDtypeStruct((r, w), BF16)],
        scratch_shapes=[pltpu.VMEM((2, 8, w), F32)],
        compiler_params=_cparams(("arbitrary", "arbitrary")),
        name="lru_scan",
    )(u, u, ra_w, ix_w, ra_b.reshape(2, 1, w), ix_b.reshape(2, 1, w), lam.reshape(2, 1, w))


def _rope_tables(n_lat, rot_dim):
    m = rot_dim // 2
    half = m // 2
    t = np.arange(n_lat)
    pos = np.stack([t // GRID_W, t % GRID_W], axis=1).astype(np.float32)
    inv = (ROPE_THETA ** (-(np.arange(half, dtype=np.float32) * 2.0) / m)).astype(np.float32)
    lane = np.arange(LANE)
    axis = np.minimum(lane // m, 1)
    freq = inv[lane % half]
    ang = pos[:, axis] * freq[None, :]
    live = (lane < rot_dim)[None, :]
    first = ((lane % m) < half)[None, :]
    cos = np.where(live, np.cos(ang), 1.0)
    sin = np.where(live, np.sin(ang), 0.0)
    s_hi = np.where(first, -sin, 0.0)
    s_lo = np.where(first, 0.0, sin)
    ident = np.concatenate([np.ones((TILE, LANE)), np.zeros((TILE, LANE)), np.zeros((TILE, LANE))], 1)
    tab = np.concatenate([cos, s_hi, s_lo], axis=1)
    tab = np.concatenate([ident, tab], axis=0).astype(np.float32)
    return (jnp.asarray(tab[:, :LANE]), jnp.asarray(tab[:, LANE:2 * LANE]), jnp.asarray(tab[:, 2 * LANE:]))


def kernel(x, c, ctx, c_ctx, ada_w, ada_b, ln_g, ln_b, mlp_w1, mlp_w2, gqa_wq, gqa_wk, gqa_wv, gqa_wo, gqa_q_g, gqa_k_g, lru_wx, lru_wy, lru_conv_w, lru_conv_b, lru_ra_w, lru_ra_b, lru_ix_w, lru_ix_b, lru_lam, lru_wo, mla_wq_a, mla_q_a_g, mla_wq_b, mla_wkv_a, mla_kv_a_g, mla_wkv_b, mla_wo):
    batch, n_lat, d = x.shape
    depth = ada_w.shape[0]
    assert ctx.shape[1] == TILE and n_lat % KV_CHUNK == 0 and n_lat % GRID_W == 0
    t_len = TILE + n_lat
    n_tiles = t_len // TILE
    rows = batch * t_len
    n_groups = rows // TILE
    alpha = (2 * depth) ** 0.25


    cond_rows = -(-(batch + 1) // 8) * 8
    cond = jnp.concatenate([c, c_ctx[None, :], jnp.zeros((cond_rows - batch - 1, d), F32)], axis=0)
    mods = _ada_mods(cond, ada_w, ada_b)
    m_lat = jnp.broadcast_to(mods[:, :batch, None, :], (depth, batch, n_tiles - 1, 6 * d))
    m_ctx = jnp.broadcast_to(mods[:, batch:batch + 1, None, :], (depth, batch, 1, 6 * d))
    mods = jnp.concatenate([m_ctx, m_lat], axis=2).reshape(depth, n_groups, 6, d)

    def mod3(gate, scale, shift):
        return jnp.stack([gate, scale, shift], axis=1)

    tabs_gqa = _rope_tables(n_lat, A_HEAD_DIM)
    tabs_mla = _rope_tables(n_lat, MLA_ROPE)

    def tab_specs():
        return [pl.BlockSpec((TILE, LANE), lambda i: (i % n_tiles, 0)) for _ in range(3)]

    def vt_spec(n):
        return pl.BlockSpec((None, n, TILE), lambda i: (i // n_tiles, 0, i % n_tiles))

    def lat_tile(i):
        return i + i // (n_tiles - 1) + 1

    xs, h = _modulate(x, ctx, mods[0])
    w1_all = mlp_w1.astype(BF16)
    w2_all = mlp_w2.astype(BF16)

    for i in range(depth):
        kind = i % 3
        slot = i // 3
        m_i = mods[i]
        if kind == 0:
            wq, wk, wv, wo = (gqa_wq[slot].astype(BF16), gqa_wk[slot].astype(BF16),
                              gqa_wv[slot].astype(BF16), gqa_wo[slot].astype(BF16))

            nq, nkv = wq.shape[1], wk.shape[1]
            wqkv = jnp.concatenate([wq, wk, wv], axis=1)
            ex = [(gqa_q_g[slot].reshape(1, LANE), _const_spec((1, LANE))),
                  (gqa_k_g[slot].reshape(1, LANE), _const_spec((1, LANE)))] + list(zip(tabs_gqa, tab_specs()))
            outs = [(jax.ShapeDtypeStruct((rows, nq), BF16), _row_spec(TILE, nq)),
                    (jax.ShapeDtypeStruct((rows, nkv), BF16), _row_spec(TILE, nkv)),
                    (jax.ShapeDtypeStruct((batch, nkv, t_len), BF16), vt_spec(nkv))]
            q, k, vt = _proj(h, wqkv, ex, outs, tm=TILE, name="gqa_qkv",
                             epilogue=functools.partial(_ep_gqa_qkv, nq=nq // LANE, nkv=nkv // LANE,
                                                        off=A_HEAD_DIM // 4,
                                                        qscale=A_HEAD_DIM ** -0.5 * LOG2E))
            o = _attention(q, k, vt, batch=batch, n_tiles=n_tiles, n_steps=A_HEADS // GQA_STEP_HEADS,
                           group=GQA_STEP_HEADS, dq=A_HEAD_DIM, dv=A_HEAD_DIM, q_per_kv=A_GROUP)
            a_list, w_out, prologue = [o], wo, None
        elif kind == 1:
            wx, wy, wo = lru_wx[slot].astype(BF16), lru_wy[slot].astype(BF16), lru_wo[slot].astype(BF16)
            lw = wx.shape[1]
            xw, y = _proj(h, jnp.concatenate([wx, wy], axis=1), [],
                          [(jax.ShapeDtypeStruct((rows, lw), F32), _row_spec(2 * TILE, lw)),
                           (jax.ShapeDtypeStruct((rows, lw), BF16), _row_spec(2 * TILE, lw))],
                          tm=2 * TILE, epilogue=_ep_lru_xy, name="lru_xy")
            u = _dwconv(xw, lru_conv_w[slot], lru_conv_b[slot], n_tiles)
            hf, hb = _lru_scan(u, (0.5 * lru_ra_w[slot]).astype(BF16), (0.5 * lru_ix_w[slot]).astype(BF16),
                               0.5 * lru_ra_b[slot], 0.5 * lru_ix_b[slot], lru_lam[slot],
                               batch=batch, n_tiles=n_tiles)
            a_list, w_out, prologue = [hf, hb, y], wo, _pro_lru_gate
        else:
            nq = MLA_NOPE + MLA_ROPE
            wq_a = mla_wq_a[slot].astype(BF16)
            wq_b = mla_wq_b[slot].reshape(MLA_LORA, MLA_HEADS, nq)
            wq_b = jnp.concatenate([wq_b, jnp.zeros((MLA_LORA, MLA_HEADS, 2 * LANE - nq), F32)], axis=-1)
            wq_b = wq_b.reshape(MLA_LORA, MLA_HEADS * 2 * LANE).astype(BF16)
            wkv_a = jnp.concatenate([mla_wkv_a[slot], jnp.zeros((d, LANE - MLA_ROPE), F32)], axis=-1).astype(BF16)
            wkv_b = mla_wkv_b[slot].reshape(MLA_LORA, MLA_HEADS, MLA_NOPE + MLA_V)
            wkv_b = jnp.concatenate([wkv_b[:, :, :MLA_NOPE].reshape(MLA_LORA, -1),
                                     wkv_b[:, :, MLA_NOPE:].reshape(MLA_LORA, -1)], axis=-1).astype(BF16)
            wo = mla_wo[slot].astype(BF16)

            q, kfull, vt = _mla_proj(h, jnp.concatenate([wq_a, wkv_a], axis=1), wq_b, wkv_b,
                                     mla_q_a_g[slot], mla_kv_a_g[slot], tabs_mla,
                                     batch=batch, n_tiles=n_tiles, qscale=float(nq) ** -0.5 * LOG2E)
            o = _attention(q, kfull, vt, batch=batch, n_tiles=n_tiles, n_steps=MLA_HEADS // MLA_STEP_HEADS,
                           group=MLA_STEP_HEADS, dq=2 * LANE, dv=MLA_V, q_per_kv=1)
            a_list, w_out, prologue = [o], wo, None

        last = i == depth - 1
        xs, h2 = _mix_out(a_list, w_out, xs, mod3(m_i[:, 2], m_i[:, 4], m_i[:, 3]), ln_g[i, 0], ln_b[i, 0],
                          ngroups=1 if last else 2, out_rows=batch * n_lat if last else rows,
                          row_map=lat_tile if last else (lambda t: t), prologue=prologue, alpha=alpha)
        if not last:
            nxt = mod3(m_i[:, 5], mods[i + 1][:, 1], mods[i + 1][:, 0])
        else:
            zero = jnp.zeros_like(m_i[:, 5])
            nxt = mod3(m_i[:, 5], zero, zero)
            nxt = nxt.reshape(batch, n_tiles, 3, d)[:, 1:].reshape(batch * (n_tiles - 1), 3, d)
        xs, h = _mlp(h2, w1_all, w2_all, i, xs, nxt, ln_g[i, 1], ln_b[i, 1], tm=2 * TILE, tf=1024,
                     alpha=alpha)

    return xs.reshape(batch, n_lat, d)
```

```python
import functools
import math

import numpy as np
import jax
import jax.numpy as jnp
from jax import lax
from jax.experimental import pallas as pl
from jax.experimental.pallas import tpu as pltpu

F32 = jnp.float32
BF16 = jnp.bfloat16

TILE = 256
GRID_W = 64
ROPE_THETA = 10000.0
EPS = 1e-6
LANE = 128
VMEM_LIMIT_MB = 56

A_HEADS = 16
A_KV_HEADS = 4
A_GROUP = A_HEADS // A_KV_HEADS
A_HEAD_DIM = 128

LRU_BLOCKS = 8
CONV_W = 4
LRU_C = 8.0

MLA_HEADS = 16
MLA_LORA = 512
MLA_NOPE = 128
MLA_ROPE = 64
MLA_V = 128

KV_CHUNK = 512
GQA_STEP_HEADS = 8
MLA_STEP_HEADS = 8
LOG2E = math.log2(math.e)


def _cparams(sem, vmem_mb=VMEM_LIMIT_MB):
    return pltpu.CompilerParams(dimension_semantics=sem,
                                vmem_limit_bytes=vmem_mb * 1024 * 1024)


def _proj_body(*refs, n_ex, epilogue):
    a_ref, w_ref = refs[0], refs[1]
    ex = refs[2:2 + n_ex]
    outs = refs[2 + n_ex:]
    epilogue(jnp.dot(a_ref[...], w_ref[...], preferred_element_type=F32), ex, outs)


def _const_spec(shape):
    zeros = (0,) * len(shape)
    return pl.BlockSpec(shape, lambda i: zeros, pipeline_mode=pl.Buffered(1))


def _proj(a, w, extras, outs, *, tm, epilogue, name=None):
    m, kdim = a.shape
    assert m % tm == 0 and w.shape[0] == kdim
    return pl.pallas_call(
        functools.partial(_proj_body, n_ex=len(extras), epilogue=epilogue),
        grid=(m // tm,),
        in_specs=[pl.BlockSpec((tm, kdim), lambda i: (i, 0)), _const_spec(w.shape)] + [s for _, s in extras],
        out_specs=[s for _, s in outs],
        out_shape=[s for s, _ in outs],
        compiler_params=_cparams(("arbitrary",)),
        name=name,
    )(a, w, *[arr for arr, _ in extras])


def _row_spec(tm, width):
    return pl.BlockSpec((tm, width), lambda i: (i, 0))


def _mix_out_body(*refs, n_a, ngroups, prologue, alpha):
    a_refs = refs[:n_a]
    w_ref, xold_ref, mod_ref, lng_ref, lnb_ref, x_out, h_out = refs[n_a:]
    for g in range(ngroups):
        r0 = g * TILE
        vals = [r[r0:r0 + TILE, :] for r in a_refs]
        a = vals[0] if prologue is None else prologue(vals)
        part = jnp.dot(a, w_ref[...], preferred_element_type=F32)
        _resid_ln_rows(part, r0, xold_ref, mod_ref, lng_ref, lnb_ref, x_out, h_out, alpha)


def _mix_out(a_list, w, x_old, mod, lng, lnb, *, ngroups, out_rows, row_map, prologue, alpha):
    kdim, d = w.shape
    tm = ngroups * TILE
    src = lambda i: (row_map(i), 0)
    dst = pl.BlockSpec((tm, d), lambda i: (i, 0))
    return pl.pallas_call(
        functools.partial(_mix_out_body, n_a=len(a_list), ngroups=ngroups, prologue=prologue, alpha=alpha),
        grid=(out_rows // tm,),
        in_specs=[pl.BlockSpec((tm, kdim), src) for _ in a_list]
        + [_const_spec(w.shape), pl.BlockSpec((tm, d), src),
           pl.BlockSpec((ngroups, 3, d), lambda i: (row_map(i), 0, 0)),
           _const_spec((1, d)), _const_spec((1, d))],
        out_specs=[dst, dst],
        out_shape=[jax.ShapeDtypeStruct((out_rows, d), F32), jax.ShapeDtypeStruct((out_rows, d), BF16)],
        compiler_params=_cparams(("arbitrary",)),
        name="mix_out",
    )(*a_list, w, x_old, mod, lng.reshape(1, d), lnb.reshape(1, d))


def _rope(x, cos, s_hi, s_lo, off):
    return x * cos + pltpu.roll(x, LANE - off, 1) * s_hi + pltpu.roll(x, off, 1) * s_lo


def _ep_headnorm_rope(acc, ex, outs, *, nblk, off, scale):
    g_ref, cos_ref, shi_ref, slo_ref = ex
    g = g_ref[...]
    cos, shi, slo = cos_ref[...], shi_ref[...], slo_ref[...]
    for j in range(nblk):
        x = acc[:, j * LANE:(j + 1) * LANE]
        ms = jnp.mean(x * x, axis=-1, keepdims=True)
        xn = x * lax.rsqrt(ms + EPS) * g
        y = _rope(xn, cos, shi, slo, off)
        if scale != 1.0:
            y = y * scale
        outs[0][:, j * LANE:(j + 1) * LANE] = y.astype(outs[0].dtype)


def _ep_gqa_qkv(acc, ex, outs, *, nq, nkv, off, qscale):
    qg_ref, kg_ref, cos_ref, shi_ref, slo_ref = ex
    q_out, k_out, vt_out = outs
    tabs = (cos_ref, shi_ref, slo_ref)
    _ep_headnorm_rope(acc[:, :nq * LANE], (qg_ref,) + tabs, (q_out,), nblk=nq, off=off, scale=qscale)
    _ep_headnorm_rope(acc[:, nq * LANE:(nq + nkv) * LANE], (kg_ref,) + tabs, (k_out,), nblk=nkv, off=off,
                      scale=1.0)
    vt_out[...] = acc[:, (nq + nkv) * LANE:].T.astype(vt_out.dtype)


def _rms(x, g):
    ms = jnp.mean(x * x, axis=-1, keepdims=True)
    return x * lax.rsqrt(ms + EPS) * g


def _ep_mla_q(acc, ex, outs, *, nheads, off, scale):
    cos_ref, shi_ref, slo_ref = ex
    cos, shi, slo = cos_ref[...], shi_ref[...], slo_ref[...]
    for h in range(nheads):
        c0 = h * 2 * LANE
        outs[0][:, c0:c0 + LANE] = (acc[:, c0:c0 + LANE] * scale).astype(outs[0].dtype)
        pe = acc[:, c0 + LANE:c0 + 2 * LANE]
        outs[0][:, c0 + LANE:c0 + 2 * LANE] = (_rope(pe, cos, shi, slo, off) * scale).astype(outs[0].dtype)


def _ep_mla_kv(acc, ex, outs, *, nheads):
    kpe_ref, = ex
    k_out, vt_out = outs
    kpe = kpe_ref[...]
    for h in range(nheads):
        k_out[:, 2 * h * LANE:(2 * h + 1) * LANE] = acc[:, h * LANE:(h + 1) * LANE].astype(k_out.dtype)
        k_out[:, (2 * h + 1) * LANE:(2 * h + 2) * LANE] = kpe
    vt_out[...] = acc[:, nheads * LANE:2 * nheads * LANE].T.astype(vt_out.dtype)


def _resid_ln_rows(f_out, r0, xold_ref, mod_ref, lng_ref, lnb_ref, x_out, h_out, alpha):
    n = f_out.shape[0]
    g = r0 // TILE
    assert r0 % TILE + n <= TILE
    gate = mod_ref[g, 0:1, :]
    scale = mod_ref[g, 1:2, :]
    shift = mod_ref[g, 2:3, :]
    y = alpha * xold_ref[r0:r0 + n, :] + (1.0 + gate) * f_out
    mu = jnp.mean(y, axis=-1, keepdims=True)
    yc = y - mu
    var = jnp.mean(yc * yc, axis=-1, keepdims=True)
    xn = yc * lax.rsqrt(var + EPS) * lng_ref[...] + lnb_ref[...]
    x_out[r0:r0 + n, :] = xn
    h_out[r0:r0 + n, :] = (xn * (1.0 + scale) + shift).astype(h_out.dtype)


def _pro_lru_gate(vals):
    hf, hb, y = vals
    return ((hf.astype(F32) + hb.astype(F32)) * y.astype(F32)).astype(BF16)


def _ep_lru_xy(acc, ex, outs):
    w = outs[0].shape[1]
    outs[0][...] = acc[:, 0:w]
    outs[1][...] = jax.nn.gelu(acc[:, w:2 * w], approximate=True).astype(outs[1].dtype)


def _mlp_body(h_ref, w1_ref, w2_ref, xold_ref, mod_ref, lng_ref, lnb_ref, x_out, h_out, acc_ref,
              *, nf, ngroups, alpha):
    f = pl.program_id(1)

    @pl.when(f == 0)
    def _():
        acc_ref[...] = jnp.zeros_like(acc_ref)

    def hidden():
        u = jnp.maximum(jnp.dot(h_ref[...], w1_ref[...], preferred_element_type=F32), 0.0)
        return (u * u).astype(BF16)

    @pl.when(f < nf - 1)
    def _():
        acc_ref[...] += jnp.dot(hidden(), w2_ref[...], preferred_element_type=F32)

    @pl.when(f == nf - 1)
    def _():
        u2 = hidden()
        for g in range(ngroups):
            r0 = g * TILE
            tot = acc_ref[r0:r0 + TILE, :] + jnp.dot(u2[r0:r0 + TILE, :], w2_ref[...],
                                                   preferred_element_type=F32)
            _resid_ln_rows(tot, r0, xold_ref, mod_ref, lng_ref, lnb_ref, x_out, h_out, alpha)


def _mlp(h, w1, w2, layer, x_old, mod, lng, lnb, *, tm, tf, alpha):
    rows, d = h.shape
    ff = w1.shape[2]
    nf = ff // tf
    row = pl.BlockSpec((tm, d), lambda i, f: (i, 0))
    vec = pl.BlockSpec((1, d), lambda i, f: (0, 0))
    return pl.pallas_call(
        functools.partial(_mlp_body, nf=nf, ngroups=tm // TILE, alpha=alpha),
        grid=(rows // tm, nf),
        in_specs=[row,
                  pl.BlockSpec((None, d, tf), lambda i, f: (layer, 0, f)),
                  pl.BlockSpec((None, tf, d), lambda i, f: (layer, f, 0)),
                  row,
                  pl.BlockSpec((tm // TILE, 3, d), lambda i, f: (i, 0, 0)),
                  vec, vec],
        out_specs=[row, row],
        out_shape=[jax.ShapeDtypeStruct((rows, d), F32), jax.ShapeDtypeStruct((rows, d), BF16)],
        scratch_shapes=[pltpu.VMEM((tm, d), F32)],
        compiler_params=_cparams(("arbitrary", "arbitrary")),
        name="mlp",
    )(h, w1, w2, x_old, mod, lng.reshape(1, d), lnb.reshape(1, d))


def _mla_proj_body(h_ref, wa_ref, wqb_ref, wkvb_ref, qg_ref, kvg_ref, cos_ref, shi_ref, slo_ref,
                   q_out, k_out, vt_out, *, lora, nheads, off, qscale):
    tabs = (cos_ref, shi_ref, slo_ref)
    t = jnp.dot(h_ref[...], wa_ref[...], preferred_element_type=F32)
    qa = _rms(t[:, 0:lora], qg_ref[...]).astype(BF16)
    ckv = _rms(t[:, lora:2 * lora], kvg_ref[...]).astype(BF16)
    kpe = _rope(t[:, 2 * lora:2 * lora + LANE], cos_ref[...], shi_ref[...], slo_ref[...], off).astype(BF16)
    _ep_mla_q(jnp.dot(qa, wqb_ref[...], preferred_element_type=F32), tabs, (q_out,),
              nheads=nheads, off=off, scale=qscale)
    _ep_mla_kv(jnp.dot(ckv, wkvb_ref[...], preferred_element_type=F32), (kpe,), (k_out, vt_out),
               nheads=nheads)


def _mla_proj(h, w_a, wq_b, wkv_b, q_g, kv_g, tabs, *, batch, n_tiles, qscale):
    rows, d = h.shape
    t_len = n_tiles * TILE
    nq, nkv = wq_b.shape[1], wkv_b.shape[1]
    nv = MLA_HEADS * MLA_V

    def const(shape):
        return pl.BlockSpec(shape, lambda i: (0,) * len(shape), pipeline_mode=pl.Buffered(1))

    tab = pl.BlockSpec((TILE, LANE), lambda i: (i % n_tiles, 0))
    return pl.pallas_call(
        functools.partial(_mla_proj_body, lora=MLA_LORA, nheads=MLA_HEADS, off=MLA_ROPE // 4, qscale=qscale),
        grid=(rows // TILE,),
        in_specs=[pl.BlockSpec((TILE, d), lambda i: (i, 0)),
                  const(w_a.shape), const(wq_b.shape), const(wkv_b.shape),
                  const((1, MLA_LORA)), const((1, MLA_LORA)), tab, tab, tab],
        out_specs=[pl.BlockSpec((TILE, nq), lambda i: (i, 0)),
                   pl.BlockSpec((TILE, nq), lambda i: (i, 0)),
                   pl.BlockSpec((None, nv, TILE), lambda i: (i // n_tiles, 0, i % n_tiles))],
        out_shape=[jax.ShapeDtypeStruct((rows, nq), BF16), jax.ShapeDtypeStruct((rows, nq), BF16),
                   jax.ShapeDtypeStruct((batch, nv, t_len), BF16)],
        compiler_params=_cparams(("arbitrary",)),
        name="mla_proj",
    )(h, w_a, wq_b, wkv_b, q_g.reshape(1, MLA_LORA), kv_g.reshape(1, MLA_LORA), *tabs)


def _ada_body(c_ref, w_ref, b_ref, o_ref):
    cond = c_ref[...]
    sc = (cond * jax.nn.sigmoid(cond)).astype(BF16)
    o_ref[0] = jnp.dot(sc, w_ref[0].astype(BF16), preferred_element_type=F32) + b_ref[0]


def _ada_mods(cond, ada_w, ada_b, tn=1024):
    depth, d, n = ada_w.shape
    rows = cond.shape[0]
    return pl.pallas_call(
        _ada_body,
        grid=(depth, n // tn),
        in_specs=[pl.BlockSpec((rows, d), lambda l, j: (0, 0)),
                  pl.BlockSpec((1, d, tn), lambda l, j: (l, 0, j)),
                  pl.BlockSpec((1, 1, tn), lambda l, j: (l, 0, j))],
        out_specs=pl.BlockSpec((1, rows, tn), lambda l, j: (l, 0, j)),
        out_shape=jax.ShapeDtypeStruct((depth, rows, n), F32),
        compiler_params=_cparams(("arbitrary", "arbitrary")),
        name="ada_mods",
    )(cond, ada_w, ada_b.reshape(depth, 1, n))


def _modulate_body(ctx_ref, x_ref, mod_ref, xs_ref, h_ref):
    j = pl.program_id(1)

    def emit(src):
        xs_ref[...] = src
        h_ref[...] = (src * (1.0 + mod_ref[0, 1:2, :]) + mod_ref[0, 0:1, :]).astype(h_ref.dtype)

    @pl.when(j == 0)
    def _():
        emit(ctx_ref[...])

    @pl.when(j > 0)
    def _():
        emit(x_ref[...])


def _modulate(x, ctx, mod):
    batch, n_lat, d = x.shape
    n_tiles = n_lat // TILE + 1
    rows = batch * n_tiles * TILE
    row = pl.BlockSpec((TILE, d), lambda b, j: (b * n_tiles + j, 0))
    return pl.pallas_call(
        _modulate_body,
        grid=(batch, n_tiles),
        in_specs=[pl.BlockSpec((None, TILE, d), lambda b, j: (b, 0, 0)),
                  pl.BlockSpec((None, TILE, d), lambda b, j: (b, jnp.maximum(j - 1, 0), 0)),
                  pl.BlockSpec((1, mod.shape[1], d), lambda b, j: (b * n_tiles + j, 0, 0))],
        out_specs=[row, row],
        out_shape=[jax.ShapeDtypeStruct((rows, d), F32), jax.ShapeDtypeStruct((rows, d), BF16)],
        compiler_params=_cparams(("arbitrary", "arbitrary")),
        name="modulate",
    )(ctx, x, mod)


def _dot_nt(a, b):
    return lax.dot_general(a, b, (((1,), (1,)), ((), ())), preferred_element_type=F32)


def _attn_body(q_ref, k_ref, vt_ref, o_ref, s_ref, p_ref, *, group, dq, dv, q_per_kv, n_lat_chunks):
    i = pl.program_id(2)

    def kcol(g):
        return (g // q_per_kv) * dq

    def vrow(g):
        return (g // q_per_kv) * dv

    def finish(g, acc, l8):
        l = jnp.sum(l8, axis=0, keepdims=True)
        out = acc * (1.0 / l)
        o_ref[:, g * dv:(g + 1) * dv] = out.T.astype(o_ref.dtype)

    def colmax8(x):
        return jnp.max(x.reshape(x.shape[0] // 8, 8, x.shape[1]), axis=0)

    def colsum8(x):
        return jnp.sum(x.reshape(x.shape[0] // 8, 8, x.shape[1]), axis=0)

    @pl.when(i == 0)
    def _():
        for g in range(group):
            st = _dot_nt(k_ref[0:TILE, kcol(g):kcol(g) + dq], q_ref[:, g * dq:(g + 1) * dq])
            m = jnp.max(st, axis=0, keepdims=True)
            p = jnp.exp2(st - m)
            acc = jnp.dot(vt_ref[vrow(g):vrow(g) + dv, 0:TILE], p.astype(BF16), preferred_element_type=F32)
            finish(g, acc, colsum8(p))

    @pl.when(i > 0)
    def _():
        neg = jnp.full((8, TILE), -jnp.inf, F32)
        zero8 = jnp.zeros((8, TILE), F32)
        zacc = jnp.zeros((dv, TILE), F32)
        state = {}
        for ph in range(group + 2):
            ga, gb, gc = ph, ph - 1, ph - 2
            do_a, do_b, do_c = ga < group, 0 <= gb < group, 0 <= gc < group
            q_a = q_ref[:, ga * dq:(ga + 1) * dq] if do_a else None
            m_b = jnp.max(state["m8"], axis=0, keepdims=True) if do_b else None

            def stage(start, size, carry, q_a=q_a, m_b=m_b, ga=ga, gb=gb, gc=gc,
                      do_a=do_a, do_b=do_b, do_c=do_c):
                m8, l8, acc = carry

                def value_product(acc):
                    return acc + jnp.dot(vt_ref[vrow(gc):vrow(gc) + dv, pl.ds(start, size)],
                                         p_ref[gc % 2, pl.ds(start, size), :], preferred_element_type=F32)

                if do_c and q_per_kv > 1:
                    acc = value_product(acc)
                if do_a:
                    st = _dot_nt(k_ref[pl.ds(start, size), kcol(ga):kcol(ga) + dq], q_a)
                    s_ref[ga % 2, pl.ds(start, size), :] = st
                    m8 = jnp.maximum(m8, colmax8(st))
                if do_c and q_per_kv == 1:
                    acc = value_product(acc)
                if do_b:
                    p = jnp.exp2(s_ref[gb % 2, pl.ds(start, size), :] - m_b)
                    p_ref[gb % 2, pl.ds(start, size), :] = p.astype(BF16)
                    l8 = l8 + colsum8(p)
                return m8, l8, acc

            carry = stage(0, TILE, (neg, zero8, zacc))

            def body(c, carry, stage=stage):
                return stage(pl.multiple_of(TILE + c * KV_CHUNK, TILE), KV_CHUNK, carry)

            m8, l8, acc = lax.fori_loop(0, n_lat_chunks, body, carry, unroll=True)
            if do_c:
                finish(gc, acc, state["l8"])
            if do_b:
                state["l8"] = l8
            if do_a:
                state["m8"] = m8


def _attention(q, k, vt, *, batch, n_tiles, n_steps, group, dq, dv, q_per_kv):
    r = q.shape[0]
    t_len = n_tiles * TILE
    kvw = group // q_per_kv
    return pl.pallas_call(
        functools.partial(_attn_body, group=group, dq=dq, dv=dv, q_per_kv=q_per_kv,
                          n_lat_chunks=(t_len - TILE) // KV_CHUNK),
        grid=(batch, n_steps, n_tiles),
        in_specs=[pl.BlockSpec((TILE, group * dq), lambda b, h, i: (b * n_tiles + i, h)),
                  pl.BlockSpec((t_len, kvw * dq), lambda b, h, i: (b, h), pipeline_mode=pl.Buffered(1)),
                  pl.BlockSpec((None, kvw * dv, t_len), lambda b, h, i: (b, h, 0), pipeline_mode=pl.Buffered(1))],
        out_specs=pl.BlockSpec((TILE, group * dv), lambda b, h, i: (b * n_tiles + i, h)),
        out_shape=jax.ShapeDtypeStruct((r, n_steps * group * dv), BF16),
        scratch_shapes=[pltpu.VMEM((2, t_len, TILE), F32), pltpu.VMEM((2, t_len, TILE), BF16)],
        compiler_params=_cparams(("arbitrary", "arbitrary", "arbitrary")),
        name="attention",
    )(q, k, vt)


def _conv_body(x_ref, prev_ref, next_ref, w_ref, b_ref, u_ref, *, n_tiles):
    i = pl.program_id(0)
    seg = i % n_tiles
    has_prev = seg > 1
    has_next = jnp.logical_and(seg > 0, seg < n_tiles - 1)
    ext = jnp.concatenate([jnp.where(has_prev, prev_ref[...], 0.0), x_ref[...],
                           jnp.where(has_next, next_ref[...], 0.0)], axis=0)
    n = ext.shape[0]
    left = CONV_W // 2
    acc = b_ref[...]
    for j in range(CONV_W):
        shift = left - j
        rows = ext if shift == 0 else pltpu.roll(ext, shift % n, 0)
        acc = acc + rows[8:8 + TILE, :] * w_ref[j:j + 1, :]
    u_ref[...] = acc


def _dwconv(xw, conv_w, conv_b, n_tiles):
    r, w = xw.shape
    sub = TILE // 8
    n_sub = r // 8
    return pl.pallas_call(
        functools.partial(_conv_body, n_tiles=n_tiles),
        grid=(r // TILE,),
        in_specs=[pl.BlockSpec((TILE, w), lambda i: (i, 0)),
                  pl.BlockSpec((8, w), lambda i: (jnp.maximum(i * sub - 1, 0), 0)),
                  pl.BlockSpec((8, w), lambda i: (jnp.minimum((i + 1) * sub, n_sub - 1), 0)),
                  pl.BlockSpec((CONV_W, w), lambda i: (0, 0)),
                  pl.BlockSpec((1, w), lambda i: (0, 0))],
        out_specs=pl.BlockSpec((TILE, w), lambda i: (i, 0)),
        out_shape=jax.ShapeDtypeStruct((r, w), F32),
        compiler_params=_cparams(("arbitrary",)),
        name="dwconv",
    )(xw, xw, xw, conv_w, conv_b.reshape(1, w))


def _log_sigmoid(x):
    return jnp.minimum(x, 0.0) - jnp.log1p(jnp.exp(-jnp.abs(x)))


def _scan_tile(a, b, carry, reverse):
    n, w = a.shape
    groups = n // 8
    a3 = a.reshape(groups, 8, w)
    b3 = b.reshape(groups, 8, w)
    sub = lax.broadcasted_iota(jnp.int32, (groups, 8, w), 1)
    for s in (1, 2, 4):
        shift = 8 - s if reverse else s
        valid = (sub < 8 - s) if reverse else (sub >= s)
        a_sh = pltpu.roll(a3, shift, 1)
        b_sh = pltpu.roll(b3, shift, 1)
        b3 = jnp.where(valid, b3 + a3 * b_sh, b3)
        a3 = jnp.where(valid, a3 * a_sh, a3)
    outs = [None] * groups
    for r in (range(groups - 1, -1, -1) if reverse else range(groups)):
        hv = b3[r] + a3[r] * carry
        outs[r] = hv
        edge = hv[0:1, :] if reverse else hv[7:8, :]
        carry = jnp.broadcast_to(edge, (8, w))
    return jnp.concatenate(outs, axis=0), carry


def _lru_body(uf_ref, ub_ref, raw_ref, ixw_ref, rab_ref, ixb_ref, lam_ref,
              hf_ref, hb_ref, carry_ref, *, blk):
    j = pl.program_id(1)

    @pl.when(j == 0)
    def _():
        carry_ref[...] = jnp.zeros_like(carry_ref)

    for d, (u_ref, h_ref) in enumerate(((uf_ref, hf_ref), (ub_ref, hb_ref))):
        half_l2 = (0.5 * LRU_C * LOG2E) * _log_sigmoid(lam_ref[d])
        for n in range(LRU_BLOCKS):
            cs = slice(n * blk, (n + 1) * blk)
            u = u_ref[:, cs]
            ub16 = u.astype(BF16)
            t_r = jnp.tanh(jnp.dot(ub16, raw_ref[d, n], preferred_element_type=F32) + rab_ref[d][:, cs])
            t_g = jnp.tanh(jnp.dot(ub16, ixw_ref[d, n], preferred_element_type=F32) + ixb_ref[d][:, cs])
            a = jnp.exp2((t_r + 1.0) * half_l2[:, cs])
            om = 1.0 - a * a
            root = jnp.where(om > 0.0, om * lax.rsqrt(om), 0.0)
            bx = root * ((t_g + 1.0) * (0.5 * u))
            h, carry = _scan_tile(a, bx, carry_ref[d, :, cs], reverse=(d == 1))
            h_ref[:, cs] = h.astype(h_ref.dtype)
            carry_ref[d, :, cs] = carry


def _lru_scan(u, ra_w, ix_w, ra_b, ix_b, lam, *, batch, n_tiles):
    r, w = u.shape
    blk = w // LRU_BLOCKS

    def fwd_map(b, j):
        return (b * n_tiles + j, 0)

    def bwd_map(b, j):
        return (b * n_tiles + jnp.where(j == 0, 0, n_tiles - j), 0)

    full5 = pl.BlockSpec((2, LRU_BLOCKS, blk, blk), lambda b, j: (0, 0, 0, 0))
    vec = pl.BlockSpec((2, 1, w), lambda b, j: (0, 0, 0))
    return pl.pallas_call(
        functools.partial(_lru_body, blk=blk),
        grid=(batch, n_tiles),
        in_specs=[pl.BlockSpec((TILE, w), fwd_map), pl.BlockSpec((TILE, w), bwd_map),
                  full5, full5, vec, vec, vec],
        out_specs=[pl.BlockSpec((TILE, w), fwd_map), pl.BlockSpec((TILE, w), bwd_map)],
        out_shape=[jax.ShapeDtypeStruct((r, w), BF16), jax.ShapeDtypeStruct((r, w), BF16)],
        scratch_shapes=[pltpu.VMEM((2, 8, w), F32)],
        compiler_params=_cparams(("arbitrary", "arbitrary")),
        name="lru_scan",
    )(u, u, ra_w, ix_w, ra_b.reshape(2, 1, w), ix_b.reshape(2, 1, w), lam.reshape(2, 1, w))


def _rope_tables(n_lat, rot_dim):
    m = rot_dim // 2
    half = m // 2
    t = np.arange(n_lat)
    pos = np.stack([t // GRID_W, t % GRID_W], axis=1).astype(np.float32)
    inv = (ROPE_THETA ** (-(np.arange(half, dtype=np.float32) * 2.0) / m)).astype(np.float32)
    lane = np.arange(LANE)
    axis = np.minimum(lane // m, 1)
    freq = inv[lane % half]
    ang = pos[:, axis] * freq[None, :]
    live = (lane < rot_dim)[None, :]
    first = ((lane % m) < half)[None, :]
    cos = np.where(live, np.cos(ang), 1.0)
    sin = np.where(live, np.sin(ang), 0.0)
    s_hi = np.where(first, -sin, 0.0)
    s_lo = np.where(first, 0.0, sin)
    ident = np.concatenate([np.ones((TILE, LANE)), np.zeros((TILE, LANE)), np.zeros((TILE, LANE))], 1)
    tab = np.concatenate([cos, s_hi, s_lo], axis=1)
    tab = np.concatenate([ident, tab], axis=0).astype(np.float32)
    return (jnp.asarray(tab[:, :LANE]), jnp.asarray(tab[:, LANE:2 * LANE]), jnp.asarray(tab[:, 2 * LANE:]))


def kernel(x, c, ctx, c_ctx, ada_w, ada_b, ln_g, ln_b, mlp_w1, mlp_w2, gqa_wq, gqa_wk, gqa_wv, gqa_wo, gqa_q_g, gqa_k_g, lru_wx, lru_wy, lru_conv_w, lru_conv_b, lru_ra_w, lru_ra_b, lru_ix_w, lru_ix_b, lru_lam, lru_wo, mla_wq_a, mla_q_a_g, mla_wq_b, mla_wkv_a, mla_kv_a_g, mla_wkv_b, mla_wo):
    batch, n_lat, d = x.shape
    depth = ada_w.shape[0]
    assert ctx.shape[1] == TILE and n_lat % KV_CHUNK == 0 and n_lat % GRID_W == 0
    t_len = TILE + n_lat
    n_tiles = t_len // TILE
    rows = batch * t_len
    n_groups = rows // TILE
    alpha = (2 * depth) ** 0.25


    cond_rows = -(-(batch + 1) // 8) * 8
    cond = jnp.concatenate([c, c_ctx[None, :], jnp.zeros((cond_rows - batch - 1, d), F32)], axis=0)
    mods = _ada_mods(cond, ada_w, ada_b)
    m_lat = jnp.broadcast_to(mods[:, :batch, None, :], (depth, batch, n_tiles - 1, 6 * d))
    m_ctx = jnp.broadcast_to(mods[:, batch:batch + 1, None, :], (depth, batch, 1, 6 * d))
    mods = jnp.concatenate([m_ctx, m_lat], axis=2).reshape(depth, n_groups, 6, d)

    def mod3(gate, scale, shift):
        return jnp.stack([gate, scale, shift], axis=1)

    tabs_gqa = _rope_tables(n_lat, A_HEAD_DIM)
    tabs_mla = _rope_tables(n_lat, MLA_ROPE)

    def tab_specs():
        return [pl.BlockSpec((TILE, LANE), lambda i: (i % n_tiles, 0)) for _ in range(3)]

    def vt_spec(n):
        return pl.BlockSpec((None, n, TILE), lambda i: (i // n_tiles, 0, i % n_tiles))

    def lat_tile(i):
        return i + i // (n_tiles - 1) + 1

    xs, h = _modulate(x, ctx, mods[0])
    w1_all = mlp_w1.astype(BF16)
    w2_all = mlp_w2.astype(BF16)

    for i in range(depth):
        kind = i % 3
        slot = i // 3
        m_i = mods[i]
        if kind == 0:
            wq, wk, wv, wo = (gqa_wq[slot].astype(BF16), gqa_wk[slot].astype(BF16),
                              gqa_wv[slot].astype(BF16), gqa_wo[slot].astype(BF16))

            nq, nkv = wq.shape[1], wk.shape[1]
            wqkv = jnp.concatenate([wq, wk, wv], axis=1)
            ex = [(gqa_q_g[slot].reshape(1, LANE), _const_spec((1, LANE))),
                  (gqa_k_g[slot].reshape(1, LANE), _const_spec((1, LANE)))] + list(zip(tabs_gqa, tab_specs()))
            outs = [(jax.ShapeDtypeStruct((rows, nq), BF16), _row_spec(TILE, nq)),
                    (jax.ShapeDtypeStruct((rows, nkv), BF16), _row_spec(TILE, nkv)),
                    (jax.ShapeDtypeStruct((batch, nkv, t_len), BF16), vt_spec(nkv))]
            q, k, vt = _proj(h, wqkv, ex, outs, tm=TILE, name="gqa_qkv",
                             epilogue=functools.partial(_ep_gqa_qkv, nq=nq // LANE, nkv=nkv // LANE,
                                                        off=A_HEAD_DIM // 4,
                                                        qscale=A_HEAD_DIM ** -0.5 * LOG2E))
            o = _attention(q, k, vt, batch=batch, n_tiles=n_tiles, n_steps=A_HEADS // GQA_STEP_HEADS,
                           group=GQA_STEP_HEADS, dq=A_HEAD_DIM, dv=A_HEAD_DIM, q_per_kv=A_GROUP)
            a_list, w_out, prologue = [o], wo, None
        elif kind == 1:
            wx, wy, wo = lru_wx[slot].astype(BF16), lru_wy[slot].astype(BF16), lru_wo[slot].astype(BF16)
            lw = wx.shape[1]
            xw, y = _proj(h, jnp.concatenate([wx, wy], axis=1), [],
                          [(jax.ShapeDtypeStruct((rows, lw), F32), _row_spec(2 * TILE, lw)),
                           (jax.ShapeDtypeStruct((rows, lw), BF16), _row_spec(2 * TILE, lw))],
                          tm=2 * TILE, epilogue=_ep_lru_xy, name="lru_xy")
            u = _dwconv(xw, lru_conv_w[slot], lru_conv_b[slot], n_tiles)
            hf, hb = _lru_scan(u, (0.5 * lru_ra_w[slot]).astype(BF16), (0.5 * lru_ix_w[slot]).astype(BF16),
                               0.5 * lru_ra_b[slot], 0.5 * lru_ix_b[slot], lru_lam[slot],
                               batch=batch, n_tiles=n_tiles)
            a_list, w_out, prologue = [hf, hb, y], wo, _pro_lru_gate
        else:
            nq = MLA_NOPE + MLA_ROPE
            wq_a = mla_wq_a[slot].astype(BF16)
            wq_b = mla_wq_b[slot].reshape(MLA_LORA, MLA_HEADS, nq)
            wq_b = jnp.concatenate([wq_b, jnp.zeros((MLA_LORA, MLA_HEADS, 2 * LANE - nq), F32)], axis=-1)
            wq_b = wq_b.reshape(MLA_LORA, MLA_HEADS * 2 * LANE).astype(BF16)
            wkv_a = jnp.concatenate([mla_wkv_a[slot], jnp.zeros((d, LANE - MLA_ROPE), F32)], axis=-1).astype(BF16)
            wkv_b = mla_wkv_b[slot].reshape(MLA_LORA, MLA_HEADS, MLA_NOPE + MLA_V)
            wkv_b = jnp.concatenate([wkv_b[:, :, :MLA_NOPE].reshape(MLA_LORA, -1),
                                     wkv_b[:, :, MLA_NOPE:].reshape(MLA_LORA, -1)], axis=-1).astype(BF16)
            wo = mla_wo[slot].astype(BF16)

            q, kfull, vt = _mla_proj(h, jnp.concatenate([wq_a, wkv_a], axis=1), wq_b, wkv_b,
                                     mla_q_a_g[slot], mla_kv_a_g[slot], tabs_mla,
                                     batch=batch, n_tiles=n_tiles, qscale=float(nq) ** -0.5 * LOG2E)
            o = _attention(q, kfull, vt, batch=batch, n_tiles=n_tiles, n_steps=MLA_HEADS // MLA_STEP_HEADS,
                           group=MLA_STEP_HEADS, dq=2 * LANE, dv=MLA_V, q_per_kv=1)
            a_list, w_out, prologue = [o], wo, None

        last = i == depth - 1
        xs, h2 = _mix_out(a_list, w_out, xs, mod3(m_i[:, 2], m_i[:, 4], m_i[:, 3]), ln_g[i, 0], ln_b[i, 0],
                          ngroups=1 if last else 2, out_rows=batch * n_lat if last else rows,
                          row_map=lat_tile if last else (lambda t: t), prologue=prologue, alpha=alpha)
        if not last:
            nxt = mod3(m_i[:, 5], mods[i + 1][:, 1], mods[i + 1][:, 0])
        else:
            zero = jnp.zeros_like(m_i[:, 5])
            nxt = mod3(m_i[:, 5], zero, zero)
            nxt = nxt.reshape(batch, n_tiles, 3, d)[:, 1:].reshape(batch * (n_tiles - 1), 3, d)
        xs, h = _mlp(h2, w1_all, w2_all, i, xs, nxt, ln_g[i, 1], ln_b[i, 1], tm=2 * TILE, tf=1024,
                     alpha=alpha)

    return xs.reshape(batch, n_lat, d)
```

```python
import functools
import math

import numpy as np
import jax
import jax.numpy as jnp
from jax import lax
from jax.experimental import pallas as pl
from jax.experimental.pallas import tpu as pltpu

F32 = jnp.float32
BF16 = jnp.bfloat16

TILE = 256
GRID_W = 64
ROPE_THETA = 10000.0
EPS = 1e-6
LANE = 128
VMEM_LIMIT_MB = 56

A_HEADS = 16
A_KV_HEADS = 4
A_GROUP = A_HEADS // A_KV_HEADS
A_HEAD_DIM = 128

LRU_BLOCKS = 8
CONV_W = 4
LRU_C = 8.0
HALO = 16

MLA_HEADS = 16
MLA_LORA = 512
MLA_NOPE = 128
MLA_ROPE = 64
MLA_V = 128

KV_CHUNK = 512
GQA_STEP_HEADS = 8
MLA_STEP_HEADS = 8
LOG2E = math.log2(math.e)


def _cparams(sem, vmem_mb=VMEM_LIMIT_MB):
    return pltpu.CompilerParams(dimension_semantics=sem,
                                vmem_limit_bytes=vmem_mb * 1024 * 1024)


def _proj_body(*refs, n_ex, epilogue):
    a_ref, w_ref = refs[0], refs[1]
    ex = refs[2:2 + n_ex]
    outs = refs[2 + n_ex:]
    epilogue(jnp.dot(a_ref[...], w_ref[...], preferred_element_type=F32), ex, outs)


def _const_spec(shape):
    zeros = (0,) * len(shape)
    return pl.BlockSpec(shape, lambda i: zeros, pipeline_mode=pl.Buffered(1))


def _proj(a, w, extras, outs, *, tm, epilogue, name=None):
    m, kdim = a.shape
    assert m % tm == 0 and w.shape[0] == kdim
    return pl.pallas_call(
        functools.partial(_proj_body, n_ex=len(extras), epilogue=epilogue),
        grid=(m // tm,),
        in_specs=[pl.BlockSpec((tm, kdim), lambda i: (i, 0)), _const_spec(w.shape)] + [s for _, s in extras],
        out_specs=[s for _, s in outs],
        out_shape=[s for s, _ in outs],
        compiler_params=_cparams(("arbitrary",)),
        name=name,
    )(a, w, *[arr for arr, _ in extras])


def _row_spec(tm, width):
    return pl.BlockSpec((tm, width), lambda i: (i, 0))


def _mix_out_body(*refs, n_a, ngroups, prologue, alpha):
    a_refs = refs[:n_a]
    w_ref, xold_ref, mod_ref, lng_ref, lnb_ref, x_out, h_out = refs[n_a:]
    for g in range(ngroups):
        r0 = g * TILE
        vals = [r[r0:r0 + TILE, :] for r in a_refs]
        a = vals[0] if prologue is None else prologue(vals)
        part = jnp.dot(a, w_ref[...], preferred_element_type=F32)
        _resid_ln_rows(part, r0, xold_ref, mod_ref, lng_ref, lnb_ref, x_out, h_out, alpha)


def _mix_out(a_list, w, x_old, mod, lng, lnb, *, ngroups, out_rows, row_map, prologue, alpha):
    kdim, d = w.shape
    tm = ngroups * TILE
    src = lambda i: (row_map(i), 0)
    dst = pl.BlockSpec((tm, d), lambda i: (i, 0))
    return pl.pallas_call(
        functools.partial(_mix_out_body, n_a=len(a_list), ngroups=ngroups, prologue=prologue, alpha=alpha),
        grid=(out_rows // tm,),
        in_specs=[pl.BlockSpec((tm, kdim), src) for _ in a_list]
        + [_const_spec(w.shape), pl.BlockSpec((tm, d), src),
           pl.BlockSpec((ngroups, 3, d), lambda i: (row_map(i), 0, 0)),
           _const_spec((1, d)), _const_spec((1, d))],
        out_specs=[dst, dst],
        out_shape=[jax.ShapeDtypeStruct((out_rows, d), F32), jax.ShapeDtypeStruct((out_rows, d), BF16)],
        compiler_params=_cparams(("arbitrary",)),
        name="mix_out",
    )(*a_list, w, x_old, mod, lng.reshape(1, d), lnb.reshape(1, d))


def _rope(x, cos, s_hi, s_lo, off):
    return x * cos + pltpu.roll(x, LANE - off, 1) * s_hi + pltpu.roll(x, off, 1) * s_lo


def _ep_headnorm_rope(acc, ex, outs, *, nblk, off, scale):
    g_ref, cos_ref, shi_ref, slo_ref = ex
    g = g_ref[...]
    cos, shi, slo = cos_ref[...], shi_ref[...], slo_ref[...]
    for j in range(nblk):
        x = acc[:, j * LANE:(j + 1) * LANE]
        ms = jnp.mean(x * x, axis=-1, keepdims=True)
        xn = x * lax.rsqrt(ms + EPS) * g
        y = _rope(xn, cos, shi, slo, off)
        if scale != 1.0:
            y = y * scale
        outs[0][:, j * LANE:(j + 1) * LANE] = y.astype(outs[0].dtype)


def _ep_gqa_qkv(acc, ex, outs, *, nq, nkv, off, qscale):
    qg_ref, kg_ref, cos_ref, shi_ref, slo_ref = ex
    q_out, k_out, vt_out = outs
    tabs = (cos_ref, shi_ref, slo_ref)
    _ep_headnorm_rope(acc[:, :nq * LANE], (qg_ref,) + tabs, (q_out,), nblk=nq, off=off, scale=qscale)
    _ep_headnorm_rope(acc[:, nq * LANE:(nq + nkv) * LANE], (kg_ref,) + tabs, (k_out,), nblk=nkv, off=off,
                      scale=1.0)
    vt_out[...] = acc[:, (nq + nkv) * LANE:].T.astype(vt_out.dtype)


def _rms(x, g):
    ms = jnp.mean(x * x, axis=-1, keepdims=True)
    return x * lax.rsqrt(ms + EPS) * g


def _ep_mla_q(acc, ex, outs, *, nheads, off, scale):
    cos_ref, shi_ref, slo_ref = ex
    cos, shi, slo = cos_ref[...], shi_ref[...], slo_ref[...]
    for h in range(nheads):
        c0 = h * 2 * LANE
        outs[0][:, c0:c0 + LANE] = (acc[:, c0:c0 + LANE] * scale).astype(outs[0].dtype)
        pe = acc[:, c0 + LANE:c0 + 2 * LANE]
        outs[0][:, c0 + LANE:c0 + 2 * LANE] = (_rope(pe, cos, shi, slo, off) * scale).astype(outs[0].dtype)


def _ep_mla_kv(acc, ex, outs, *, nheads):
    kpe_ref, = ex
    k_out, vt_out = outs
    kpe = kpe_ref[...]
    for h in range(nheads):
        k_out[:, 2 * h * LANE:(2 * h + 1) * LANE] = acc[:, h * LANE:(h + 1) * LANE].astype(k_out.dtype)
        k_out[:, (2 * h + 1) * LANE:(2 * h + 2) * LANE] = kpe
    vt_out[...] = acc[:, nheads * LANE:2 * nheads * LANE].T.astype(vt_out.dtype)


def _resid_ln_rows(f_out, r0, xold_ref, mod_ref, lng_ref, lnb_ref, x_out, h_out, alpha):
    n = f_out.shape[0]
    g = r0 // TILE
    assert r0 % TILE + n <= TILE
    gate = mod_ref[g, 0:1, :]
    scale = mod_ref[g, 1:2, :]
    shift = mod_ref[g, 2:3, :]
    y = alpha * xold_ref[r0:r0 + n, :] + (1.0 + gate) * f_out
    mu = jnp.mean(y, axis=-1, keepdims=True)
    yc = y - mu
    var = jnp.mean(yc * yc, axis=-1, keepdims=True)
    xn = yc * lax.rsqrt(var + EPS) * lng_ref[...] + lnb_ref[...]
    x_out[r0:r0 + n, :] = xn
    h_out[r0:r0 + n, :] = (xn * (1.0 + scale) + shift).astype(h_out.dtype)


def _pro_lru_gate(vals):
    hf, hb, y = vals
    return ((hf.astype(F32) + hb.astype(F32)) * y.astype(F32)).astype(BF16)


def _mlp_body(h_ref, w1_ref, w2_ref, xold_ref, mod_ref, lng_ref, lnb_ref, x_out, h_out, acc_ref,
              *, nf, ngroups, alpha):
    f = pl.program_id(1)

    @pl.when(f == 0)
    def _():
        acc_ref[...] = jnp.zeros_like(acc_ref)

    def hidden():
        u = jnp.maximum(jnp.dot(h_ref[...], w1_ref[...], preferred_element_type=F32), 0.0)
        return (u * u).astype(BF16)

    @pl.when(f < nf - 1)
    def _():
        acc_ref[...] += jnp.dot(hidden(), w2_ref[...], preferred_element_type=F32)

    @pl.when(f == nf - 1)
    def _():
        u2 = hidden()
        for g in range(ngroups):
            r0 = g * TILE
            tot = acc_ref[r0:r0 + TILE, :] + jnp.dot(u2[r0:r0 + TILE, :], w2_ref[...],
                                                   preferred_element_type=F32)
            _resid_ln_rows(tot, r0, xold_ref, mod_ref, lng_ref, lnb_ref, x_out, h_out, alpha)


def _mlp(h, w1, w2, layer, x_old, mod, lng, lnb, *, tm, tf, alpha):
    rows, d = h.shape
    ff = w1.shape[2]
    nf = ff // tf
    row = pl.BlockSpec((tm, d), lambda i, f: (i, 0))
    vec = pl.BlockSpec((1, d), lambda i, f: (0, 0))
    return pl.pallas_call(
        functools.partial(_mlp_body, nf=nf, ngroups=tm // TILE, alpha=alpha),
        grid=(rows // tm, nf),
        in_specs=[row,
                  pl.BlockSpec((None, d, tf), lambda i, f: (layer, 0, f)),
                  pl.BlockSpec((None, tf, d), lambda i, f: (layer, f, 0)),
                  row,
                  pl.BlockSpec((tm // TILE, 3, d), lambda i, f: (i, 0, 0)),
                  vec, vec],
        out_specs=[row, row],
        out_shape=[jax.ShapeDtypeStruct((rows, d), F32), jax.ShapeDtypeStruct((rows, d), BF16)],
        scratch_shapes=[pltpu.VMEM((tm, d), F32)],
        compiler_params=_cparams(("arbitrary", "arbitrary")),
        name="mlp",
    )(h, w1, w2, x_old, mod, lng.reshape(1, d), lnb.reshape(1, d))


def _mla_proj_body(h_ref, wa_ref, wqb_ref, wkvb_ref, qg_ref, kvg_ref, cos_ref, shi_ref, slo_ref,
                   q_out, k_out, vt_out, *, lora, nheads, off, qscale):
    tabs = (cos_ref, shi_ref, slo_ref)
    t = jnp.dot(h_ref[...], wa_ref[...], preferred_element_type=F32)
    qa = _rms(t[:, 0:lora], qg_ref[...]).astype(BF16)
    ckv = _rms(t[:, lora:2 * lora], kvg_ref[...]).astype(BF16)
    kpe = _rope(t[:, 2 * lora:2 * lora + LANE], cos_ref[...], shi_ref[...], slo_ref[...], off).astype(BF16)
    _ep_mla_q(jnp.dot(qa, wqb_ref[...], preferred_element_type=F32), tabs, (q_out,),
              nheads=nheads, off=off, scale=qscale)
    _ep_mla_kv(jnp.dot(ckv, wkvb_ref[...], preferred_element_type=F32), (kpe,), (k_out, vt_out),
               nheads=nheads)


def _mla_proj(h, w_a, wq_b, wkv_b, q_g, kv_g, tabs, *, batch, n_tiles, qscale):
    rows, d = h.shape
    t_len = n_tiles * TILE
    nq, nkv = wq_b.shape[1], wkv_b.shape[1]
    nv = MLA_HEADS * MLA_V

    def const(shape):
        return pl.BlockSpec(shape, lambda i: (0,) * len(shape), pipeline_mode=pl.Buffered(1))

    tab = pl.BlockSpec((TILE, LANE), lambda i: (i % n_tiles, 0))
    return pl.pallas_call(
        functools.partial(_mla_proj_body, lora=MLA_LORA, nheads=MLA_HEADS, off=MLA_ROPE // 4, qscale=qscale),
        grid=(rows // TILE,),
        in_specs=[pl.BlockSpec((TILE, d), lambda i: (i, 0)),
                  const(w_a.shape), const(wq_b.shape), const(wkv_b.shape),
                  const((1, MLA_LORA)), const((1, MLA_LORA)), tab, tab, tab],
        out_specs=[pl.BlockSpec((TILE, nq), lambda i: (i, 0)),
                   pl.BlockSpec((TILE, nq), lambda i: (i, 0)),
                   pl.BlockSpec((None, nv, TILE), lambda i: (i // n_tiles, 0, i % n_tiles))],
        out_shape=[jax.ShapeDtypeStruct((rows, nq), BF16), jax.ShapeDtypeStruct((rows, nq), BF16),
                   jax.ShapeDtypeStruct((batch, nv, t_len), BF16)],
        compiler_params=_cparams(("arbitrary",)),
        name="mla_proj",
    )(h, w_a, wq_b, wkv_b, q_g.reshape(1, MLA_LORA), kv_g.reshape(1, MLA_LORA), *tabs)


def _ada_body(c_ref, w_ref, b_ref, o_ref):
    cond = c_ref[...]
    sc = (cond * jax.nn.sigmoid(cond)).astype(BF16)
    o_ref[0] = jnp.dot(sc, w_ref[0].astype(BF16), preferred_element_type=F32) + b_ref[0]


def _ada_mods(cond, ada_w, ada_b, tn=1024):
    depth, d, n = ada_w.shape
    rows = cond.shape[0]
    return pl.pallas_call(
        _ada_body,
        grid=(depth, n // tn),
        in_specs=[pl.BlockSpec((rows, d), lambda l, j: (0, 0)),
                  pl.BlockSpec((1, d, tn), lambda l, j: (l, 0, j)),
                  pl.BlockSpec((1, 1, tn), lambda l, j: (l, 0, j))],
        out_specs=pl.BlockSpec((1, rows, tn), lambda l, j: (l, 0, j)),
        out_shape=jax.ShapeDtypeStruct((depth, rows, n), F32),
        compiler_params=_cparams(("arbitrary", "arbitrary")),
        name="ada_mods",
    )(cond, ada_w, ada_b.reshape(depth, 1, n))


def _modulate_body(ctx_ref, x_ref, mod_ref, xs_ref, h_ref):
    j = pl.program_id(1)

    def emit(src):
        xs_ref[...] = src
        h_ref[...] = (src * (1.0 + mod_ref[0, 1:2, :]) + mod_ref[0, 0:1, :]).astype(h_ref.dtype)

    @pl.when(j == 0)
    def _():
        emit(ctx_ref[...])

    @pl.when(j > 0)
    def _():
        emit(x_ref[...])


def _modulate(x, ctx, mod):
    batch, n_lat, d = x.shape
    n_tiles = n_lat // TILE + 1
    rows = batch * n_tiles * TILE
    row = pl.BlockSpec((TILE, d), lambda b, j: (b * n_tiles + j, 0))
    return pl.pallas_call(
        _modulate_body,
        grid=(batch, n_tiles),
        in_specs=[pl.BlockSpec((None, TILE, d), lambda b, j: (b, 0, 0)),
                  pl.BlockSpec((None, TILE, d), lambda b, j: (b, jnp.maximum(j - 1, 0), 0)),
                  pl.BlockSpec((1, mod.shape[1], d), lambda b, j: (b * n_tiles + j, 0, 0))],
        out_specs=[row, row],
        out_shape=[jax.ShapeDtypeStruct((rows, d), F32), jax.ShapeDtypeStruct((rows, d), BF16)],
        compiler_params=_cparams(("arbitrary", "arbitrary")),
        name="modulate",
    )(ctx, x, mod)


def _dot_nt(a, b):
    return lax.dot_general(a, b, (((1,), (1,)), ((), ())), preferred_element_type=F32)


def _attn_body(q_ref, k_ref, vt_ref, o_ref, s_ref, p_ref, *, group, dq, dv, q_per_kv, n_lat_chunks):
    i = pl.program_id(2)

    def kcol(g):
        return (g // q_per_kv) * dq

    def vrow(g):
        return (g // q_per_kv) * dv

    def finish(g, acc, l8):
        l = jnp.sum(l8, axis=0, keepdims=True)
        out = acc * (1.0 / l)
        o_ref[:, g * dv:(g + 1) * dv] = out.T.astype(o_ref.dtype)

    def colmax8(x):
        return jnp.max(x.reshape(x.shape[0] // 8, 8, x.shape[1]), axis=0)

    def colsum8(x):
        return jnp.sum(x.reshape(x.shape[0] // 8, 8, x.shape[1]), axis=0)

    @pl.when(i == 0)
    def _():
        for g in range(group):
            st = _dot_nt(k_ref[0:TILE, kcol(g):kcol(g) + dq], q_ref[:, g * dq:(g + 1) * dq])
            m = jnp.max(st, axis=0, keepdims=True)
            p = jnp.exp2(st - m)
            acc = jnp.dot(vt_ref[vrow(g):vrow(g) + dv, 0:TILE], p.astype(BF16), preferred_element_type=F32)
            finish(g, acc, colsum8(p))

    @pl.when(i > 0)
    def _():
        neg = jnp.full((8, TILE), -jnp.inf, F32)
        zero8 = jnp.zeros((8, TILE), F32)
        zacc = jnp.zeros((dv, TILE), F32)
        state = {}
        for ph in range(group + 2):
            ga, gb, gc = ph, ph - 1, ph - 2
            do_a, do_b, do_c = ga < group, 0 <= gb < group, 0 <= gc < group
            q_a = q_ref[:, ga * dq:(ga + 1) * dq] if do_a else None
            m_b = jnp.max(state["m8"], axis=0, keepdims=True) if do_b else None

            def stage(start, size, carry, q_a=q_a, m_b=m_b, ga=ga, gb=gb, gc=gc,
                      do_a=do_a, do_b=do_b, do_c=do_c):
                m8, l8, acc = carry

                def value_product(acc):
                    return acc + jnp.dot(vt_ref[vrow(gc):vrow(gc) + dv, pl.ds(start, size)],
                                         p_ref[gc % 2, pl.ds(start, size), :], preferred_element_type=F32)

                if do_c and q_per_kv > 1:
                    acc = value_product(acc)
                if do_a:
                    st = _dot_nt(k_ref[pl.ds(start, size), kcol(ga):kcol(ga) + dq], q_a)
                    s_ref[ga % 2, pl.ds(start, size), :] = st
                    m8 = jnp.maximum(m8, colmax8(st))
                if do_c and q_per_kv == 1:
                    acc = value_product(acc)
                if do_b:
                    p = jnp.exp2(s_ref[gb % 2, pl.ds(start, size), :] - m_b)
                    p_ref[gb % 2, pl.ds(start, size), :] = p.astype(BF16)
                    l8 = l8 + colsum8(p)
                return m8, l8, acc

            carry = stage(0, TILE, (neg, zero8, zacc))

            def body(c, carry, stage=stage):
                return stage(pl.multiple_of(TILE + c * KV_CHUNK, TILE), KV_CHUNK, carry)

            m8, l8, acc = lax.fori_loop(0, n_lat_chunks, body, carry, unroll=True)
            if do_c:
                finish(gc, acc, state["l8"])
            if do_b:
                state["l8"] = l8
            if do_a:
                state["m8"] = m8


def _attention(q, k, vt, *, batch, n_tiles, n_steps, group, dq, dv, q_per_kv):
    r = q.shape[0]
    t_len = n_tiles * TILE
    kvw = group // q_per_kv
    return pl.pallas_call(
        functools.partial(_attn_body, group=group, dq=dq, dv=dv, q_per_kv=q_per_kv,
                          n_lat_chunks=(t_len - TILE) // KV_CHUNK),
        grid=(batch, n_steps, n_tiles),
        in_specs=[pl.BlockSpec((TILE, group * dq), lambda b, h, i: (b * n_tiles + i, h)),
                  pl.BlockSpec((t_len, kvw * dq), lambda b, h, i: (b, h), pipeline_mode=pl.Buffered(1)),
                  pl.BlockSpec((None, kvw * dv, t_len), lambda b, h, i: (b, h, 0), pipeline_mode=pl.Buffered(1))],
        out_specs=pl.BlockSpec((TILE, group * dv), lambda b, h, i: (b * n_tiles + i, h)),
        out_shape=jax.ShapeDtypeStruct((r, n_steps * group * dv), BF16),
        scratch_shapes=[pltpu.VMEM((2, t_len, TILE), F32), pltpu.VMEM((2, t_len, TILE), BF16)],
        compiler_params=_cparams(("arbitrary", "arbitrary", "arbitrary")),
        name="attention",
    )(q, k, vt)


def _lru_in_body(h_ref, prev_ref, next_ref, w_ref, cw_ref, cb_ref, u_ref, y_ref, *, n_tiles, ngroups):
    i = pl.program_id(0)
    width = u_ref.shape[1]
    h_ext = jnp.concatenate([prev_ref[...], h_ref[...], next_ref[...]], axis=0)
    xw = jnp.dot(h_ext, w_ref[:, 0:width], preferred_element_type=F32)
    left = CONV_W // 2
    for g in range(ngroups):
        seg = (i * ngroups + g) % n_tiles
        has_prev = seg > 1
        has_next = jnp.logical_and(seg > 0, seg < n_tiles - 1)
        o = HALO + g * TILE
        ext = jnp.concatenate([jnp.where(has_prev, xw[o - 8:o, :], 0.0), xw[o:o + TILE, :],
                               jnp.where(has_next, xw[o + TILE:o + TILE + 8, :], 0.0)], axis=0)
        n = ext.shape[0]
        acc = cb_ref[...]
        for j in range(CONV_W):
            shift = left - j
            rows = ext if shift == 0 else pltpu.roll(ext, shift % n, 0)
            acc = acc + rows[8:8 + TILE, :] * cw_ref[j:j + 1, :]
        u_ref[g * TILE:(g + 1) * TILE, :] = acc
    y = jnp.dot(h_ref[...], w_ref[:, width:2 * width], preferred_element_type=F32)
    y_ref[...] = jax.nn.gelu(y, approximate=True).astype(y_ref.dtype)


def _lru_in(h, wxy, conv_w, conv_b, *, n_tiles, ngroups):
    r, d = h.shape
    width = wxy.shape[1] // 2
    tm = ngroups * TILE
    per = tm // HALO
    n_halo = r // HALO
    row = pl.BlockSpec((tm, width), lambda i: (i, 0))
    return pl.pallas_call(
        functools.partial(_lru_in_body, n_tiles=n_tiles, ngroups=ngroups),
        grid=(r // tm,),
        in_specs=[pl.BlockSpec((tm, d), lambda i: (i, 0)),
                  pl.BlockSpec((HALO, d), lambda i: (jnp.maximum(i * per - 1, 0), 0)),
                  pl.BlockSpec((HALO, d), lambda i: (jnp.minimum((i + 1) * per, n_halo - 1), 0)),
                  _const_spec(wxy.shape), _const_spec((CONV_W, width)), _const_spec((1, width))],
        out_specs=[row, row],
        out_shape=[jax.ShapeDtypeStruct((r, width), F32), jax.ShapeDtypeStruct((r, width), BF16)],
        compiler_params=_cparams(("arbitrary",)),
        name="lru_in",
    )(h, h, h, wxy, conv_w, conv_b.reshape(1, width))


def _log_sigmoid(x):
    return jnp.minimum(x, 0.0) - jnp.log1p(jnp.exp(-jnp.abs(x)))


def _scan_tile(a, b, carry, reverse):
    n, w = a.shape
    groups = n // 8
    a3 = a.reshape(groups, 8, w)
    b3 = b.reshape(groups, 8, w)
    sub = lax.broadcasted_iota(jnp.int32, (groups, 8, w), 1)
    for s in (1, 2, 4):
        shift = 8 - s if reverse else s
        valid = (sub < 8 - s) if reverse else (sub >= s)
        a_sh = pltpu.roll(a3, shift, 1)
        b_sh = pltpu.roll(b3, shift, 1)
        b3 = jnp.where(valid, b3 + a3 * b_sh, b3)
        a3 = jnp.where(valid, a3 * a_sh, a3)
    outs = [None] * groups
    for r in (range(groups - 1, -1, -1) if reverse else range(groups)):
        hv = b3[r] + a3[r] * carry
        outs[r] = hv
        edge = hv[0:1, :] if reverse else hv[7:8, :]
        carry = jnp.broadcast_to(edge, (8, w))
    return jnp.concatenate(outs, axis=0), carry


def _lru_body(uf_ref, ub_ref, raw_ref, ixw_ref, rab_ref, ixb_ref, lam_ref,
              hf_ref, hb_ref, carry_ref, *, blk):
    j = pl.program_id(1)

    @pl.when(j == 0)
    def _():
        carry_ref[...] = jnp.zeros_like(carry_ref)

    for d, (u_ref, h_ref) in enumerate(((uf_ref, hf_ref), (ub_ref, hb_ref))):
        half_l2 = (0.5 * LRU_C * LOG2E) * _log_sigmoid(lam_ref[d])
        for n in range(LRU_BLOCKS):
            cs = slice(n * blk, (n + 1) * blk)
            u = u_ref[:, cs]
            ub16 = u.astype(BF16)
            t_r = jnp.tanh(jnp.dot(ub16, raw_ref[d, n], preferred_element_type=F32) + rab_ref[d][:, cs])
            t_g = jnp.tanh(jnp.dot(ub16, ixw_ref[d, n], preferred_element_type=F32) + ixb_ref[d][:, cs])
            a = jnp.exp2((t_r + 1.0) * half_l2[:, cs])
            om = 1.0 - a * a
            root = jnp.where(om > 0.0, om * lax.rsqrt(om), 0.0)
            bx = root * ((t_g + 1.0) * (0.5 * u))
            h, carry = _scan_tile(a, bx, carry_ref[d, :, cs], reverse=(d == 1))
            h_ref[:, cs] = h.astype(h_ref.dtype)
            carry_ref[d, :, cs] = carry


def _lru_scan(u, ra_w, ix_w, ra_b, ix_b, lam, *, batch, n_tiles):
    r, w = u.shape
    blk = w // LRU_BLOCKS

    def fwd_map(b, j):
        return (b * n_tiles + j, 0)

    def bwd_map(b, j):
        return (b * n_tiles + jnp.where(j == 0, 0, n_tiles - j), 0)

    full5 = pl.BlockSpec((2, LRU_BLOCKS, blk, blk), lambda b, j: (0, 0, 0, 0))
    vec = pl.BlockSpec((2, 1, w), lambda b, j: (0, 0, 0))
    return pl.pallas_call(
        functools.partial(_lru_body, blk=blk),
        grid=(batch, n_tiles),
        in_specs=[pl.BlockSpec((TILE, w), fwd_map), pl.BlockSpec((TILE, w), bwd_map),
                  full5, full5, vec, vec, vec],
        out_specs=[pl.BlockSpec((TILE, w), fwd_map), pl.BlockSpec((TILE, w), bwd_map)],
        out_shape=[jax.ShapeDtypeStruct((r, w), BF16), jax.ShapeDtypeStruct((r, w), BF16)],
        scratch_shapes=[pltpu.VMEM((2, 8, w), F32)],
        compiler_params=_cparams(("arbitrary", "arbitrary")),
        name="lru_scan",
    )(u, u, ra_w, ix_w, ra_b.reshape(2, 1, w), ix_b.reshape(2, 1, w), lam.reshape(2, 1, w))


def _rope_tables(n_lat, rot_dim):
    m = rot_dim // 2
    half = m // 2
    t = np.arange(n_lat)
    pos = np.stack([t // GRID_W, t % GRID_W], axis=1).astype(np.float32)
    inv = (ROPE_THETA ** (-(np.arange(half, dtype=np.float32) * 2.0) / m)).astype(np.float32)
    lane = np.arange(LANE)
    axis = np.minimum(lane // m, 1)
    freq = inv[lane % half]
    ang = pos[:, axis] * freq[None, :]
    live = (lane < rot_dim)[None, :]
    first = ((lane % m) < half)[None, :]
    cos = np.where(live, np.cos(ang), 1.0)
    sin = np.where(live, np.sin(ang), 0.0)
    s_hi = np.where(first, -sin, 0.0)
    s_lo = np.where(first, 0.0, sin)
    ident = np.concatenate([np.ones((TILE, LANE)), np.zeros((TILE, LANE)), np.zeros((TILE, LANE))], 1)
    tab = np.concatenate([cos, s_hi, s_lo], axis=1)
    tab = np.concatenate([ident, tab], axis=0).astype(np.float32)
    return (jnp.asarray(tab[:, :LANE]), jnp.asarray(tab[:, LANE:2 * LANE]), jnp.asarray(tab[:, 2 * LANE:]))


def kernel(x, c, ctx, c_ctx, ada_w, ada_b, ln_g, ln_b, mlp_w1, mlp_w2, gqa_wq, gqa_wk, gqa_wv, gqa_wo, gqa_q_g, gqa_k_g, lru_wx, lru_wy, lru_conv_w, lru_conv_b, lru_ra_w, lru_ra_b, lru_ix_w, lru_ix_b, lru_lam, lru_wo, mla_wq_a, mla_q_a_g, mla_wq_b, mla_wkv_a, mla_kv_a_g, mla_wkv_b, mla_wo):
    batch, n_lat, d = x.shape
    depth = ada_w.shape[0]
    assert ctx.shape[1] == TILE and n_lat % KV_CHUNK == 0 and n_lat % GRID_W == 0
    t_len = TILE + n_lat
    n_tiles = t_len // TILE
    rows = batch * t_len
    n_groups = rows // TILE
    alpha = (2 * depth) ** 0.25


    cond_rows = -(-(batch + 1) // 8) * 8
    cond = jnp.concatenate([c, c_ctx[None, :], jnp.zeros((cond_rows - batch - 1, d), F32)], axis=0)
    mods = _ada_mods(cond, ada_w, ada_b)
    m_lat = jnp.broadcast_to(mods[:, :batch, None, :], (depth, batch, n_tiles - 1, 6 * d))
    m_ctx = jnp.broadcast_to(mods[:, batch:batch + 1, None, :], (depth, batch, 1, 6 * d))
    mods = jnp.concatenate([m_ctx, m_lat], axis=2).reshape(depth, n_groups, 6, d)

    def mod3(gate, scale, shift):
        return jnp.stack([gate, scale, shift], axis=1)

    tabs_gqa = _rope_tables(n_lat, A_HEAD_DIM)
    tabs_mla = _rope_tables(n_lat, MLA_ROPE)

    def tab_specs():
        return [pl.BlockSpec((TILE, LANE), lambda i: (i % n_tiles, 0)) for _ in range(3)]

    def vt_spec(n):
        return pl.BlockSpec((None, n, TILE), lambda i: (i // n_tiles, 0, i % n_tiles))

    def lat_tile(i):
        return i + i // (n_tiles - 1) + 1

    xs, h = _modulate(x, ctx, mods[0])
    w1_all = mlp_w1.astype(BF16)
    w2_all = mlp_w2.astype(BF16)

    for i in range(depth):
        kind = i % 3
        slot = i // 3
        m_i = mods[i]
        if kind == 0:
            wq, wk, wv, wo = (gqa_wq[slot].astype(BF16), gqa_wk[slot].astype(BF16),
                              gqa_wv[slot].astype(BF16), gqa_wo[slot].astype(BF16))

            nq, nkv = wq.shape[1], wk.shape[1]
            wqkv = jnp.concatenate([wq, wk, wv], axis=1)
            ex = [(gqa_q_g[slot].reshape(1, LANE), _const_spec((1, LANE))),
                  (gqa_k_g[slot].reshape(1, LANE), _const_spec((1, LANE)))] + list(zip(tabs_gqa, tab_specs()))
            outs = [(jax.ShapeDtypeStruct((rows, nq), BF16), _row_spec(TILE, nq)),
                    (jax.ShapeDtypeStruct((rows, nkv), BF16), _row_spec(TILE, nkv)),
                    (jax.ShapeDtypeStruct((batch, nkv, t_len), BF16), vt_spec(nkv))]
            q, k, vt = _proj(h, wqkv, ex, outs, tm=TILE, name="gqa_qkv",
                             epilogue=functools.partial(_ep_gqa_qkv, nq=nq // LANE, nkv=nkv // LANE,
                                                        off=A_HEAD_DIM // 4,
                                                        qscale=A_HEAD_DIM ** -0.5 * LOG2E))
            o = _attention(q, k, vt, batch=batch, n_tiles=n_tiles, n_steps=A_HEADS // GQA_STEP_HEADS,
                           group=GQA_STEP_HEADS, dq=A_HEAD_DIM, dv=A_HEAD_DIM, q_per_kv=A_GROUP)
            a_list, w_out, prologue = [o], wo, None
        elif kind == 1:
            wx, wy, wo = lru_wx[slot].astype(BF16), lru_wy[slot].astype(BF16), lru_wo[slot].astype(BF16)
            u, y = _lru_in(h, jnp.concatenate([wx, wy], axis=1), lru_conv_w[slot], lru_conv_b[slot],
                           n_tiles=n_tiles, ngroups=2)
            hf, hb = _lru_scan(u, (0.5 * lru_ra_w[slot]).astype(BF16), (0.5 * lru_ix_w[slot]).astype(BF16),
                               0.5 * lru_ra_b[slot], 0.5 * lru_ix_b[slot], lru_lam[slot],
                               batch=batch, n_tiles=n_tiles)
            a_list, w_out, prologue = [hf, hb, y], wo, _pro_lru_gate
        else:
            nq = MLA_NOPE + MLA_ROPE
            wq_a = mla_wq_a[slot].astype(BF16)
            wq_b = mla_wq_b[slot].reshape(MLA_LORA, MLA_HEADS, nq)
            wq_b = jnp.concatenate([wq_b, jnp.zeros((MLA_LORA, MLA_HEADS, 2 * LANE - nq), F32)], axis=-1)
            wq_b = wq_b.reshape(MLA_LORA, MLA_HEADS * 2 * LANE).astype(BF16)
            wkv_a = jnp.concatenate([mla_wkv_a[slot], jnp.zeros((d, LANE - MLA_ROPE), F32)], axis=-1).astype(BF16)
            wkv_b = mla_wkv_b[slot].reshape(MLA_LORA, MLA_HEADS, MLA_NOPE + MLA_V)
            wkv_b = jnp.concatenate([wkv_b[:, :, :MLA_NOPE].reshape(MLA_LORA, -1),
                                     wkv_b[:, :, MLA_NOPE:].reshape(MLA_LORA, -1)], axis=-1).astype(BF16)
            wo = mla_wo[slot].astype(BF16)

            q, kfull, vt = _mla_proj(h, jnp.concatenate([wq_a, wkv_a], axis=1), wq_b, wkv_b,
                                     mla_q_a_g[slot], mla_kv_a_g[slot], tabs_mla,
                                     batch=batch, n_tiles=n_tiles, qscale=float(nq) ** -0.5 * LOG2E)
            o = _attention(q, kfull, vt, batch=batch, n_tiles=n_tiles, n_steps=MLA_HEADS // MLA_STEP_HEADS,
                           group=MLA_STEP_HEADS, dq=2 * LANE, dv=MLA_V, q_per_kv=1)
            a_list, w_out, prologue = [o], wo, None

        last = i == depth - 1
        xs, h2 = _mix_out(a_list, w_out, xs, mod3(m_i[:, 2], m_i[:, 4], m_i[:, 3]), ln_g[i, 0], ln_b[i, 0],
                          ngroups=1 if last else 2, out_rows=batch * n_lat if last else rows,
                          row_map=lat_tile if last else (lambda t: t), prologue=prologue, alpha=alpha)
        if not last:
            nxt = mod3(m_i[:, 5], mods[i + 1][:, 1], mods[i + 1][:, 0])
        else:
            zero = jnp.zeros_like(m_i[:, 5])
            nxt = mod3(m_i[:, 5], zero, zero)
            nxt = nxt.reshape(batch, n_tiles, 3, d)[:, 1:].reshape(batch * (n_tiles - 1), 3, d)
        xs, h = _mlp(h2, w1_all, w2_all, i, xs, nxt, ln_g[i, 1], ln_b[i, 1], tm=2 * TILE, tf=1024,
                     alpha=alpha)

    return xs.reshape(batch, n_lat, d)
```

```python
import functools
import math

import numpy as np
import jax
import jax.numpy as jnp
from jax import lax
from jax.experimental import pallas as pl
from jax.experimental.pallas import tpu as pltpu

F32 = jnp.float32
BF16 = jnp.bfloat16

TILE = 256
GRID_W = 64
ROPE_THETA = 10000.0
EPS = 1e-6
LANE = 128
VMEM_LIMIT_MB = 56

A_HEADS = 16
A_KV_HEADS = 4
A_GROUP = A_HEADS // A_KV_HEADS
A_HEAD_DIM = 128

LRU_BLOCKS = 8
CONV_W = 4
LRU_C = 8.0
HALO = 16

MLA_HEADS = 16
MLA_LORA = 512
MLA_NOPE = 128
MLA_ROPE = 64
MLA_V = 128

KV_CHUNK = 512
GQA_STEP_HEADS = 8
MLA_STEP_HEADS = 8
LOG2E = math.log2(math.e)


def _cparams(sem, vmem_mb=VMEM_LIMIT_MB):
    return pltpu.CompilerParams(dimension_semantics=sem,
                                vmem_limit_bytes=vmem_mb * 1024 * 1024)


def _proj_body(*refs, n_ex, epilogue):
    a_ref, w_ref = refs[0], refs[1]
    ex = refs[2:2 + n_ex]
    outs = refs[2 + n_ex:]
    epilogue(jnp.dot(a_ref[...], w_ref[...], preferred_element_type=F32), ex, outs)


def _const_spec(shape):
    zeros = (0,) * len(shape)
    return pl.BlockSpec(shape, lambda i: zeros, pipeline_mode=pl.Buffered(1))


def _proj(a, w, extras, outs, *, tm, epilogue, name=None):
    m, kdim = a.shape
    assert m % tm == 0 and w.shape[0] == kdim
    return pl.pallas_call(
        functools.partial(_proj_body, n_ex=len(extras), epilogue=epilogue),
        grid=(m // tm,),
        in_specs=[pl.BlockSpec((tm, kdim), lambda i: (i, 0)), _const_spec(w.shape)] + [s for _, s in extras],
        out_specs=[s for _, s in outs],
        out_shape=[s for s, _ in outs],
        compiler_params=_cparams(("arbitrary",)),
        name=name,
    )(a, w, *[arr for arr, _ in extras])


def _row_spec(tm, width):
    return pl.BlockSpec((tm, width), lambda i: (i, 0))


def _mix_out_body(*refs, n_a, ngroups, prologue, alpha):
    a_refs = refs[:n_a]
    w_ref, xold_ref, mod_ref, lng_ref, lnb_ref, x_out, h_out = refs[n_a:]
    for g in range(ngroups):
        r0 = g * TILE
        vals = [r[r0:r0 + TILE, :] for r in a_refs]
        a = vals[0] if prologue is None else prologue(vals)
        part = jnp.dot(a, w_ref[...], preferred_element_type=F32)
        _resid_ln_rows(part, r0, xold_ref, mod_ref, lng_ref, lnb_ref, x_out, h_out, alpha)


def _mix_out(a_list, w, x_old, mod, lng, lnb, *, ngroups, out_rows, row_map, prologue, alpha):
    kdim, d = w.shape
    tm = ngroups * TILE
    src = lambda i: (row_map(i), 0)
    dst = pl.BlockSpec((tm, d), lambda i: (i, 0))
    return pl.pallas_call(
        functools.partial(_mix_out_body, n_a=len(a_list), ngroups=ngroups, prologue=prologue, alpha=alpha),
        grid=(out_rows // tm,),
        in_specs=[pl.BlockSpec((tm, kdim), src) for _ in a_list]
        + [_const_spec(w.shape), pl.BlockSpec((tm, d), src),
           pl.BlockSpec((ngroups, 3, d), lambda i: (row_map(i), 0, 0)),
           _const_spec((1, d)), _const_spec((1, d))],
        out_specs=[dst, dst],
        out_shape=[jax.ShapeDtypeStruct((out_rows, d), F32), jax.ShapeDtypeStruct((out_rows, d), BF16)],
        compiler_params=_cparams(("arbitrary",)),
        name="mix_out",
    )(*a_list, w, x_old, mod, lng.reshape(1, d), lnb.reshape(1, d))


def _rope(x, cos, s_hi, s_lo, off):
    return x * cos + pltpu.roll(x, LANE - off, 1) * s_hi + pltpu.roll(x, off, 1) * s_lo


def _ep_headnorm_rope(acc, ex, outs, *, nblk, off, scale):
    g_ref, cos_ref, shi_ref, slo_ref = ex
    g = g_ref[...]
    cos, shi, slo = cos_ref[...], shi_ref[...], slo_ref[...]
    for j in range(nblk):
        x = acc[:, j * LANE:(j + 1) * LANE]
        ms = jnp.mean(x * x, axis=-1, keepdims=True)
        xn = x * lax.rsqrt(ms + EPS) * g
        y = _rope(xn, cos, shi, slo, off)
        if scale != 1.0:
            y = y * scale
        outs[0][:, j * LANE:(j + 1) * LANE] = y.astype(outs[0].dtype)


def _ep_gqa_qkv(acc, ex, outs, *, nq, nkv, off, qscale):
    qg_ref, kg_ref, cos_ref, shi_ref, slo_ref = ex
    q_out, k_out, vt_out = outs
    tabs = (cos_ref, shi_ref, slo_ref)
    _ep_headnorm_rope(acc[:, :nq * LANE], (qg_ref,) + tabs, (q_out,), nblk=nq, off=off, scale=qscale)
    _ep_headnorm_rope(acc[:, nq * LANE:(nq + nkv) * LANE], (kg_ref,) + tabs, (k_out,), nblk=nkv, off=off,
                      scale=1.0)
    vt_out[...] = acc[:, (nq + nkv) * LANE:].T.astype(vt_out.dtype)


def _rms(x, g):
    ms = jnp.mean(x * x, axis=-1, keepdims=True)
    return x * lax.rsqrt(ms + EPS) * g


def _ep_mla_q(acc, ex, outs, *, nheads, off, scale):
    cos_ref, shi_ref, slo_ref = ex
    cos, shi, slo = cos_ref[...], shi_ref[...], slo_ref[...]
    for h in range(nheads):
        c0 = h * 2 * LANE
        outs[0][:, c0:c0 + LANE] = (acc[:, c0:c0 + LANE] * scale).astype(outs[0].dtype)
        pe = acc[:, c0 + LANE:c0 + 2 * LANE]
        outs[0][:, c0 + LANE:c0 + 2 * LANE] = (_rope(pe, cos, shi, slo, off) * scale).astype(outs[0].dtype)


def _ep_mla_kv(acc, ex, outs, *, nheads):
    kpe_ref, = ex
    k_out, vt_out = outs
    kpe = kpe_ref[...]
    for h in range(nheads):
        k_out[:, 2 * h * LANE:(2 * h + 1) * LANE] = acc[:, h * LANE:(h + 1) * LANE].astype(k_out.dtype)
        k_out[:, (2 * h + 1) * LANE:(2 * h + 2) * LANE] = kpe
    vt_out[...] = acc[:, nheads * LANE:2 * nheads * LANE].T.astype(vt_out.dtype)


def _resid_ln_rows(f_out, r0, xold_ref, mod_ref, lng_ref, lnb_ref, x_out, h_out, alpha):
    n = f_out.shape[0]
    g = r0 // TILE
    assert r0 % TILE + n <= TILE
    gate = mod_ref[g, 0:1, :]
    scale = mod_ref[g, 1:2, :]
    shift = mod_ref[g, 2:3, :]
    y = alpha * xold_ref[r0:r0 + n, :] + (1.0 + gate) * f_out
    mu = jnp.mean(y, axis=-1, keepdims=True)
    yc = y - mu
    var = jnp.mean(yc * yc, axis=-1, keepdims=True)
    xn = yc * lax.rsqrt(var + EPS) * lng_ref[...] + lnb_ref[...]
    x_out[r0:r0 + n, :] = xn
    h_out[r0:r0 + n, :] = (xn * (1.0 + scale) + shift).astype(h_out.dtype)


def _pro_lru_gate(vals):
    hf, hb, y = vals
    return ((hf.astype(F32) + hb.astype(F32)) * y.astype(F32)).astype(BF16)


def _mlp_body(h_ref, w1_ref, w2_ref, xold_ref, mod_ref, lng_ref, lnb_ref, x_out, h_out, acc_ref,
              *, nf, ngroups, alpha):
    f = pl.program_id(1)

    def hidden():
        u = jnp.maximum(jnp.dot(h_ref[...], w1_ref[...], preferred_element_type=F32), 0.0)
        return (u * u).astype(BF16)

    @pl.when(f == 0)
    def _():
        acc_ref[...] = jnp.dot(hidden(), w2_ref[...], preferred_element_type=F32)

    @pl.when(jnp.logical_and(f > 0, f < nf - 1))
    def _():
        acc_ref[...] += jnp.dot(hidden(), w2_ref[...], preferred_element_type=F32)

    @pl.when(f == nf - 1)
    def _():
        u2 = hidden()
        for g in range(ngroups):
            r0 = g * TILE
            tot = acc_ref[r0:r0 + TILE, :] + jnp.dot(u2[r0:r0 + TILE, :], w2_ref[...],
                                                   preferred_element_type=F32)
            _resid_ln_rows(tot, r0, xold_ref, mod_ref, lng_ref, lnb_ref, x_out, h_out, alpha)


def _mlp(h, w1, w2, layer, x_old, mod, lng, lnb, *, tm, tf, alpha):
    rows, d = h.shape
    ff = w1.shape[2]
    nf = ff // tf
    assert nf >= 2
    row = pl.BlockSpec((tm, d), lambda i, f: (i, 0))
    vec = pl.BlockSpec((1, d), lambda i, f: (0, 0))
    return pl.pallas_call(
        functools.partial(_mlp_body, nf=nf, ngroups=tm // TILE, alpha=alpha),
        grid=(rows // tm, nf),
        in_specs=[row,
                  pl.BlockSpec((None, d, tf), lambda i, f: (layer, 0, f)),
                  pl.BlockSpec((None, tf, d), lambda i, f: (layer, f, 0)),
                  row,
                  pl.BlockSpec((tm // TILE, 3, d), lambda i, f: (i, 0, 0)),
                  vec, vec],
        out_specs=[row, row],
        out_shape=[jax.ShapeDtypeStruct((rows, d), F32), jax.ShapeDtypeStruct((rows, d), BF16)],
        scratch_shapes=[pltpu.VMEM((tm, d), F32)],
        compiler_params=_cparams(("arbitrary", "arbitrary")),
        name="mlp",
    )(h, w1, w2, x_old, mod, lng.reshape(1, d), lnb.reshape(1, d))


def _mla_proj_body(h_ref, wa_ref, wqb_ref, wkvb_ref, qg_ref, kvg_ref, cos_ref, shi_ref, slo_ref,
                   q_out, k_out, vt_out, *, lora, nheads, off, qscale):
    tabs = (cos_ref, shi_ref, slo_ref)
    t = jnp.dot(h_ref[...], wa_ref[...], preferred_element_type=F32)
    qa = _rms(t[:, 0:lora], qg_ref[...]).astype(BF16)
    ckv = _rms(t[:, lora:2 * lora], kvg_ref[...]).astype(BF16)
    kpe = _rope(t[:, 2 * lora:2 * lora + LANE], cos_ref[...], shi_ref[...], slo_ref[...], off).astype(BF16)
    _ep_mla_q(jnp.dot(qa, wqb_ref[...], preferred_element_type=F32), tabs, (q_out,),
              nheads=nheads, off=off, scale=qscale)
    _ep_mla_kv(jnp.dot(ckv, wkvb_ref[...], preferred_element_type=F32), (kpe,), (k_out, vt_out),
               nheads=nheads)


def _mla_proj(h, w_a, wq_b, wkv_b, q_g, kv_g, tabs, *, batch, n_tiles, qscale):
    rows, d = h.shape
    t_len = n_tiles * TILE
    nq, nkv = wq_b.shape[1], wkv_b.shape[1]
    nv = MLA_HEADS * MLA_V

    const = _const_spec
    tab = pl.BlockSpec((TILE, LANE), lambda i: (i % n_tiles, 0))
    return pl.pallas_call(
        functools.partial(_mla_proj_body, lora=MLA_LORA, nheads=MLA_HEADS, off=MLA_ROPE // 4, qscale=qscale),
        grid=(rows // TILE,),
        in_specs=[pl.BlockSpec((TILE, d), lambda i: (i, 0)),
                  const(w_a.shape), const(wq_b.shape), const(wkv_b.shape),
                  const((1, MLA_LORA)), const((1, MLA_LORA)), tab, tab, tab],
        out_specs=[pl.BlockSpec((TILE, nq), lambda i: (i, 0)),
                   pl.BlockSpec((TILE, nq), lambda i: (i, 0)),
                   pl.BlockSpec((None, nv, TILE), lambda i: (i // n_tiles, 0, i % n_tiles))],
        out_shape=[jax.ShapeDtypeStruct((rows, nq), BF16), jax.ShapeDtypeStruct((rows, nq), BF16),
                   jax.ShapeDtypeStruct((batch, nv, t_len), BF16)],
        compiler_params=_cparams(("arbitrary",)),
        name="mla_proj",
    )(h, w_a, wq_b, wkv_b, q_g.reshape(1, MLA_LORA), kv_g.reshape(1, MLA_LORA), *tabs)


def _ada_body(c_ref, w_ref, b_ref, o_ref):
    cond = c_ref[...]
    sc = (cond * jax.nn.sigmoid(cond)).astype(BF16)
    o_ref[0] = jnp.dot(sc, w_ref[0].astype(BF16), preferred_element_type=F32) + b_ref[0]


def _ada_mods(cond, ada_w, ada_b, tn=1024):
    depth, d, n = ada_w.shape
    rows = cond.shape[0]
    return pl.pallas_call(
        _ada_body,
        grid=(depth, n // tn),
        in_specs=[pl.BlockSpec((rows, d), lambda l, j: (0, 0)),
                  pl.BlockSpec((1, d, tn), lambda l, j: (l, 0, j)),
                  pl.BlockSpec((1, 1, tn), lambda l, j: (l, 0, j))],
        out_specs=pl.BlockSpec((1, rows, tn), lambda l, j: (l, 0, j)),
        out_shape=jax.ShapeDtypeStruct((depth, rows, n), F32),
        compiler_params=_cparams(("arbitrary", "arbitrary")),
        name="ada_mods",
    )(cond, ada_w, ada_b.reshape(depth, 1, n))


def _modulate_body(ctx_ref, x_ref, mod_ref, xs_ref, h_ref):
    j = pl.program_id(1)

    def emit(src):
        xs_ref[...] = src
        h_ref[...] = (src * (1.0 + mod_ref[0, 1:2, :]) + mod_ref[0, 0:1, :]).astype(h_ref.dtype)

    @pl.when(j == 0)
    def _():
        emit(ctx_ref[...])

    @pl.when(j > 0)
    def _():
        emit(x_ref[...])


def _modulate(x, ctx, mod):
    batch, n_lat, d = x.shape
    n_tiles = n_lat // TILE + 1
    rows = batch * n_tiles * TILE
    row = pl.BlockSpec((TILE, d), lambda b, j: (b * n_tiles + j, 0))
    return pl.pallas_call(
        _modulate_body,
        grid=(batch, n_tiles),
        in_specs=[pl.BlockSpec((None, TILE, d), lambda b, j: (b, 0, 0)),
                  pl.BlockSpec((None, TILE, d), lambda b, j: (b, jnp.maximum(j - 1, 0), 0)),
                  pl.BlockSpec((1, mod.shape[1], d), lambda b, j: (b * n_tiles + j, 0, 0))],
        out_specs=[row, row],
        out_shape=[jax.ShapeDtypeStruct((rows, d), F32), jax.ShapeDtypeStruct((rows, d), BF16)],
        compiler_params=_cparams(("arbitrary", "arbitrary")),
        name="modulate",
    )(ctx, x, mod)


def _dot_nt(a, b):
    return lax.dot_general(a, b, (((1,), (1,)), ((), ())), preferred_element_type=F32)


def _attn_body(q_ref, k_ref, vt_ref, o_ref, s_ref, p_ref, *, group, dq, dv, q_per_kv, n_lat_chunks):
    i = pl.program_id(2)

    def kcol(g):
        return (g // q_per_kv) * dq

    def vrow(g):
        return (g // q_per_kv) * dv

    def finish(g, acc, l8):
        l = jnp.sum(l8, axis=0, keepdims=True)
        out = acc * (1.0 / l)
        o_ref[:, g * dv:(g + 1) * dv] = out.T.astype(o_ref.dtype)

    def colmax8(x):
        return jnp.max(x.reshape(x.shape[0] // 8, 8, x.shape[1]), axis=0)

    def colsum8(x):
        return jnp.sum(x.reshape(x.shape[0] // 8, 8, x.shape[1]), axis=0)

    @pl.when(i == 0)
    def _():
        for g in range(group):
            st = _dot_nt(k_ref[0:TILE, kcol(g):kcol(g) + dq], q_ref[:, g * dq:(g + 1) * dq])
            m = jnp.max(st, axis=0, keepdims=True)
            p = jnp.exp2(st - m)
            acc = jnp.dot(vt_ref[vrow(g):vrow(g) + dv, 0:TILE], p.astype(BF16), preferred_element_type=F32)
            finish(g, acc, colsum8(p))

    @pl.when(i > 0)
    def _():
        neg = jnp.full((8, TILE), -jnp.inf, F32)
        zero8 = jnp.zeros((8, TILE), F32)
        zacc = jnp.zeros((dv, TILE), F32)
        state = {}
        for ph in range(group + 2):
            ga, gb, gc = ph, ph - 1, ph - 2
            do_a, do_b, do_c = ga < group, 0 <= gb < group, 0 <= gc < group
            q_a = q_ref[:, ga * dq:(ga + 1) * dq] if do_a else None
            m_b = jnp.max(state["m8"], axis=0, keepdims=True) if do_b else None

            def stage(start, size, carry, q_a=q_a, m_b=m_b, ga=ga, gb=gb, gc=gc,
                      do_a=do_a, do_b=do_b, do_c=do_c):
                m8, l8, acc = carry

                def value_product(acc):
                    return acc + jnp.dot(vt_ref[vrow(gc):vrow(gc) + dv, pl.ds(start, size)],
                                         p_ref[gc % 2, pl.ds(start, size), :], preferred_element_type=F32)

                if do_c and q_per_kv > 1:
                    acc = value_product(acc)
                if do_a:
                    st = _dot_nt(k_ref[pl.ds(start, size), kcol(ga):kcol(ga) + dq], q_a)
                    s_ref[ga % 2, pl.ds(start, size), :] = st
                    m8 = jnp.maximum(m8, colmax8(st))
                if do_c and q_per_kv == 1:
                    acc = value_product(acc)
                if do_b:
                    p = jnp.exp2(s_ref[gb % 2, pl.ds(start, size), :] - m_b)
                    p_ref[gb % 2, pl.ds(start, size), :] = p.astype(BF16)
                    l8 = l8 + colsum8(p)
                return m8, l8, acc

            carry = stage(0, TILE, (neg, zero8, zacc))

            def body(c, carry, stage=stage):
                return stage(pl.multiple_of(TILE + c * KV_CHUNK, TILE), KV_CHUNK, carry)

            m8, l8, acc = lax.fori_loop(0, n_lat_chunks, body, carry, unroll=True)
            if do_c:
                finish(gc, acc, state["l8"])
            if do_b:
                state["l8"] = l8
            if do_a:
                state["m8"] = m8


def _attention(q, k, vt, *, batch, n_tiles, n_steps, group, dq, dv, q_per_kv):
    r = q.shape[0]
    t_len = n_tiles * TILE
    kvw = group // q_per_kv
    return pl.pallas_call(
        functools.partial(_attn_body, group=group, dq=dq, dv=dv, q_per_kv=q_per_kv,
                          n_lat_chunks=(t_len - TILE) // KV_CHUNK),
        grid=(batch, n_steps, n_tiles),
        in_specs=[pl.BlockSpec((TILE, group * dq), lambda b, h, i: (b * n_tiles + i, h)),
                  pl.BlockSpec((t_len, kvw * dq), lambda b, h, i: (b, h), pipeline_mode=pl.Buffered(1)),
                  pl.BlockSpec((None, kvw * dv, t_len), lambda b, h, i: (b, h, 0), pipeline_mode=pl.Buffered(1))],
        out_specs=pl.BlockSpec((TILE, group * dv), lambda b, h, i: (b * n_tiles + i, h)),
        out_shape=jax.ShapeDtypeStruct((r, n_steps * group * dv), BF16),
        scratch_shapes=[pltpu.VMEM((2, t_len, TILE), F32), pltpu.VMEM((2, t_len, TILE), BF16)],
        compiler_params=_cparams(("arbitrary", "arbitrary", "arbitrary")),
        name="attention",
    )(q, k, vt)


def _lru_in_body(h_ref, prev_ref, next_ref, w_ref, cw_ref, cb_ref, u_ref, y_ref, *, n_tiles, ngroups):
    i = pl.program_id(0)
    width = u_ref.shape[1]
    h_ext = jnp.concatenate([prev_ref[...], h_ref[...], next_ref[...]], axis=0)
    xw = jnp.dot(h_ext, w_ref[:, 0:width], preferred_element_type=F32)
    left = CONV_W // 2
    for g in range(ngroups):
        seg = (i * ngroups + g) % n_tiles
        has_prev = seg > 1
        has_next = jnp.logical_and(seg > 0, seg < n_tiles - 1)
        o = HALO + g * TILE
        ext = jnp.concatenate([jnp.where(has_prev, xw[o - 8:o, :], 0.0), xw[o:o + TILE, :],
                               jnp.where(has_next, xw[o + TILE:o + TILE + 8, :], 0.0)], axis=0)
        n = ext.shape[0]
        acc = cb_ref[...]
        for j in range(CONV_W):
            shift = left - j
            rows = ext if shift == 0 else pltpu.roll(ext, shift % n, 0)
            acc = acc + rows[8:8 + TILE, :] * cw_ref[j:j + 1, :]
        u_ref[g * TILE:(g + 1) * TILE, :] = acc
    y = jnp.dot(h_ref[...], w_ref[:, width:2 * width], preferred_element_type=F32)
    y_ref[...] = jax.nn.gelu(y, approximate=True).astype(y_ref.dtype)


def _lru_in(h, wxy, conv_w, conv_b, *, n_tiles, ngroups):
    r, d = h.shape
    width = wxy.shape[1] // 2
    tm = ngroups * TILE
    per = tm // HALO
    n_halo = r // HALO
    row = pl.BlockSpec((tm, width), lambda i: (i, 0))
    return pl.pallas_call(
        functools.partial(_lru_in_body, n_tiles=n_tiles, ngroups=ngroups),
        grid=(r // tm,),
        in_specs=[pl.BlockSpec((tm, d), lambda i: (i, 0)),
                  pl.BlockSpec((HALO, d), lambda i: (jnp.maximum(i * per - 1, 0), 0)),
                  pl.BlockSpec((HALO, d), lambda i: (jnp.minimum((i + 1) * per, n_halo - 1), 0)),
                  _const_spec(wxy.shape), _const_spec((CONV_W, width)), _const_spec((1, width))],
        out_specs=[row, row],
        out_shape=[jax.ShapeDtypeStruct((r, width), F32), jax.ShapeDtypeStruct((r, width), BF16)],
        compiler_params=_cparams(("arbitrary",)),
        name="lru_in",
    )(h, h, h, wxy, conv_w, conv_b.reshape(1, width))


def _log_sigmoid(x):
    return jnp.minimum(x, 0.0) - jnp.log1p(jnp.exp(-jnp.abs(x)))


def _scan_tile(a, b, carry, reverse):
    n, w = a.shape
    groups = n // 8
    a3 = a.reshape(groups, 8, w)
    b3 = b.reshape(groups, 8, w)
    sub = lax.broadcasted_iota(jnp.int32, (groups, 8, w), 1)
    for s in (1, 2, 4):
        shift = 8 - s if reverse else s
        valid = (sub < 8 - s) if reverse else (sub >= s)
        a_sh = pltpu.roll(a3, shift, 1)
        b_sh = pltpu.roll(b3, shift, 1)
        b3 = jnp.where(valid, b3 + a3 * b_sh, b3)
        a3 = jnp.where(valid, a3 * a_sh, a3)
    outs = [None] * groups
    for r in (range(groups - 1, -1, -1) if reverse else range(groups)):
        hv = b3[r] + a3[r] * carry
        outs[r] = hv
        edge = hv[0:1, :] if reverse else hv[7:8, :]
        carry = jnp.broadcast_to(edge, (8, w))
    return jnp.concatenate(outs, axis=0), carry


def _lru_body(uf_ref, ub_ref, raw_ref, ixw_ref, rab_ref, ixb_ref, lam_ref,
              hf_ref, hb_ref, carry_ref, *, blk):
    j = pl.program_id(1)

    @pl.when(j == 0)
    def _():
        carry_ref[...] = jnp.zeros_like(carry_ref)

    for d, (u_ref, h_ref) in enumerate(((uf_ref, hf_ref), (ub_ref, hb_ref))):
        half_l2 = (0.5 * LRU_C * LOG2E) * _log_sigmoid(lam_ref[d])
        for n in range(LRU_BLOCKS):
            cs = slice(n * blk, (n + 1) * blk)
            u = u_ref[:, cs]
            ub16 = u.astype(BF16)
            t_r = jnp.tanh(jnp.dot(ub16, raw_ref[d, n], preferred_element_type=F32) + rab_ref[d][:, cs])
            t_g = jnp.tanh(jnp.dot(ub16, ixw_ref[d, n], preferred_element_type=F32) + ixb_ref[d][:, cs])
            a = jnp.exp2((t_r + 1.0) * half_l2[:, cs])
            om = 1.0 - a * a
            root = jnp.where(om > 0.0, om * lax.rsqrt(om), 0.0)
            bx = root * ((t_g + 1.0) * (0.5 * u))
            h, carry = _scan_tile(a, bx, carry_ref[d, :, cs], reverse=(d == 1))
            h_ref[:, cs] = h.astype(h_ref.dtype)
            carry_ref[d, :, cs] = carry


def _lru_scan(u, ra_w, ix_w, ra_b, ix_b, lam, *, batch, n_tiles):
    r, w = u.shape
    blk = w // LRU_BLOCKS

    def fwd_map(b, j):
        return (b * n_tiles + j, 0)

    def bwd_map(b, j):
        return (b * n_tiles + jnp.where(j == 0, 0, n_tiles - j), 0)

    full5 = pl.BlockSpec((2, LRU_BLOCKS, blk, blk), lambda b, j: (0, 0, 0, 0))
    vec = pl.BlockSpec((2, 1, w), lambda b, j: (0, 0, 0))
    return pl.pallas_call(
        functools.partial(_lru_body, blk=blk),
        grid=(batch, n_tiles),
        in_specs=[pl.BlockSpec((TILE, w), fwd_map), pl.BlockSpec((TILE, w), bwd_map),
                  full5, full5, vec, vec, vec],
        out_specs=[pl.BlockSpec((TILE, w), fwd_map), pl.BlockSpec((TILE, w), bwd_map)],
        out_shape=[jax.ShapeDtypeStruct((r, w), BF16), jax.ShapeDtypeStruct((r, w), BF16)],
        scratch_shapes=[pltpu.VMEM((2, 8, w), F32)],
        compiler_params=_cparams(("arbitrary", "arbitrary")),
        name="lru_scan",
    )(u, u, ra_w, ix_w, ra_b.reshape(2, 1, w), ix_b.reshape(2, 1, w), lam.reshape(2, 1, w))


def _rope_tables(n_lat, rot_dim):
    m = rot_dim // 2
    half = m // 2
    t = np.arange(n_lat)
    pos = np.stack([t // GRID_W, t % GRID_W], axis=1).astype(np.float32)
    inv = (ROPE_THETA ** (-(np.arange(half, dtype=np.float32) * 2.0) / m)).astype(np.float32)
    lane = np.arange(LANE)
    axis = np.minimum(lane // m, 1)
    freq = inv[lane % half]
    ang = pos[:, axis] * freq[None, :]
    live = (lane < rot_dim)[None, :]
    first = ((lane % m) < half)[None, :]
    cos = np.where(live, np.cos(ang), 1.0)
    sin = np.where(live, np.sin(ang), 0.0)
    s_hi = np.where(first, -sin, 0.0)
    s_lo = np.where(first, 0.0, sin)
    ident = np.concatenate([np.ones((TILE, LANE)), np.zeros((TILE, LANE)), np.zeros((TILE, LANE))], 1)
    tab = np.concatenate([cos, s_hi, s_lo], axis=1)
    tab = np.concatenate([ident, tab], axis=0).astype(np.float32)
    return (jnp.asarray(tab[:, :LANE]), jnp.asarray(tab[:, LANE:2 * LANE]), jnp.asarray(tab[:, 2 * LANE:]))


def kernel(x, c, ctx, c_ctx, ada_w, ada_b, ln_g, ln_b, mlp_w1, mlp_w2, gqa_wq, gqa_wk, gqa_wv, gqa_wo, gqa_q_g, gqa_k_g, lru_wx, lru_wy, lru_conv_w, lru_conv_b, lru_ra_w, lru_ra_b, lru_ix_w, lru_ix_b, lru_lam, lru_wo, mla_wq_a, mla_q_a_g, mla_wq_b, mla_wkv_a, mla_kv_a_g, mla_wkv_b, mla_wo):
    batch, n_lat, d = x.shape
    depth = ada_w.shape[0]
    assert ctx.shape[1] == TILE and n_lat % KV_CHUNK == 0 and n_lat % GRID_W == 0
    t_len = TILE + n_lat
    n_tiles = t_len // TILE
    rows = batch * t_len
    n_groups = rows // TILE
    alpha = (2 * depth) ** 0.25


    cond_rows = -(-(batch + 1) // 8) * 8
    cond = jnp.concatenate([c, c_ctx[None, :], jnp.zeros((cond_rows - batch - 1, d), F32)], axis=0)
    mods = _ada_mods(cond, ada_w, ada_b)
    m_lat = jnp.broadcast_to(mods[:, :batch, None, :], (depth, batch, n_tiles - 1, 6 * d))
    m_ctx = jnp.broadcast_to(mods[:, batch:batch + 1, None, :], (depth, batch, 1, 6 * d))
    mods = jnp.concatenate([m_ctx, m_lat], axis=2).reshape(depth, n_groups, 6, d)

    def mod3(gate, scale, shift):
        return jnp.stack([gate, scale, shift], axis=1)

    tabs_gqa = _rope_tables(n_lat, A_HEAD_DIM)
    tabs_mla = _rope_tables(n_lat, MLA_ROPE)

    def tab_specs():
        return [pl.BlockSpec((TILE, LANE), lambda i: (i % n_tiles, 0)) for _ in range(3)]

    def vt_spec(n):
        return pl.BlockSpec((None, n, TILE), lambda i: (i // n_tiles, 0, i % n_tiles))

    def lat_tile(i):
        return i + i // (n_tiles - 1) + 1

    xs, h = _modulate(x, ctx, mods[0])
    w1_all = mlp_w1.astype(BF16)
    w2_all = mlp_w2.astype(BF16)

    for i in range(depth):
        kind = i % 3
        slot = i // 3
        m_i = mods[i]
        if kind == 0:
            wq, wk, wv, wo = (gqa_wq[slot].astype(BF16), gqa_wk[slot].astype(BF16),
                              gqa_wv[slot].astype(BF16), gqa_wo[slot].astype(BF16))

            nq, nkv = wq.shape[1], wk.shape[1]
            wqkv = jnp.concatenate([wq, wk, wv], axis=1)
            ex = [(gqa_q_g[slot].reshape(1, LANE), _const_spec((1, LANE))),
                  (gqa_k_g[slot].reshape(1, LANE), _const_spec((1, LANE)))] + list(zip(tabs_gqa, tab_specs()))
            outs = [(jax.ShapeDtypeStruct((rows, nq), BF16), _row_spec(TILE, nq)),
                    (jax.ShapeDtypeStruct((rows, nkv), BF16), _row_spec(TILE, nkv)),
                    (jax.ShapeDtypeStruct((batch, nkv, t_len), BF16), vt_spec(nkv))]
            q, k, vt = _proj(h, wqkv, ex, outs, tm=TILE, name="gqa_qkv",
                             epilogue=functools.partial(_ep_gqa_qkv, nq=nq // LANE, nkv=nkv // LANE,
                                                        off=A_HEAD_DIM // 4,
                                                        qscale=A_HEAD_DIM ** -0.5 * LOG2E))
            o = _attention(q, k, vt, batch=batch, n_tiles=n_tiles, n_steps=A_HEADS // GQA_STEP_HEADS,
                           group=GQA_STEP_HEADS, dq=A_HEAD_DIM, dv=A_HEAD_DIM, q_per_kv=A_GROUP)
            a_list, w_out, prologue = [o], wo, None
        elif kind == 1:
            wx, wy, wo = lru_wx[slot].astype(BF16), lru_wy[slot].astype(BF16), lru_wo[slot].astype(BF16)
            u, y = _lru_in(h, jnp.concatenate([wx, wy], axis=1), lru_conv_w[slot], lru_conv_b[slot],
                           n_tiles=n_tiles, ngroups=2)
            hf, hb = _lru_scan(u, (0.5 * lru_ra_w[slot]).astype(BF16), (0.5 * lru_ix_w[slot]).astype(BF16),
                               0.5 * lru_ra_b[slot], 0.5 * lru_ix_b[slot], lru_lam[slot],
                               batch=batch, n_tiles=n_tiles)
            a_list, w_out, prologue = [hf, hb, y], wo, _pro_lru_gate
        else:
            nq = MLA_NOPE + MLA_ROPE
            wq_a = mla_wq_a[slot].astype(BF16)
            wq_b = mla_wq_b[slot].reshape(MLA_LORA, MLA_HEADS, nq)
            wq_b = jnp.concatenate([wq_b, jnp.zeros((MLA_LORA, MLA_HEADS, 2 * LANE - nq), F32)], axis=-1)
            wq_b = wq_b.reshape(MLA_LORA, MLA_HEADS * 2 * LANE).astype(BF16)
            wkv_a = jnp.concatenate([mla_wkv_a[slot], jnp.zeros((d, LANE - MLA_ROPE), F32)], axis=-1).astype(BF16)
            wkv_b = mla_wkv_b[slot].reshape(MLA_LORA, MLA_HEADS, MLA_NOPE + MLA_V)
            wkv_b = jnp.concatenate([wkv_b[:, :, :MLA_NOPE].reshape(MLA_LORA, -1),
                                     wkv_b[:, :, MLA_NOPE:].reshape(MLA_LORA, -1)], axis=-1).astype(BF16)
            wo = mla_wo[slot].astype(BF16)

            q, kfull, vt = _mla_proj(h, jnp.concatenate([wq_a, wkv_a], axis=1), wq_b, wkv_b,
                                     mla_q_a_g[slot], mla_kv_a_g[slot], tabs_mla,
                                     batch=batch, n_tiles=n_tiles, qscale=float(nq) ** -0.5 * LOG2E)
            o = _attention(q, kfull, vt, batch=batch, n_tiles=n_tiles, n_steps=MLA_HEADS // MLA_STEP_HEADS,
                           group=MLA_STEP_HEADS, dq=2 * LANE, dv=MLA_V, q_per_kv=1)
            a_list, w_out, prologue = [o], wo, None

        last = i == depth - 1
        xs, h2 = _mix_out(a_list, w_out, xs, mod3(m_i[:, 2], m_i[:, 4], m_i[:, 3]), ln_g[i, 0], ln_b[i, 0],
                          ngroups=1 if last else 2, out_rows=batch * n_lat if last else rows,
                          row_map=lat_tile if last else (lambda t: t), prologue=prologue, alpha=alpha)
        if not last:
            nxt = mod3(m_i[:, 5], mods[i + 1][:, 1], mods[i + 1][:, 0])
        else:
            zero = jnp.zeros_like(m_i[:, 5])
            nxt = mod3(m_i[:, 5], zero, zero)
            nxt = nxt.reshape(batch, n_tiles, 3, d)[:, 1:].reshape(batch * (n_tiles - 1), 3, d)
        xs, h = _mlp(h2, w1_all, w2_all, i, xs, nxt, ln_g[i, 1], ln_b[i, 1], tm=2 * TILE, tf=1024,
                     alpha=alpha)

    return xs.reshape(batch, n_lat, d)
```

```python
import functools
import math

import numpy as np
import jax
import jax.numpy as jnp
from jax import lax
from jax.experimental import pallas as pl
from jax.experimental.pallas import tpu as pltpu

F32 = jnp.float32
BF16 = jnp.bfloat16

TILE = 256
GRID_W = 64
ROPE_THETA = 10000.0
EPS = 1e-6
LANE = 128
SUBLANE = 8
VMEM_LIMIT_MB = 56

A_HEADS = 16
A_KV_HEADS = 4
A_GROUP = A_HEADS // A_KV_HEADS
A_HEAD_DIM = 128

LRU_BLOCKS = 8
CONV_W = 4
LRU_C = 8.0
HALO = 16

MLA_HEADS = 16
MLA_LORA = 512
MLA_NOPE = 128
MLA_ROPE = 64
MLA_V = 128

KV_CHUNK = 512
GQA_STEP_HEADS = 16
MLA_STEP_HEADS = 8
LOG2E = math.log2(math.e)


def _cparams(sem, vmem_mb=VMEM_LIMIT_MB):
    return pltpu.CompilerParams(dimension_semantics=sem,
                                vmem_limit_bytes=vmem_mb * 1024 * 1024)


def _proj_body(*refs, n_ex, epilogue):
    a_ref, w_ref = refs[0], refs[1]
    ex = refs[2:2 + n_ex]
    outs = refs[2 + n_ex:]
    epilogue(jnp.dot(a_ref[...], w_ref[...], preferred_element_type=F32), ex, outs)


def _const_spec(shape):
    zeros = (0,) * len(shape)
    return pl.BlockSpec(shape, lambda i: zeros, pipeline_mode=pl.Buffered(1))


def _proj(a, w, extras, outs, *, tm, epilogue, name=None):
    m, kdim = a.shape
    assert m % tm == 0 and w.shape[0] == kdim
    return pl.pallas_call(
        functools.partial(_proj_body, n_ex=len(extras), epilogue=epilogue),
        grid=(m // tm,),
        in_specs=[pl.BlockSpec((tm, kdim), lambda i: (i, 0)), _const_spec(w.shape)] + [s for _, s in extras],
        out_specs=[s for _, s in outs],
        out_shape=[s for s, _ in outs],
        compiler_params=_cparams(("arbitrary",)),
        name=name,
    )(a, w, *[arr for arr, _ in extras])


def _row_spec(tm, width):
    return pl.BlockSpec((tm, width), lambda i: (i, 0))


def _mix_out_body(*refs, n_a, ngroups, prologue, alpha):
    a_refs = refs[:n_a]
    w_ref, xold_ref, mod_ref, lng_ref, lnb_ref, x_out, h_out = refs[n_a:]
    for g in range(ngroups):
        r0 = g * TILE
        vals = [r[r0:r0 + TILE, :] for r in a_refs]
        a = vals[0] if prologue is None else prologue(vals)
        part = jnp.dot(a, w_ref[...], preferred_element_type=F32)
        _resid_ln_rows(part, r0, xold_ref, mod_ref, lng_ref, lnb_ref, x_out, h_out, alpha)


def _mix_out(a_list, w, x_old, mod, lng, lnb, *, ngroups, out_rows, row_map, prologue, alpha):
    kdim, d = w.shape
    tm = ngroups * TILE
    src = lambda i: (row_map(i), 0)
    dst = pl.BlockSpec((tm, d), lambda i: (i, 0))
    return pl.pallas_call(
        functools.partial(_mix_out_body, n_a=len(a_list), ngroups=ngroups, prologue=prologue, alpha=alpha),
        grid=(out_rows // tm,),
        in_specs=[pl.BlockSpec((tm, kdim), src) for _ in a_list]
        + [_const_spec(w.shape), pl.BlockSpec((tm, d), src),
           pl.BlockSpec((ngroups, 3, d), lambda i: (row_map(i), 0, 0)),
           _const_spec((1, d)), _const_spec((1, d))],
        out_specs=[dst, dst],
        out_shape=[jax.ShapeDtypeStruct((out_rows, d), F32), jax.ShapeDtypeStruct((out_rows, d), BF16)],
        compiler_params=_cparams(("arbitrary",)),
        name="mix_out",
    )(*a_list, w, x_old, mod, lng.reshape(1, d), lnb.reshape(1, d))


def _rope(x, cos, s_hi, s_lo, off):
    return x * cos + pltpu.roll(x, LANE - off, 1) * s_hi + pltpu.roll(x, off, 1) * s_lo


def _ep_headnorm_rope(acc, ex, outs, *, nblk, off, scale):
    g_ref, cos_ref, shi_ref, slo_ref = ex
    g = g_ref[...]
    cos, shi, slo = cos_ref[...], shi_ref[...], slo_ref[...]
    for j in range(nblk):
        x = acc[:, j * LANE:(j + 1) * LANE]
        ms = jnp.mean(x * x, axis=-1, keepdims=True)
        xn = x * lax.rsqrt(ms + EPS) * g
        y = _rope(xn, cos, shi, slo, off)
        if scale != 1.0:
            y = y * scale
        outs[0][:, j * LANE:(j + 1) * LANE] = y.astype(outs[0].dtype)


def _ep_gqa_qkv(acc, ex, outs, *, nq, nkv, off, qscale):
    qg_ref, kg_ref, cos_ref, shi_ref, slo_ref = ex
    q_out, k_out, vt_out = outs
    tabs = (cos_ref, shi_ref, slo_ref)
    _ep_headnorm_rope(acc[:, :nq * LANE], (qg_ref,) + tabs, (q_out,), nblk=nq, off=off, scale=qscale)
    _ep_headnorm_rope(acc[:, nq * LANE:(nq + nkv) * LANE], (kg_ref,) + tabs, (k_out,), nblk=nkv, off=off,
                      scale=1.0)
    vt_out[...] = acc[:, (nq + nkv) * LANE:].T.astype(vt_out.dtype)


def _rms(x, g):
    ms = jnp.mean(x * x, axis=-1, keepdims=True)
    return x * lax.rsqrt(ms + EPS) * g


def _ep_mla_q(acc, ex, outs, *, nheads, off, scale):
    cos_ref, shi_ref, slo_ref = ex
    cos, shi, slo = cos_ref[...], shi_ref[...], slo_ref[...]
    for h in range(nheads):
        c0 = h * 2 * LANE
        outs[0][:, c0:c0 + LANE] = (acc[:, c0:c0 + LANE] * scale).astype(outs[0].dtype)
        pe = acc[:, c0 + LANE:c0 + 2 * LANE]
        outs[0][:, c0 + LANE:c0 + 2 * LANE] = (_rope(pe, cos, shi, slo, off) * scale).astype(outs[0].dtype)


def _ep_mla_kv(acc, ex, outs, *, nheads):
    kpe_ref, = ex
    k_out, vt_out = outs
    kpe = kpe_ref[...]
    for h in range(nheads):
        k_out[:, 2 * h * LANE:(2 * h + 1) * LANE] = acc[:, h * LANE:(h + 1) * LANE].astype(k_out.dtype)
        k_out[:, (2 * h + 1) * LANE:(2 * h + 2) * LANE] = kpe
    vt_out[...] = acc[:, nheads * LANE:2 * nheads * LANE].T.astype(vt_out.dtype)


def _resid_ln_rows(f_out, r0, xold_ref, mod_ref, lng_ref, lnb_ref, x_out, h_out, alpha):
    n = f_out.shape[0]
    g = r0 // TILE
    assert r0 % TILE + n <= TILE
    gate = mod_ref[g, 0:1, :]
    scale = mod_ref[g, 1:2, :]
    shift = mod_ref[g, 2:3, :]
    y = alpha * xold_ref[r0:r0 + n, :] + (1.0 + gate) * f_out
    mu = jnp.mean(y, axis=-1, keepdims=True)
    yc = y - mu
    var = jnp.mean(yc * yc, axis=-1, keepdims=True)
    xn = yc * lax.rsqrt(var + EPS) * lng_ref[...] + lnb_ref[...]
    x_out[r0:r0 + n, :] = xn
    h_out[r0:r0 + n, :] = (xn * (1.0 + scale) + shift).astype(h_out.dtype)


def _pro_lru_gate(vals):
    hf, hb, y = vals
    return ((hf.astype(F32) + hb.astype(F32)) * y.astype(F32)).astype(BF16)


def _mlp_body(h_ref, w1_ref, w2_ref, xold_ref, mod_ref, lng_ref, lnb_ref, x_out, h_out, acc_ref,
              *, nf, ngroups, alpha):
    f = pl.program_id(1)

    def hidden():
        u = jnp.maximum(jnp.dot(h_ref[...], w1_ref[...], preferred_element_type=F32), 0.0)
        return (u * u).astype(BF16)

    @pl.when(f == 0)
    def _():
        acc_ref[...] = jnp.dot(hidden(), w2_ref[...], preferred_element_type=F32)

    @pl.when(jnp.logical_and(f > 0, f < nf - 1))
    def _():
        acc_ref[...] += jnp.dot(hidden(), w2_ref[...], preferred_element_type=F32)

    @pl.when(f == nf - 1)
    def _():
        u2 = hidden()
        for g in range(ngroups):
            r0 = g * TILE
            tot = acc_ref[r0:r0 + TILE, :] + jnp.dot(u2[r0:r0 + TILE, :], w2_ref[...],
                                                   preferred_element_type=F32)
            _resid_ln_rows(tot, r0, xold_ref, mod_ref, lng_ref, lnb_ref, x_out, h_out, alpha)


def _mlp(h, w1, w2, layer, x_old, mod, lng, lnb, *, tm, tf, alpha):
    rows, d = h.shape
    ff = w1.shape[2]
    nf = ff // tf
    assert nf >= 2
    row = pl.BlockSpec((tm, d), lambda i, f: (i, 0))
    vec = pl.BlockSpec((1, d), lambda i, f: (0, 0))
    return pl.pallas_call(
        functools.partial(_mlp_body, nf=nf, ngroups=tm // TILE, alpha=alpha),
        grid=(rows // tm, nf),
        in_specs=[row,
                  pl.BlockSpec((None, d, tf), lambda i, f: (layer, 0, f)),
                  pl.BlockSpec((None, tf, d), lambda i, f: (layer, f, 0)),
                  row,
                  pl.BlockSpec((tm // TILE, 3, d), lambda i, f: (i, 0, 0)),
                  vec, vec],
        out_specs=[row, row],
        out_shape=[jax.ShapeDtypeStruct((rows, d), F32), jax.ShapeDtypeStruct((rows, d), BF16)],
        scratch_shapes=[pltpu.VMEM((tm, d), F32)],
        compiler_params=_cparams(("arbitrary", "arbitrary")),
        name="mlp",
    )(h, w1, w2, x_old, mod, lng.reshape(1, d), lnb.reshape(1, d))


def _mla_proj_body(h_ref, wa_ref, wqb_ref, wkvb_ref, qg_ref, kvg_ref, cos_ref, shi_ref, slo_ref,
                   q_out, k_out, vt_out, *, lora, nheads, off, qscale):
    tabs = (cos_ref, shi_ref, slo_ref)
    t = jnp.dot(h_ref[...], wa_ref[...], preferred_element_type=F32)
    qa = _rms(t[:, 0:lora], qg_ref[...]).astype(BF16)
    ckv = _rms(t[:, lora:2 * lora], kvg_ref[...]).astype(BF16)
    kpe = _rope(t[:, 2 * lora:2 * lora + LANE], cos_ref[...], shi_ref[...], slo_ref[...], off).astype(BF16)
    _ep_mla_q(jnp.dot(qa, wqb_ref[...], preferred_element_type=F32), tabs, (q_out,),
              nheads=nheads, off=off, scale=qscale)
    _ep_mla_kv(jnp.dot(ckv, wkvb_ref[...], preferred_element_type=F32), (kpe,), (k_out, vt_out),
               nheads=nheads)


def _mla_proj(h, w_a, wq_b, wkv_b, q_g, kv_g, tabs, *, batch, n_tiles, qscale):
    rows, d = h.shape
    t_len = n_tiles * TILE
    nq, nkv = wq_b.shape[1], wkv_b.shape[1]
    nv = MLA_HEADS * MLA_V

    const = _const_spec
    tab = pl.BlockSpec((TILE, LANE), lambda i: (i % n_tiles, 0))
    return pl.pallas_call(
        functools.partial(_mla_proj_body, lora=MLA_LORA, nheads=MLA_HEADS, off=MLA_ROPE // 4, qscale=qscale),
        grid=(rows // TILE,),
        in_specs=[pl.BlockSpec((TILE, d), lambda i: (i, 0)),
                  const(w_a.shape), const(wq_b.shape), const(wkv_b.shape),
                  const((1, MLA_LORA)), const((1, MLA_LORA)), tab, tab, tab],
        out_specs=[pl.BlockSpec((TILE, nq), lambda i: (i, 0)),
                   pl.BlockSpec((TILE, nq), lambda i: (i, 0)),
                   pl.BlockSpec((None, nv, TILE), lambda i: (i // n_tiles, 0, i % n_tiles))],
        out_shape=[jax.ShapeDtypeStruct((rows, nq), BF16), jax.ShapeDtypeStruct((rows, nq), BF16),
                   jax.ShapeDtypeStruct((batch, nv, t_len), BF16)],
        compiler_params=_cparams(("arbitrary",)),
        name="mla_proj",
    )(h, w_a, wq_b, wkv_b, q_g.reshape(1, MLA_LORA), kv_g.reshape(1, MLA_LORA), *tabs)


def _ada_body(c_ref, w_ref, b_ref, o_ref):
    cond = c_ref[...]
    sc = (cond * jax.nn.sigmoid(cond)).astype(BF16)
    o_ref[0] = jnp.dot(sc, w_ref[0].astype(BF16), preferred_element_type=F32) + b_ref[0]


def _ada_mods(cond, ada_w, ada_b, tn=1024):
    depth, d, n = ada_w.shape
    rows = cond.shape[0]
    return pl.pallas_call(
        _ada_body,
        grid=(depth, n // tn),
        in_specs=[pl.BlockSpec((rows, d), lambda l, j: (0, 0)),
                  pl.BlockSpec((1, d, tn), lambda l, j: (l, 0, j)),
                  pl.BlockSpec((1, 1, tn), lambda l, j: (l, 0, j))],
        out_specs=pl.BlockSpec((1, rows, tn), lambda l, j: (l, 0, j)),
        out_shape=jax.ShapeDtypeStruct((depth, rows, n), F32),
        compiler_params=_cparams(("arbitrary", "arbitrary")),
        name="ada_mods",
    )(cond, ada_w, ada_b.reshape(depth, 1, n))


def _modulate_body(ctx_ref, x_ref, mod_ref, xs_ref, h_ref):
    j = pl.program_id(1)

    def emit(src):
        xs_ref[...] = src
        h_ref[...] = (src * (1.0 + mod_ref[0, 1:2, :]) + mod_ref[0, 0:1, :]).astype(h_ref.dtype)

    @pl.when(j == 0)
    def _():
        emit(ctx_ref[...])

    @pl.when(j > 0)
    def _():
        emit(x_ref[...])


def _modulate(x, ctx, mod):
    batch, n_lat, d = x.shape
    n_tiles = n_lat // TILE + 1
    rows = batch * n_tiles * TILE
    row = pl.BlockSpec((TILE, d), lambda b, j: (b * n_tiles + j, 0))
    return pl.pallas_call(
        _modulate_body,
        grid=(batch, n_tiles),
        in_specs=[pl.BlockSpec((None, TILE, d), lambda b, j: (b, 0, 0)),
                  pl.BlockSpec((None, TILE, d), lambda b, j: (b, jnp.maximum(j - 1, 0), 0)),
                  pl.BlockSpec((1, mod.shape[1], d), lambda b, j: (b * n_tiles + j, 0, 0))],
        out_specs=[row, row],
        out_shape=[jax.ShapeDtypeStruct((rows, d), F32), jax.ShapeDtypeStruct((rows, d), BF16)],
        compiler_params=_cparams(("arbitrary", "arbitrary")),
        name="modulate",
    )(ctx, x, mod)


def _dot_nt(a, b):
    return lax.dot_general(a, b, (((1,), (1,)), ((), ())), preferred_element_type=F32)


def _attn_body(q_ref, k_ref, vt_ref, o_ref, s_ref, p_ref, *, group, dq, dv, q_per_kv, n_lat_chunks):
    i = pl.program_id(2)

    def kcol(g):
        return (g // q_per_kv) * dq

    def vrow(g):
        return (g // q_per_kv) * dv

    def finish(g, acc, l8):
        l = jnp.sum(l8, axis=0, keepdims=True)
        out = acc * (1.0 / l)
        o_ref[:, g * dv:(g + 1) * dv] = out.T.astype(o_ref.dtype)

    def colmax8(x):
        return jnp.max(x.reshape(x.shape[0] // SUBLANE, SUBLANE, x.shape[1]), axis=0)

    def colsum8(x):
        return jnp.sum(x.reshape(x.shape[0] // SUBLANE, SUBLANE, x.shape[1]), axis=0)

    @pl.when(i == 0)
    def _():
        for g in range(group):
            st = _dot_nt(k_ref[0:TILE, kcol(g):kcol(g) + dq], q_ref[:, g * dq:(g + 1) * dq])
            m = jnp.max(st, axis=0, keepdims=True)
            p = jnp.exp2(st - m)
            acc = jnp.dot(vt_ref[vrow(g):vrow(g) + dv, 0:TILE], p.astype(BF16), preferred_element_type=F32)
            finish(g, acc, colsum8(p))

    @pl.when(i > 0)
    def _():
        neg = jnp.full((SUBLANE, TILE), -jnp.inf, F32)
        zero8 = jnp.zeros((SUBLANE, TILE), F32)
        zacc = jnp.zeros((dv, TILE), F32)
        state = {}
        for ph in range(group + 2):
            ga, gb, gc = ph, ph - 1, ph - 2
            do_a, do_b, do_c = ga < group, 0 <= gb < group, 0 <= gc < group
            q_a = q_ref[:, ga * dq:(ga + 1) * dq] if do_a else None
            m_b = jnp.max(state["m8"], axis=0, keepdims=True) if do_b else None

            def stage(start, size, carry, q_a=q_a, m_b=m_b, ga=ga, gb=gb, gc=gc,
                      do_a=do_a, do_b=do_b, do_c=do_c):
                m8, l8, acc = carry

                def value_product(acc):
                    return acc + jnp.dot(vt_ref[vrow(gc):vrow(gc) + dv, pl.ds(start, size)],
                                         p_ref[gc % 2, pl.ds(start, size), :], preferred_element_type=F32)

                if do_c and q_per_kv > 1:
                    acc = value_product(acc)
                if do_a:
                    st = _dot_nt(k_ref[pl.ds(start, size), kcol(ga):kcol(ga) + dq], q_a)
                    s_ref[ga % 2, pl.ds(start, size), :] = st
                    m8 = jnp.maximum(m8, colmax8(st))
                if do_c and q_per_kv == 1:
                    acc = value_product(acc)
                if do_b:
                    p = jnp.exp2(s_ref[gb % 2, pl.ds(start, size), :] - m_b)
                    p_ref[gb % 2, pl.ds(start, size), :] = p.astype(BF16)
                    l8 = l8 + colsum8(p)
                return m8, l8, acc

            carry = stage(0, TILE, (neg, zero8, zacc))

            def body(c, carry, stage=stage):
                return stage(pl.multiple_of(TILE + c * KV_CHUNK, TILE), KV_CHUNK, carry)

            m8, l8, acc = lax.fori_loop(0, n_lat_chunks, body, carry, unroll=True)
            if do_c:
                finish(gc, acc, state["l8"])
            if do_b:
                state["l8"] = l8
            if do_a:
                state["m8"] = m8


def _attention(q, k, vt, *, batch, n_tiles, n_steps, group, dq, dv, q_per_kv):
    r = q.shape[0]
    t_len = n_tiles * TILE
    kvw = group // q_per_kv
    return pl.pallas_call(
        functools.partial(_attn_body, group=group, dq=dq, dv=dv, q_per_kv=q_per_kv,
                          n_lat_chunks=(t_len - TILE) // KV_CHUNK),
        grid=(batch, n_steps, n_tiles),
        in_specs=[pl.BlockSpec((TILE, group * dq), lambda b, h, i: (b * n_tiles + i, h)),
                  pl.BlockSpec((t_len, kvw * dq), lambda b, h, i: (b, h), pipeline_mode=pl.Buffered(1)),
                  pl.BlockSpec((None, kvw * dv, t_len), lambda b, h, i: (b, h, 0), pipeline_mode=pl.Buffered(1))],
        out_specs=pl.BlockSpec((TILE, group * dv), lambda b, h, i: (b * n_tiles + i, h)),
        out_shape=jax.ShapeDtypeStruct((r, n_steps * group * dv), BF16),
        scratch_shapes=[pltpu.VMEM((2, t_len, TILE), F32), pltpu.VMEM((2, t_len, TILE), BF16)],
        compiler_params=_cparams(("arbitrary", "arbitrary", "arbitrary")),
        name="attention",
    )(q, k, vt)


def _lru_in_body(h_ref, prev_ref, next_ref, w_ref, cw_ref, cb_ref, u_ref, y_ref, *, n_tiles, ngroups):
    i = pl.program_id(0)
    width = u_ref.shape[1]
    h_ext = jnp.concatenate([prev_ref[...], h_ref[...], next_ref[...]], axis=0)
    xw = jnp.dot(h_ext, w_ref[:, 0:width], preferred_element_type=F32)
    left = CONV_W // 2
    for g in range(ngroups):
        seg = (i * ngroups + g) % n_tiles
        has_prev = seg > 1
        has_next = jnp.logical_and(seg > 0, seg < n_tiles - 1)
        o = HALO + g * TILE
        ext = jnp.concatenate([jnp.where(has_prev, xw[o - SUBLANE:o, :], 0.0), xw[o:o + TILE, :],
                               jnp.where(has_next, xw[o + TILE:o + TILE + SUBLANE, :], 0.0)], axis=0)
        n = ext.shape[0]
        acc = cb_ref[...]
        for j in range(CONV_W):
            shift = left - j
            rows = ext if shift == 0 else pltpu.roll(ext, shift % n, 0)
            acc = acc + rows[SUBLANE:SUBLANE + TILE, :] * cw_ref[j:j + 1, :]
        u_ref[g * TILE:(g + 1) * TILE, :] = acc
    y = jnp.dot(h_ref[...], w_ref[:, width:2 * width], preferred_element_type=F32)
    y_ref[...] = jax.nn.gelu(y, approximate=True).astype(y_ref.dtype)


def _lru_in(h, wxy, conv_w, conv_b, *, n_tiles, ngroups):
    r, d = h.shape
    width = wxy.shape[1] // 2
    tm = ngroups * TILE
    per = tm // HALO
    n_halo = r // HALO
    row = pl.BlockSpec((tm, width), lambda i: (i, 0))
    return pl.pallas_call(
        functools.partial(_lru_in_body, n_tiles=n_tiles, ngroups=ngroups),
        grid=(r // tm,),
        in_specs=[pl.BlockSpec((tm, d), lambda i: (i, 0)),
                  pl.BlockSpec((HALO, d), lambda i: (jnp.maximum(i * per - 1, 0), 0)),
                  pl.BlockSpec((HALO, d), lambda i: (jnp.minimum((i + 1) * per, n_halo - 1), 0)),
                  _const_spec(wxy.shape), _const_spec((CONV_W, width)), _const_spec((1, width))],
        out_specs=[row, row],
        out_shape=[jax.ShapeDtypeStruct((r, width), F32), jax.ShapeDtypeStruct((r, width), BF16)],
        compiler_params=_cparams(("arbitrary",)),
        name="lru_in",
    )(h, h, h, wxy, conv_w, conv_b.reshape(1, width))


def _log_sigmoid(x):
    return jnp.minimum(x, 0.0) - jnp.log1p(jnp.exp(-jnp.abs(x)))


def _scan_tile(a, b, carry, reverse):
    n, w = a.shape
    groups = n // SUBLANE
    a3 = a.reshape(groups, SUBLANE, w)
    b3 = b.reshape(groups, SUBLANE, w)
    sub = lax.broadcasted_iota(jnp.int32, (groups, SUBLANE, w), 1)
    s = 1
    while s < SUBLANE:
        shift = SUBLANE - s if reverse else s
        valid = (sub < SUBLANE - s) if reverse else (sub >= s)
        a_sh = pltpu.roll(a3, shift, 1)
        b_sh = pltpu.roll(b3, shift, 1)
        b3 = jnp.where(valid, b3 + a3 * b_sh, b3)
        a3 = jnp.where(valid, a3 * a_sh, a3)
        s *= 2
    outs = [None] * groups
    for r in (range(groups - 1, -1, -1) if reverse else range(groups)):
        hv = b3[r] + a3[r] * carry
        outs[r] = hv
        edge = hv[0:1, :] if reverse else hv[SUBLANE - 1:SUBLANE, :]
        carry = jnp.broadcast_to(edge, (SUBLANE, w))
    return jnp.concatenate(outs, axis=0), carry


def _lru_body(uf_ref, ub_ref, raw_ref, ixw_ref, rab_ref, ixb_ref, lam_ref,
              hf_ref, hb_ref, carry_ref, *, blk):
    j = pl.program_id(1)

    @pl.when(j == 0)
    def _():
        carry_ref[...] = jnp.zeros_like(carry_ref)

    for d, (u_ref, h_ref) in enumerate(((uf_ref, hf_ref), (ub_ref, hb_ref))):
        half_l2 = (0.5 * LRU_C * LOG2E) * _log_sigmoid(lam_ref[d])
        for n in range(LRU_BLOCKS):
            cs = slice(n * blk, (n + 1) * blk)
            u = u_ref[:, cs]
            ub16 = u.astype(BF16)
            t_r = jnp.tanh(jnp.dot(ub16, raw_ref[d, n], preferred_element_type=F32) + rab_ref[d][:, cs])
            t_g = jnp.tanh(jnp.dot(ub16, ixw_ref[d, n], preferred_element_type=F32) + ixb_ref[d][:, cs])
            a = jnp.exp2((t_r + 1.0) * half_l2[:, cs])
            om = 1.0 - a * a
            root = jnp.where(om > 0.0, om * lax.rsqrt(om), 0.0)
            bx = root * ((t_g + 1.0) * (0.5 * u))
            h, carry = _scan_tile(a, bx, carry_ref[d, :, cs], reverse=(d == 1))
            h_ref[:, cs] = h.astype(h_ref.dtype)
            carry_ref[d, :, cs] = carry


def _lru_scan(u, ra_w, ix_w, ra_b, ix_b, lam, *, batch, n_tiles):
    r, w = u.shape
    blk = w // LRU_BLOCKS

    def fwd_map(b, j):
        return (b * n_tiles + j, 0)

    def bwd_map(b, j):
        return (b * n_tiles + jnp.where(j == 0, 0, n_tiles - j), 0)

    full5 = pl.BlockSpec((2, LRU_BLOCKS, blk, blk), lambda b, j: (0, 0, 0, 0))
    vec = pl.BlockSpec((2, 1, w), lambda b, j: (0, 0, 0))
    return pl.pallas_call(
        functools.partial(_lru_body, blk=blk),
        grid=(batch, n_tiles),
        in_specs=[pl.BlockSpec((TILE, w), fwd_map), pl.BlockSpec((TILE, w), bwd_map),
                  full5, full5, vec, vec, vec],
        out_specs=[pl.BlockSpec((TILE, w), fwd_map), pl.BlockSpec((TILE, w), bwd_map)],
        out_shape=[jax.ShapeDtypeStruct((r, w), BF16), jax.ShapeDtypeStruct((r, w), BF16)],
        scratch_shapes=[pltpu.VMEM((2, SUBLANE, w), F32)],
        compiler_params=_cparams(("arbitrary", "arbitrary")),
        name="lru_scan",
    )(u, u, ra_w, ix_w, ra_b.reshape(2, 1, w), ix_b.reshape(2, 1, w), lam.reshape(2, 1, w))


def _rope_tables(n_lat, rot_dim):
    m = rot_dim // 2
    half = m // 2
    t = np.arange(n_lat)
    pos = np.stack([t // GRID_W, t % GRID_W], axis=1).astype(np.float32)
    inv = (ROPE_THETA ** (-(np.arange(half, dtype=np.float32) * 2.0) / m)).astype(np.float32)
    lane = np.arange(LANE)
    axis = np.minimum(lane // m, 1)
    freq = inv[lane % half]
    ang = pos[:, axis] * freq[None, :]
    live = (lane < rot_dim)[None, :]
    first = ((lane % m) < half)[None, :]
    cos = np.where(live, np.cos(ang), 1.0)
    sin = np.where(live, np.sin(ang), 0.0)
    s_hi = np.where(first, -sin, 0.0)
    s_lo = np.where(first, 0.0, sin)
    ident = np.concatenate([np.ones((TILE, LANE)), np.zeros((TILE, LANE)), np.zeros((TILE, LANE))], 1)
    tab = np.concatenate([cos, s_hi, s_lo], axis=1)
    tab = np.concatenate([ident, tab], axis=0).astype(np.float32)
    return (jnp.asarray(tab[:, :LANE]), jnp.asarray(tab[:, LANE:2 * LANE]), jnp.asarray(tab[:, 2 * LANE:]))


def kernel(x, c, ctx, c_ctx, ada_w, ada_b, ln_g, ln_b, mlp_w1, mlp_w2, gqa_wq, gqa_wk, gqa_wv, gqa_wo, gqa_q_g, gqa_k_g, lru_wx, lru_wy, lru_conv_w, lru_conv_b, lru_ra_w, lru_ra_b, lru_ix_w, lru_ix_b, lru_lam, lru_wo, mla_wq_a, mla_q_a_g, mla_wq_b, mla_wkv_a, mla_kv_a_g, mla_wkv_b, mla_wo):
    batch, n_lat, d = x.shape
    depth = ada_w.shape[0]
    assert ctx.shape[1] == TILE and n_lat % KV_CHUNK == 0 and n_lat % GRID_W == 0
    t_len = TILE + n_lat
    n_tiles = t_len // TILE
    rows = batch * t_len
    n_groups = rows // TILE
    alpha = (2 * depth) ** 0.25


    cond_rows = -(-(batch + 1) // SUBLANE) * SUBLANE
    cond = jnp.concatenate([c, c_ctx[None, :], jnp.zeros((cond_rows - batch - 1, d), F32)], axis=0)
    mods = _ada_mods(cond, ada_w, ada_b)
    m_lat = jnp.broadcast_to(mods[:, :batch, None, :], (depth, batch, n_tiles - 1, 6 * d))
    m_ctx = jnp.broadcast_to(mods[:, batch:batch + 1, None, :], (depth, batch, 1, 6 * d))
    mods = jnp.concatenate([m_ctx, m_lat], axis=2).reshape(depth, n_groups, 6, d)

    def mod3(gate, scale, shift):
        return jnp.stack([gate, scale, shift], axis=1)

    tabs_gqa = _rope_tables(n_lat, A_HEAD_DIM)
    tabs_mla = _rope_tables(n_lat, MLA_ROPE)

    def tab_specs():
        return [pl.BlockSpec((TILE, LANE), lambda i: (i % n_tiles, 0)) for _ in range(3)]

    def vt_spec(n):
        return pl.BlockSpec((None, n, TILE), lambda i: (i // n_tiles, 0, i % n_tiles))

    def lat_tile(i):
        return i + i // (n_tiles - 1) + 1

    xs, h = _modulate(x, ctx, mods[0])
    w1_all = mlp_w1.astype(BF16)
    w2_all = mlp_w2.astype(BF16)

    for i in range(depth):
        kind = i % 3
        slot = i // 3
        m_i = mods[i]
        if kind == 0:
            wq, wk, wv, wo = (gqa_wq[slot].astype(BF16), gqa_wk[slot].astype(BF16),
                              gqa_wv[slot].astype(BF16), gqa_wo[slot].astype(BF16))

            nq, nkv = wq.shape[1], wk.shape[1]
            wqkv = jnp.concatenate([wq, wk, wv], axis=1)
            ex = [(gqa_q_g[slot].reshape(1, LANE), _const_spec((1, LANE))),
                  (gqa_k_g[slot].reshape(1, LANE), _const_spec((1, LANE)))] + list(zip(tabs_gqa, tab_specs()))
            outs = [(jax.ShapeDtypeStruct((rows, nq), BF16), _row_spec(TILE, nq)),
                    (jax.ShapeDtypeStruct((rows, nkv), BF16), _row_spec(TILE, nkv)),
                    (jax.ShapeDtypeStruct((batch, nkv, t_len), BF16), vt_spec(nkv))]
            q, k, vt = _proj(h, wqkv, ex, outs, tm=TILE, name="gqa_qkv",
                             epilogue=functools.partial(_ep_gqa_qkv, nq=nq // LANE, nkv=nkv // LANE,
                                                        off=A_HEAD_DIM // 4,
                                                        qscale=A_HEAD_DIM ** -0.5 * LOG2E))
            o = _attention(q, k, vt, batch=batch, n_tiles=n_tiles, n_steps=A_HEADS // GQA_STEP_HEADS,
                           group=GQA_STEP_HEADS, dq=A_HEAD_DIM, dv=A_HEAD_DIM, q_per_kv=A_GROUP)
            a_list, w_out, prologue = [o], wo, None
        elif kind == 1:
            wx, wy, wo = lru_wx[slot].astype(BF16), lru_wy[slot].astype(BF16), lru_wo[slot].astype(BF16)
            u, y = _lru_in(h, jnp.concatenate([wx, wy], axis=1), lru_conv_w[slot], lru_conv_b[slot],
                           n_tiles=n_tiles, ngroups=2)
            hf, hb = _lru_scan(u, (0.5 * lru_ra_w[slot]).astype(BF16), (0.5 * lru_ix_w[slot]).astype(BF16),
                               0.5 * lru_ra_b[slot], 0.5 * lru_ix_b[slot], lru_lam[slot],
                               batch=batch, n_tiles=n_tiles)
            a_list, w_out, prologue = [hf, hb, y], wo, _pro_lru_gate
        else:
            nq = MLA_NOPE + MLA_ROPE
            wq_a = mla_wq_a[slot].astype(BF16)
            wq_b = mla_wq_b[slot].reshape(MLA_LORA, MLA_HEADS, nq)
            wq_b = jnp.concatenate([wq_b, jnp.zeros((MLA_LORA, MLA_HEADS, 2 * LANE - nq), F32)], axis=-1)
            wq_b = wq_b.reshape(MLA_LORA, MLA_HEADS * 2 * LANE).astype(BF16)
            wkv_a = jnp.concatenate([mla_wkv_a[slot], jnp.zeros((d, LANE - MLA_ROPE), F32)], axis=-1).astype(BF16)
            wkv_b = mla_wkv_b[slot].reshape(MLA_LORA, MLA_HEADS, MLA_NOPE + MLA_V)
            wkv_b = jnp.concatenate([wkv_b[:, :, :MLA_NOPE].reshape(MLA_LORA, -1),
                                     wkv_b[:, :, MLA_NOPE:].reshape(MLA_LORA, -1)], axis=-1).astype(BF16)
            wo = mla_wo[slot].astype(BF16)

            q, kfull, vt = _mla_proj(h, jnp.concatenate([wq_a, wkv_a], axis=1), wq_b, wkv_b,
                                     mla_q_a_g[slot], mla_kv_a_g[slot], tabs_mla,
                                     batch=batch, n_tiles=n_tiles, qscale=float(nq) ** -0.5 * LOG2E)
            o = _attention(q, kfull, vt, batch=batch, n_tiles=n_tiles, n_steps=MLA_HEADS // MLA_STEP_HEADS,
                           group=MLA_STEP_HEADS, dq=2 * LANE, dv=MLA_V, q_per_kv=1)
            a_list, w_out, prologue = [o], wo, None

        last = i == depth - 1
        xs, h2 = _mix_out(a_list, w_out, xs, mod3(m_i[:, 2], m_i[:, 4], m_i[:, 3]), ln_g[i, 0], ln_b[i, 0],
                          ngroups=1 if last else 2, out_rows=batch * n_lat if last else rows,
                          row_map=lat_tile if last else (lambda t: t), prologue=prologue, alpha=alpha)
        if not last:
            nxt = mod3(m_i[:, 5], mods[i + 1][:, 1], mods[i + 1][:, 0])
        else:
            zero = jnp.zeros_like(m_i[:, 5])
            nxt = mod3(m_i[:, 5], zero, zero)
            nxt = nxt.reshape(batch, n_tiles, 3, d)[:, 1:].reshape(batch * (n_tiles - 1), 3, d)
        xs, h = _mlp(h2, w1_all, w2_all, i, xs, nxt, ln_g[i, 1], ln_b[i, 1], tm=2 * TILE, tf=1024,
                     alpha=alpha)

    return xs.reshape(batch, n_lat, d)
```

```python
import functools
import math

import numpy as np
import jax
import jax.numpy as jnp
from jax import lax
from jax.experimental import pallas as pl
from jax.experimental.pallas import tpu as pltpu

F32 = jnp.float32
BF16 = jnp.bfloat16

TILE = 256
GRID_W = 64
ROPE_THETA = 10000.0
EPS = 1e-6
LANE = 128
SUBLANE = 8
VMEM_LIMIT_MB = 56

A_HEADS = 16
A_KV_HEADS = 4
A_GROUP = A_HEADS // A_KV_HEADS
A_HEAD_DIM = 128

LRU_BLOCKS = 8
CONV_W = 4
LRU_C = 8.0
HALO = 16

MLA_HEADS = 16
MLA_LORA = 512
MLA_NOPE = 128
MLA_ROPE = 64
MLA_V = 128

KV_CHUNK = 1024
GQA_STEP_HEADS = 16
MLA_STEP_HEADS = 8
LOG2E = math.log2(math.e)


def _cparams(sem, vmem_mb=VMEM_LIMIT_MB):
    return pltpu.CompilerParams(dimension_semantics=sem,
                                vmem_limit_bytes=vmem_mb * 1024 * 1024)


def _proj_body(*refs, n_ex, epilogue):
    a_ref, w_ref = refs[0], refs[1]
    ex = refs[2:2 + n_ex]
    outs = refs[2 + n_ex:]
    epilogue(jnp.dot(a_ref[...], w_ref[...], preferred_element_type=F32), ex, outs)


def _const_spec(shape):
    zeros = (0,) * len(shape)
    return pl.BlockSpec(shape, lambda i: zeros, pipeline_mode=pl.Buffered(1))


def _proj(a, w, extras, outs, *, tm, epilogue, name=None):
    m, kdim = a.shape
    assert m % tm == 0 and w.shape[0] == kdim
    return pl.pallas_call(
        functools.partial(_proj_body, n_ex=len(extras), epilogue=epilogue),
        grid=(m // tm,),
        in_specs=[pl.BlockSpec((tm, kdim), lambda i: (i, 0)), _const_spec(w.shape)] + [s for _, s in extras],
        out_specs=[s for _, s in outs],
        out_shape=[s for s, _ in outs],
        compiler_params=_cparams(("arbitrary",)),
        name=name,
    )(a, w, *[arr for arr, _ in extras])


def _row_spec(tm, width):
    return pl.BlockSpec((tm, width), lambda i: (i, 0))


def _mix_out_body(*refs, n_a, ngroups, prologue, alpha):
    a_refs = refs[:n_a]
    w_ref, xold_ref, mod_ref, lng_ref, lnb_ref, x_out, h_out = refs[n_a:]
    for g in range(ngroups):
        r0 = g * TILE
        vals = [r[r0:r0 + TILE, :] for r in a_refs]
        a = vals[0] if prologue is None else prologue(vals)
        part = jnp.dot(a, w_ref[...], preferred_element_type=F32)
        _resid_ln_rows(part, r0, xold_ref, mod_ref, lng_ref, lnb_ref, x_out, h_out, alpha)


def _mix_out(a_list, w, x_old, mod, lng, lnb, *, ngroups, out_rows, row_map, prologue, alpha):
    kdim, d = w.shape
    tm = ngroups * TILE
    assert out_rows % tm == 0
    src = lambda i: (row_map(i), 0)
    dst = pl.BlockSpec((tm, d), lambda i: (i, 0))
    return pl.pallas_call(
        functools.partial(_mix_out_body, n_a=len(a_list), ngroups=ngroups, prologue=prologue, alpha=alpha),
        grid=(out_rows // tm,),
        in_specs=[pl.BlockSpec((tm, kdim), src) for _ in a_list]
        + [_const_spec(w.shape), pl.BlockSpec((tm, d), src),
           pl.BlockSpec((ngroups, 3, d), lambda i: (row_map(i), 0, 0)),
           _const_spec((1, d)), _const_spec((1, d))],
        out_specs=[dst, dst],
        out_shape=[jax.ShapeDtypeStruct((out_rows, d), F32), jax.ShapeDtypeStruct((out_rows, d), BF16)],
        compiler_params=_cparams(("arbitrary",)),
        name="mix_out",
    )(*a_list, w, x_old, mod, lng.reshape(1, d), lnb.reshape(1, d))


def _rope(x, cos, s_hi, s_lo, off):
    return x * cos + pltpu.roll(x, LANE - off, 1) * s_hi + pltpu.roll(x, off, 1) * s_lo


def _ep_headnorm_rope(acc, ex, outs, *, nblk, off, scale):
    g_ref, cos_ref, shi_ref, slo_ref = ex
    g = g_ref[...]
    cos, shi, slo = cos_ref[...], shi_ref[...], slo_ref[...]
    for j in range(nblk):
        x = acc[:, j * LANE:(j + 1) * LANE]
        ms = jnp.mean(x * x, axis=-1, keepdims=True)
        xn = x * lax.rsqrt(ms + EPS) * g
        y = _rope(xn, cos, shi, slo, off)
        if scale != 1.0:
            y = y * scale
        outs[0][:, j * LANE:(j + 1) * LANE] = y.astype(outs[0].dtype)


def _ep_gqa_qkv(acc, ex, outs, *, nq, nkv, off, qscale):
    qg_ref, kg_ref, cos_ref, shi_ref, slo_ref = ex
    q_out, k_out, vt_out = outs
    tabs = (cos_ref, shi_ref, slo_ref)
    _ep_headnorm_rope(acc[:, :nq * LANE], (qg_ref,) + tabs, (q_out,), nblk=nq, off=off, scale=qscale)
    _ep_headnorm_rope(acc[:, nq * LANE:(nq + nkv) * LANE], (kg_ref,) + tabs, (k_out,), nblk=nkv, off=off,
                      scale=1.0)
    vt_out[...] = acc[:, (nq + nkv) * LANE:].T.astype(vt_out.dtype)


def _rms(x, g):
    ms = jnp.mean(x * x, axis=-1, keepdims=True)
    return x * lax.rsqrt(ms + EPS) * g


def _ep_mla_q(acc, ex, outs, *, nheads, off, scale):
    cos_ref, shi_ref, slo_ref = ex
    cos, shi, slo = cos_ref[...], shi_ref[...], slo_ref[...]
    for h in range(nheads):
        c0 = h * 2 * LANE
        outs[0][:, c0:c0 + LANE] = (acc[:, c0:c0 + LANE] * scale).astype(outs[0].dtype)
        pe = acc[:, c0 + LANE:c0 + 2 * LANE]
        outs[0][:, c0 + LANE:c0 + 2 * LANE] = (_rope(pe, cos, shi, slo, off) * scale).astype(outs[0].dtype)


def _ep_mla_kv(acc, ex, outs, *, nheads):
    kpe_ref, = ex
    k_out, vt_out = outs
    kpe = kpe_ref[...]
    for h in range(nheads):
        k_out[:, 2 * h * LANE:(2 * h + 1) * LANE] = acc[:, h * LANE:(h + 1) * LANE].astype(k_out.dtype)
        k_out[:, (2 * h + 1) * LANE:(2 * h + 2) * LANE] = kpe
    vt_out[...] = acc[:, nheads * LANE:2 * nheads * LANE].T.astype(vt_out.dtype)


def _resid_ln_rows(f_out, r0, xold_ref, mod_ref, lng_ref, lnb_ref, x_out, h_out, alpha):
    n = f_out.shape[0]
    g = r0 // TILE
    assert r0 % TILE + n <= TILE
    gate = mod_ref[g, 0:1, :]
    scale = mod_ref[g, 1:2, :]
    shift = mod_ref[g, 2:3, :]
    y = alpha * xold_ref[r0:r0 + n, :] + (1.0 + gate) * f_out
    mu = jnp.mean(y, axis=-1, keepdims=True)
    yc = y - mu
    var = jnp.mean(yc * yc, axis=-1, keepdims=True)
    xn = yc * lax.rsqrt(var + EPS) * lng_ref[...] + lnb_ref[...]
    x_out[r0:r0 + n, :] = xn
    h_out[r0:r0 + n, :] = (xn * (1.0 + scale) + shift).astype(h_out.dtype)


def _pro_lru_gate(vals):
    hf, hb, y = vals
    return ((hf.astype(F32) + hb.astype(F32)) * y.astype(F32)).astype(BF16)


def _mlp_body(h_ref, w1_ref, w2_ref, xold_ref, mod_ref, lng_ref, lnb_ref, x_out, h_out, acc_ref,
              *, nf, ngroups, alpha):
    f = pl.program_id(1)

    def hidden():
        u = jnp.maximum(jnp.dot(h_ref[...], w1_ref[...], preferred_element_type=F32), 0.0)
        return (u * u).astype(BF16)

    @pl.when(f == 0)
    def _():
        acc_ref[...] = jnp.dot(hidden(), w2_ref[...], preferred_element_type=F32)

    @pl.when(jnp.logical_and(f > 0, f < nf - 1))
    def _():
        acc_ref[...] += jnp.dot(hidden(), w2_ref[...], preferred_element_type=F32)

    @pl.when(f == nf - 1)
    def _():
        u2 = hidden()
        for g in range(ngroups):
            r0 = g * TILE
            tot = acc_ref[r0:r0 + TILE, :] + jnp.dot(u2[r0:r0 + TILE, :], w2_ref[...],
                                                   preferred_element_type=F32)
            _resid_ln_rows(tot, r0, xold_ref, mod_ref, lng_ref, lnb_ref, x_out, h_out, alpha)


def _mlp(h, w1, w2, layer, x_old, mod, lng, lnb, *, tm, tf, alpha):
    rows, d = h.shape
    ff = w1.shape[2]
    nf = ff // tf
    assert rows % tm == 0 and ff % tf == 0
    assert nf >= 2
    row = pl.BlockSpec((tm, d), lambda i, f: (i, 0))
    vec = pl.BlockSpec((1, d), lambda i, f: (0, 0))
    return pl.pallas_call(
        functools.partial(_mlp_body, nf=nf, ngroups=tm // TILE, alpha=alpha),
        grid=(rows // tm, nf),
        in_specs=[row,
                  pl.BlockSpec((None, d, tf), lambda i, f: (layer, 0, f)),
                  pl.BlockSpec((None, tf, d), lambda i, f: (layer, f, 0)),
                  row,
                  pl.BlockSpec((tm // TILE, 3, d), lambda i, f: (i, 0, 0)),
                  vec, vec],
        out_specs=[row, row],
        out_shape=[jax.ShapeDtypeStruct((rows, d), F32), jax.ShapeDtypeStruct((rows, d), BF16)],
        scratch_shapes=[pltpu.VMEM((tm, d), F32)],
        compiler_params=_cparams(("arbitrary", "arbitrary")),
        name="mlp",
    )(h, w1, w2, x_old, mod, lng.reshape(1, d), lnb.reshape(1, d))


def _mla_proj_body(h_ref, wa_ref, wqb_ref, wkvb_ref, qg_ref, kvg_ref, cos_ref, shi_ref, slo_ref,
                   q_out, k_out, vt_out, *, lora, nheads, off, qscale):
    tabs = (cos_ref, shi_ref, slo_ref)
    t = jnp.dot(h_ref[...], wa_ref[...], preferred_element_type=F32)
    qa = _rms(t[:, 0:lora], qg_ref[...]).astype(BF16)
    ckv = _rms(t[:, lora:2 * lora], kvg_ref[...]).astype(BF16)
    kpe = _rope(t[:, 2 * lora:2 * lora + LANE], cos_ref[...], shi_ref[...], slo_ref[...], off).astype(BF16)
    _ep_mla_q(jnp.dot(qa, wqb_ref[...], preferred_element_type=F32), tabs, (q_out,),
              nheads=nheads, off=off, scale=qscale)
    _ep_mla_kv(jnp.dot(ckv, wkvb_ref[...], preferred_element_type=F32), (kpe,), (k_out, vt_out),
               nheads=nheads)


def _mla_proj(h, w_a, wq_b, wkv_b, q_g, kv_g, tabs, *, batch, n_tiles, qscale):
    rows, d = h.shape
    t_len = n_tiles * TILE
    nq, nkv = wq_b.shape[1], wkv_b.shape[1]
    nv = MLA_HEADS * MLA_V

    const = _const_spec
    tab = pl.BlockSpec((TILE, LANE), lambda i: (i % n_tiles, 0))
    return pl.pallas_call(
        functools.partial(_mla_proj_body, lora=MLA_LORA, nheads=MLA_HEADS, off=MLA_ROPE // 4, qscale=qscale),
        grid=(rows // TILE,),
        in_specs=[pl.BlockSpec((TILE, d), lambda i: (i, 0)),
                  const(w_a.shape), const(wq_b.shape), const(wkv_b.shape),
                  const((1, MLA_LORA)), const((1, MLA_LORA)), tab, tab, tab],
        out_specs=[pl.BlockSpec((TILE, nq), lambda i: (i, 0)),
                   pl.BlockSpec((TILE, nq), lambda i: (i, 0)),
                   pl.BlockSpec((None, nv, TILE), lambda i: (i // n_tiles, 0, i % n_tiles))],
        out_shape=[jax.ShapeDtypeStruct((rows, nq), BF16), jax.ShapeDtypeStruct((rows, nq), BF16),
                   jax.ShapeDtypeStruct((batch, nv, t_len), BF16)],
        compiler_params=_cparams(("arbitrary",)),
        name="mla_proj",
    )(h, w_a, wq_b, wkv_b, q_g.reshape(1, MLA_LORA), kv_g.reshape(1, MLA_LORA), *tabs)


def _ada_body(c_ref, w_ref, b_ref, o_ref):
    cond = c_ref[...]
    sc = (cond * jax.nn.sigmoid(cond)).astype(BF16)
    o_ref[0] = jnp.dot(sc, w_ref[0].astype(BF16), preferred_element_type=F32) + b_ref[0]


def _ada_mods(cond, ada_w, ada_b, tn=1024):
    depth, d, n = ada_w.shape
    rows = cond.shape[0]
    return pl.pallas_call(
        _ada_body,
        grid=(depth, n // tn),
        in_specs=[pl.BlockSpec((rows, d), lambda l, j: (0, 0)),
                  pl.BlockSpec((1, d, tn), lambda l, j: (l, 0, j)),
                  pl.BlockSpec((1, 1, tn), lambda l, j: (l, 0, j))],
        out_specs=pl.BlockSpec((1, rows, tn), lambda l, j: (l, 0, j)),
        out_shape=jax.ShapeDtypeStruct((depth, rows, n), F32),
        compiler_params=_cparams(("arbitrary", "arbitrary")),
        name="ada_mods",
    )(cond, ada_w, ada_b.reshape(depth, 1, n))


def _modulate_body(ctx_ref, x_ref, mod_ref, xs_ref, h_ref):
    j = pl.program_id(1)

    def emit(src):
        xs_ref[...] = src
        h_ref[...] = (src * (1.0 + mod_ref[0, 1:2, :]) + mod_ref[0, 0:1, :]).astype(h_ref.dtype)

    @pl.when(j == 0)
    def _():
        emit(ctx_ref[...])

    @pl.when(j > 0)
    def _():
        emit(x_ref[...])


def _modulate(x, ctx, mod):
    batch, n_lat, d = x.shape
    n_tiles = n_lat // TILE + 1
    rows = batch * n_tiles * TILE
    row = pl.BlockSpec((TILE, d), lambda b, j: (b * n_tiles + j, 0))
    return pl.pallas_call(
        _modulate_body,
        grid=(batch, n_tiles),
        in_specs=[pl.BlockSpec((None, TILE, d), lambda b, j: (b, 0, 0)),
                  pl.BlockSpec((None, TILE, d), lambda b, j: (b, jnp.maximum(j - 1, 0), 0)),
                  pl.BlockSpec((1, mod.shape[1], d), lambda b, j: (b * n_tiles + j, 0, 0))],
        out_specs=[row, row],
        out_shape=[jax.ShapeDtypeStruct((rows, d), F32), jax.ShapeDtypeStruct((rows, d), BF16)],
        compiler_params=_cparams(("arbitrary", "arbitrary")),
        name="modulate",
    )(ctx, x, mod)


def _dot_nt(a, b):
    return lax.dot_general(a, b, (((1,), (1,)), ((), ())), preferred_element_type=F32)


def _attn_body(q_ref, k_ref, vt_ref, o_ref, s_ref, p_ref, *, group, dq, dv, q_per_kv, n_lat_chunks):
    i = pl.program_id(2)

    def kcol(g):
        return (g // q_per_kv) * dq

    def vrow(g):
        return (g // q_per_kv) * dv

    def finish(g, acc, l8):
        l = jnp.sum(l8, axis=0, keepdims=True)
        out = acc * (1.0 / l)
        o_ref[:, g * dv:(g + 1) * dv] = out.T.astype(o_ref.dtype)

    def colmax8(x):
        return jnp.max(x.reshape(x.shape[0] // SUBLANE, SUBLANE, x.shape[1]), axis=0)

    def colsum8(x):
        return jnp.sum(x.reshape(x.shape[0] // SUBLANE, SUBLANE, x.shape[1]), axis=0)

    @pl.when(i == 0)
    def _():
        for g in range(group):
            st = _dot_nt(k_ref[0:TILE, kcol(g):kcol(g) + dq], q_ref[:, g * dq:(g + 1) * dq])
            m = jnp.max(st, axis=0, keepdims=True)
            p = jnp.exp2(st - m)
            acc = jnp.dot(vt_ref[vrow(g):vrow(g) + dv, 0:TILE], p.astype(BF16), preferred_element_type=F32)
            finish(g, acc, colsum8(p))

    @pl.when(i > 0)
    def _():
        neg = jnp.full((SUBLANE, TILE), -jnp.inf, F32)
        zero8 = jnp.zeros((SUBLANE, TILE), F32)
        zacc = jnp.zeros((dv, TILE), F32)
        state = {}
        for ph in range(group + 2):
            ga, gb, gc = ph, ph - 1, ph - 2
            do_a, do_b, do_c = ga < group, 0 <= gb < group, 0 <= gc < group
            q_a = q_ref[:, ga * dq:(ga + 1) * dq] if do_a else None
            m_b = jnp.max(state["m8"], axis=0, keepdims=True) if do_b else None

            def stage(start, size, carry, q_a=q_a, m_b=m_b, ga=ga, gb=gb, gc=gc,
                      do_a=do_a, do_b=do_b, do_c=do_c):
                m8, l8, acc = carry

                def value_product(acc):
                    return acc + jnp.dot(vt_ref[vrow(gc):vrow(gc) + dv, pl.ds(start, size)],
                                         p_ref[gc % 2, pl.ds(start, size), :], preferred_element_type=F32)

                if do_c and q_per_kv > 1:
                    acc = value_product(acc)
                if do_a:
                    st = _dot_nt(k_ref[pl.ds(start, size), kcol(ga):kcol(ga) + dq], q_a)
                    s_ref[ga % 2, pl.ds(start, size), :] = st
                    m8 = jnp.maximum(m8, colmax8(st))
                if do_c and q_per_kv == 1:
                    acc = value_product(acc)
                if do_b:
                    p = jnp.exp2(s_ref[gb % 2, pl.ds(start, size), :] - m_b)
                    p_ref[gb % 2, pl.ds(start, size), :] = p.astype(BF16)
                    l8 = l8 + colsum8(p)
                return m8, l8, acc

            carry = stage(0, TILE, (neg, zero8, zacc))

            def body(c, carry, stage=stage):
                return stage(pl.multiple_of(TILE + c * KV_CHUNK, TILE), KV_CHUNK, carry)

            m8, l8, acc = lax.fori_loop(0, n_lat_chunks, body, carry, unroll=True)
            if do_c:
                finish(gc, acc, state["l8"])
            if do_b:
                state["l8"] = l8
            if do_a:
                state["m8"] = m8


def _attention(q, k, vt, *, batch, n_tiles, n_steps, group, dq, dv, q_per_kv):
    r = q.shape[0]
    t_len = n_tiles * TILE
    kvw = group // q_per_kv
    return pl.pallas_call(
        functools.partial(_attn_body, group=group, dq=dq, dv=dv, q_per_kv=q_per_kv,
                          n_lat_chunks=(t_len - TILE) // KV_CHUNK),
        grid=(batch, n_steps, n_tiles),
        in_specs=[pl.BlockSpec((TILE, group * dq), lambda b, h, i: (b * n_tiles + i, h)),
                  pl.BlockSpec((t_len, kvw * dq), lambda b, h, i: (b, h), pipeline_mode=pl.Buffered(1)),
                  pl.BlockSpec((None, kvw * dv, t_len), lambda b, h, i: (b, h, 0), pipeline_mode=pl.Buffered(1))],
        out_specs=pl.BlockSpec((TILE, group * dv), lambda b, h, i: (b * n_tiles + i, h)),
        out_shape=jax.ShapeDtypeStruct((r, n_steps * group * dv), BF16),
        scratch_shapes=[pltpu.VMEM((2, t_len, TILE), F32), pltpu.VMEM((2, t_len, TILE), BF16)],
        compiler_params=_cparams(("arbitrary", "arbitrary", "arbitrary")),
        name="attention",
    )(q, k, vt)


def _lru_in_body(h_ref, prev_ref, next_ref, w_ref, cw_ref, cb_ref, u_ref, y_ref, *, n_tiles, ngroups):
    i = pl.program_id(0)
    width = u_ref.shape[1]
    h_ext = jnp.concatenate([prev_ref[...], h_ref[...], next_ref[...]], axis=0)
    xw = jnp.dot(h_ext, w_ref[:, 0:width], preferred_element_type=F32)
    left = CONV_W // 2
    for g in range(ngroups):
        seg = (i * ngroups + g) % n_tiles
        has_prev = seg > 1
        has_next = jnp.logical_and(seg > 0, seg < n_tiles - 1)
        o = HALO + g * TILE
        ext = jnp.concatenate([jnp.where(has_prev, xw[o - SUBLANE:o, :], 0.0), xw[o:o + TILE, :],
                               jnp.where(has_next, xw[o + TILE:o + TILE + SUBLANE, :], 0.0)], axis=0)
        n = ext.shape[0]
        acc = cb_ref[...]
        for j in range(CONV_W):
            shift = left - j
            rows = ext if shift == 0 else pltpu.roll(ext, shift % n, 0)
            acc = acc + rows[SUBLANE:SUBLANE + TILE, :] * cw_ref[j:j + 1, :]
        u_ref[g * TILE:(g + 1) * TILE, :] = acc
    y = jnp.dot(h_ref[...], w_ref[:, width:2 * width], preferred_element_type=F32)
    y_ref[...] = jax.nn.gelu(y, approximate=True).astype(y_ref.dtype)


def _lru_in(h, wxy, conv_w, conv_b, *, n_tiles, ngroups):
    r, d = h.shape
    width = wxy.shape[1] // 2
    tm = ngroups * TILE
    assert r % tm == 0
    per = tm // HALO
    n_halo = r // HALO
    row = pl.BlockSpec((tm, width), lambda i: (i, 0))
    return pl.pallas_call(
        functools.partial(_lru_in_body, n_tiles=n_tiles, ngroups=ngroups),
        grid=(r // tm,),
        in_specs=[pl.BlockSpec((tm, d), lambda i: (i, 0)),
                  pl.BlockSpec((HALO, d), lambda i: (jnp.maximum(i * per - 1, 0), 0)),
                  pl.BlockSpec((HALO, d), lambda i: (jnp.minimum((i + 1) * per, n_halo - 1), 0)),
                  _const_spec(wxy.shape), _const_spec((CONV_W, width)), _const_spec((1, width))],
        out_specs=[row, row],
        out_shape=[jax.ShapeDtypeStruct((r, width), F32), jax.ShapeDtypeStruct((r, width), BF16)],
        compiler_params=_cparams(("arbitrary",)),
        name="lru_in",
    )(h, h, h, wxy, conv_w, conv_b.reshape(1, width))


def _log_sigmoid(x):
    return jnp.minimum(x, 0.0) - jnp.log1p(jnp.exp(-jnp.abs(x)))


def _scan_tile(a, b, carry, reverse):
    n, w = a.shape
    groups = n // SUBLANE
    a3 = a.reshape(groups, SUBLANE, w)
    b3 = b.reshape(groups, SUBLANE, w)
    sub = lax.broadcasted_iota(jnp.int32, (groups, SUBLANE, w), 1)
    s = 1
    while s < SUBLANE:
        shift = SUBLANE - s if reverse else s
        valid = (sub < SUBLANE - s) if reverse else (sub >= s)
        a_sh = pltpu.roll(a3, shift, 1)
        b_sh = pltpu.roll(b3, shift, 1)
        b3 = jnp.where(valid, b3 + a3 * b_sh, b3)
        a3 = jnp.where(valid, a3 * a_sh, a3)
        s *= 2
    outs = [None] * groups
    for r in (range(groups - 1, -1, -1) if reverse else range(groups)):
        hv = b3[r] + a3[r] * carry
        outs[r] = hv
        edge = hv[0:1, :] if reverse else hv[SUBLANE - 1:SUBLANE, :]
        carry = jnp.broadcast_to(edge, (SUBLANE, w))
    return jnp.concatenate(outs, axis=0), carry


def _lru_body(uf_ref, ub_ref, raw_ref, ixw_ref, rab_ref, ixb_ref, lam_ref,
              hf_ref, hb_ref, carry_ref, *, blk):
    j = pl.program_id(1)

    @pl.when(j == 0)
    def _():
        carry_ref[...] = jnp.zeros_like(carry_ref)

    for d, (u_ref, h_ref) in enumerate(((uf_ref, hf_ref), (ub_ref, hb_ref))):
        half_l2 = (0.5 * LRU_C * LOG2E) * _log_sigmoid(lam_ref[d])
        for n in range(LRU_BLOCKS):
            cs = slice(n * blk, (n + 1) * blk)
            u = u_ref[:, cs]
            ub16 = u.astype(BF16)
            t_r = jnp.tanh(jnp.dot(ub16, raw_ref[d, n], preferred_element_type=F32) + rab_ref[d][:, cs])
            t_g = jnp.tanh(jnp.dot(ub16, ixw_ref[d, n], preferred_element_type=F32) + ixb_ref[d][:, cs])
            a = jnp.exp2((t_r + 1.0) * half_l2[:, cs])
            om = 1.0 - a * a
            root = jnp.where(om > 0.0, om * lax.rsqrt(om), 0.0)
            bx = root * ((t_g + 1.0) * (0.5 * u))
            h, carry = _scan_tile(a, bx, carry_ref[d, :, cs], reverse=(d == 1))
            h_ref[:, cs] = h.astype(h_ref.dtype)
            carry_ref[d, :, cs] = carry


def _lru_scan(u, ra_w, ix_w, ra_b, ix_b, lam, *, batch, n_tiles):
    r, w = u.shape
    blk = w // LRU_BLOCKS

    def fwd_map(b, j):
        return (b * n_tiles + j, 0)

    def bwd_map(b, j):
        return (b * n_tiles + jnp.where(j == 0, 0, n_tiles - j), 0)

    full5 = pl.BlockSpec((2, LRU_BLOCKS, blk, blk), lambda b, j: (0, 0, 0, 0))
    vec = pl.BlockSpec((2, 1, w), lambda b, j: (0, 0, 0))
    return pl.pallas_call(
        functools.partial(_lru_body, blk=blk),
        grid=(batch, n_tiles),
        in_specs=[pl.BlockSpec((TILE, w), fwd_map), pl.BlockSpec((TILE, w), bwd_map),
                  full5, full5, vec, vec, vec],
        out_specs=[pl.BlockSpec((TILE, w), fwd_map), pl.BlockSpec((TILE, w), bwd_map)],
        out_shape=[jax.ShapeDtypeStruct((r, w), BF16), jax.ShapeDtypeStruct((r, w), BF16)],
        scratch_shapes=[pltpu.VMEM((2, SUBLANE, w), F32)],
        compiler_params=_cparams(("arbitrary", "arbitrary")),
        name="lru_scan",
    )(u, u, ra_w, ix_w, ra_b.reshape(2, 1, w), ix_b.reshape(2, 1, w), lam.reshape(2, 1, w))


def _rope_tables(n_lat, rot_dim):
    m = rot_dim // 2
    half = m // 2
    t = np.arange(n_lat)
    pos = np.stack([t // GRID_W, t % GRID_W], axis=1).astype(np.float32)
    inv = (ROPE_THETA ** (-(np.arange(half, dtype=np.float32) * 2.0) / m)).astype(np.float32)
    lane = np.arange(LANE)
    axis = np.minimum(lane // m, 1)
    freq = inv[lane % half]
    ang = pos[:, axis] * freq[None, :]
    live = (lane < rot_dim)[None, :]
    first = ((lane % m) < half)[None, :]
    cos = np.where(live, np.cos(ang), 1.0)
    sin = np.where(live, np.sin(ang), 0.0)
    s_hi = np.where(first, -sin, 0.0)
    s_lo = np.where(first, 0.0, sin)
    ident = np.concatenate([np.ones((TILE, LANE)), np.zeros((TILE, LANE)), np.zeros((TILE, LANE))], 1)
    tab = np.concatenate([cos, s_hi, s_lo], axis=1)
    tab = np.concatenate([ident, tab], axis=0).astype(np.float32)
    return (jnp.asarray(tab[:, :LANE]), jnp.asarray(tab[:, LANE:2 * LANE]), jnp.asarray(tab[:, 2 * LANE:]))


def kernel(x, c, ctx, c_ctx, ada_w, ada_b, ln_g, ln_b, mlp_w1, mlp_w2, gqa_wq, gqa_wk, gqa_wv, gqa_wo, gqa_q_g, gqa_k_g, lru_wx, lru_wy, lru_conv_w, lru_conv_b, lru_ra_w, lru_ra_b, lru_ix_w, lru_ix_b, lru_lam, lru_wo, mla_wq_a, mla_q_a_g, mla_wq_b, mla_wkv_a, mla_kv_a_g, mla_wkv_b, mla_wo):
    batch, n_lat, d = x.shape
    depth = ada_w.shape[0]
    assert ctx.shape[1] == TILE and n_lat % KV_CHUNK == 0 and n_lat % GRID_W == 0
    assert (batch * (TILE + n_lat)) % (2 * TILE) == 0 and (batch * n_lat) % (2 * TILE) == 0
    t_len = TILE + n_lat
    n_tiles = t_len // TILE
    rows = batch * t_len
    n_groups = rows // TILE
    alpha = (2 * depth) ** 0.25


    cond_rows = -(-(batch + 1) // SUBLANE) * SUBLANE
    cond = jnp.concatenate([c, c_ctx[None, :], jnp.zeros((cond_rows - batch - 1, d), F32)], axis=0)
    mods = _ada_mods(cond, ada_w, ada_b)
    m_lat = jnp.broadcast_to(mods[:, :batch, None, :], (depth, batch, n_tiles - 1, 6 * d))
    m_ctx = jnp.broadcast_to(mods[:, batch:batch + 1, None, :], (depth, batch, 1, 6 * d))
    mods = jnp.concatenate([m_ctx, m_lat], axis=2).reshape(depth, n_groups, 6, d)

    def mod3(gate, scale, shift):
        return jnp.stack([gate, scale, shift], axis=1)

    tabs_gqa = _rope_tables(n_lat, A_HEAD_DIM)
    tabs_mla = _rope_tables(n_lat, MLA_ROPE)

    def tab_specs():
        return [pl.BlockSpec((TILE, LANE), lambda i: (i % n_tiles, 0)) for _ in range(3)]

    def vt_spec(n):
        return pl.BlockSpec((None, n, TILE), lambda i: (i // n_tiles, 0, i % n_tiles))

    def lat_tile(i):
        return i + i // (n_tiles - 1) + 1

    xs, h = _modulate(x, ctx, mods[0])
    w1_all = mlp_w1.astype(BF16)
    w2_all = mlp_w2.astype(BF16)

    for i in range(depth):
        kind = i % 3
        slot = i // 3
        m_i = mods[i]
        if kind == 0:
            wq, wk, wv, wo = (gqa_wq[slot].astype(BF16), gqa_wk[slot].astype(BF16),
                              gqa_wv[slot].astype(BF16), gqa_wo[slot].astype(BF16))

            nq, nkv = wq.shape[1], wk.shape[1]
            wqkv = jnp.concatenate([wq, wk, wv], axis=1)
            ex = [(gqa_q_g[slot].reshape(1, LANE), _const_spec((1, LANE))),
                  (gqa_k_g[slot].reshape(1, LANE), _const_spec((1, LANE)))] + list(zip(tabs_gqa, tab_specs()))
            outs = [(jax.ShapeDtypeStruct((rows, nq), BF16), _row_spec(TILE, nq)),
                    (jax.ShapeDtypeStruct((rows, nkv), BF16), _row_spec(TILE, nkv)),
                    (jax.ShapeDtypeStruct((batch, nkv, t_len), BF16), vt_spec(nkv))]
            q, k, vt = _proj(h, wqkv, ex, outs, tm=TILE, name="gqa_qkv",
                             epilogue=functools.partial(_ep_gqa_qkv, nq=nq // LANE, nkv=nkv // LANE,
                                                        off=A_HEAD_DIM // 4,
                                                        qscale=A_HEAD_DIM ** -0.5 * LOG2E))
            o = _attention(q, k, vt, batch=batch, n_tiles=n_tiles, n_steps=A_HEADS // GQA_STEP_HEADS,
                           group=GQA_STEP_HEADS, dq=A_HEAD_DIM, dv=A_HEAD_DIM, q_per_kv=A_GROUP)
            a_list, w_out, prologue = [o], wo, None
        elif kind == 1:
            wx, wy, wo = lru_wx[slot].astype(BF16), lru_wy[slot].astype(BF16), lru_wo[slot].astype(BF16)
            u, y = _lru_in(h, jnp.concatenate([wx, wy], axis=1), lru_conv_w[slot], lru_conv_b[slot],
                           n_tiles=n_tiles, ngroups=2)
            hf, hb = _lru_scan(u, (0.5 * lru_ra_w[slot]).astype(BF16), (0.5 * lru_ix_w[slot]).astype(BF16),
                               0.5 * lru_ra_b[slot], 0.5 * lru_ix_b[slot], lru_lam[slot],
                               batch=batch, n_tiles=n_tiles)
            a_list, w_out, prologue = [hf, hb, y], wo, _pro_lru_gate
        else:
            nq = MLA_NOPE + MLA_ROPE
            wq_a = mla_wq_a[slot].astype(BF16)
            wq_b = mla_wq_b[slot].reshape(MLA_LORA, MLA_HEADS, nq)
            wq_b = jnp.concatenate([wq_b, jnp.zeros((MLA_LORA, MLA_HEADS, 2 * LANE - nq), F32)], axis=-1)
            wq_b = wq_b.reshape(MLA_LORA, MLA_HEADS * 2 * LANE).astype(BF16)
            wkv_a = jnp.concatenate([mla_wkv_a[slot], jnp.zeros((d, LANE - MLA_ROPE), F32)], axis=-1).astype(BF16)
            wkv_b = mla_wkv_b[slot].reshape(MLA_LORA, MLA_HEADS, MLA_NOPE + MLA_V)
            wkv_b = jnp.concatenate([wkv_b[:, :, :MLA_NOPE].reshape(MLA_LORA, -1),
                                     wkv_b[:, :, MLA_NOPE:].reshape(MLA_LORA, -1)], axis=-1).astype(BF16)
            wo = mla_wo[slot].astype(BF16)

            q, kfull, vt = _mla_proj(h, jnp.concatenate([wq_a, wkv_a], axis=1), wq_b, wkv_b,
                                     mla_q_a_g[slot], mla_kv_a_g[slot], tabs_mla,
                                     batch=batch, n_tiles=n_tiles, qscale=float(nq) ** -0.5 * LOG2E)
            o = _attention(q, kfull, vt, batch=batch, n_tiles=n_tiles, n_steps=MLA_HEADS // MLA_STEP_HEADS,
                           group=MLA_STEP_HEADS, dq=2 * LANE, dv=MLA_V, q_per_kv=1)
            a_list, w_out, prologue = [o], wo, None

        last = i == depth - 1
        xs, h2 = _mix_out(a_list, w_out, xs, mod3(m_i[:, 2], m_i[:, 4], m_i[:, 3]), ln_g[i, 0], ln_b[i, 0],
                          ngroups=1 if last else 2, out_rows=batch * n_lat if last else rows,
                          row_map=lat_tile if last else (lambda t: t), prologue=prologue, alpha=alpha)
        if not last:
            nxt = mod3(m_i[:, 5], mods[i + 1][:, 1], mods[i + 1][:, 0])
        else:
            zero = jnp.zeros_like(m_i[:, 5])
            nxt = mod3(m_i[:, 5], zero, zero)
            nxt = nxt.reshape(batch, n_tiles, 3, d)[:, 1:].reshape(batch * (n_tiles - 1), 3, d)
        xs, h = _mlp(h2, w1_all, w2_all, i, xs, nxt, ln_g[i, 1], ln_b[i, 1], tm=2 * TILE, tf=1024,
                     alpha=alpha)

    return xs.reshape(batch, n_lat, d)
```

```python
import functools
import math

import numpy as np
import jax
import jax.numpy as jnp
from jax import lax
from jax.experimental import pallas as pl
from jax.experimental.pallas import tpu as pltpu

F32 = jnp.float32
BF16 = jnp.bfloat16

TILE = 256
GRID_W = 64
ROPE_THETA = 10000.0
EPS = 1e-6
LANE = 128
SUBLANE = 8
VMEM_LIMIT_MB = 56

A_HEADS = 16
A_KV_HEADS = 4
A_GROUP = A_HEADS // A_KV_HEADS
A_HEAD_DIM = 128

LRU_BLOCKS = 8
CONV_W = 4
LRU_C = 8.0
HALO = 16

MLA_HEADS = 16
MLA_LORA = 512
MLA_NOPE = 128
MLA_ROPE = 64
MLA_V = 128

KV_CHUNK = 1024
GQA_STEP_HEADS = 16
MLA_STEP_HEADS = 8
LOG2E = math.log2(math.e)


def _cparams(sem, vmem_mb=VMEM_LIMIT_MB):
    return pltpu.CompilerParams(dimension_semantics=sem,
                                vmem_limit_bytes=vmem_mb * 1024 * 1024)


def _proj_body(*refs, n_ex, epilogue):
    a_ref, w_ref = refs[0], refs[1]
    ex = refs[2:2 + n_ex]
    outs = refs[2 + n_ex:]
    epilogue(jnp.dot(a_ref[...], w_ref[...], preferred_element_type=F32), ex, outs)


def _const_spec(shape):
    zeros = (0,) * len(shape)
    return pl.BlockSpec(shape, lambda i: zeros, pipeline_mode=pl.Buffered(1))


def _proj(a, w, extras, outs, *, tm, epilogue, name=None):
    m, kdim = a.shape
    assert m % tm == 0 and w.shape[0] == kdim
    return pl.pallas_call(
        functools.partial(_proj_body, n_ex=len(extras), epilogue=epilogue),
        grid=(m // tm,),
        in_specs=[pl.BlockSpec((tm, kdim), lambda i: (i, 0)), _const_spec(w.shape)] + [s for _, s in extras],
        out_specs=[s for _, s in outs],
        out_shape=[s for s, _ in outs],
        compiler_params=_cparams(("arbitrary",)),
        name=name,
    )(a, w, *[arr for arr, _ in extras])


def _row_spec(tm, width):
    return pl.BlockSpec((tm, width), lambda i: (i, 0))


def _mix_out_body(*refs, n_a, ngroups, prologue, alpha):
    a_refs = refs[:n_a]
    w_ref, xold_ref, mod_ref, lng_ref, lnb_ref, x_out, h_out = refs[n_a:]
    for g in range(ngroups):
        r0 = g * TILE
        vals = [r[r0:r0 + TILE, :] for r in a_refs]
        a = vals[0] if prologue is None else prologue(vals)
        part = jnp.dot(a, w_ref[...], preferred_element_type=F32)
        _resid_ln_rows(part, r0, xold_ref, mod_ref, lng_ref, lnb_ref, x_out, h_out, alpha)


def _mix_out(a_list, w, x_old, mod, lng, lnb, *, ngroups, out_rows, row_map, prologue, alpha):
    kdim, d = w.shape
    tm = ngroups * TILE
    assert out_rows % tm == 0
    src = lambda i: (row_map(i), 0)
    dst = pl.BlockSpec((tm, d), lambda i: (i, 0))
    return pl.pallas_call(
        functools.partial(_mix_out_body, n_a=len(a_list), ngroups=ngroups, prologue=prologue, alpha=alpha),
        grid=(out_rows // tm,),
        in_specs=[pl.BlockSpec((tm, kdim), src) for _ in a_list]
        + [_const_spec(w.shape), pl.BlockSpec((tm, d), src),
           pl.BlockSpec((ngroups, 3, d), lambda i: (row_map(i), 0, 0)),
           _const_spec((1, d)), _const_spec((1, d))],
        out_specs=[dst, dst],
        out_shape=[jax.ShapeDtypeStruct((out_rows, d), F32), jax.ShapeDtypeStruct((out_rows, d), BF16)],
        compiler_params=_cparams(("arbitrary",)),
        name="mix_out",
    )(*a_list, w, x_old, mod, lng.reshape(1, d), lnb.reshape(1, d))


def _rope(x, cos, s_hi, s_lo, off):
    return x * cos + pltpu.roll(x, LANE - off, 1) * s_hi + pltpu.roll(x, off, 1) * s_lo


def _ep_headnorm_rope(acc, ex, outs, *, nblk, off, scale):
    g_ref, cos_ref, shi_ref, slo_ref = ex
    g = g_ref[...]
    cos, shi, slo = cos_ref[...], shi_ref[...], slo_ref[...]
    for j in range(nblk):
        x = acc[:, j * LANE:(j + 1) * LANE]
        ms = jnp.mean(x * x, axis=-1, keepdims=True)
        xn = x * lax.rsqrt(ms + EPS) * g
        y = _rope(xn, cos, shi, slo, off)
        if scale != 1.0:
            y = y * scale
        outs[0][:, j * LANE:(j + 1) * LANE] = y.astype(outs[0].dtype)


def _ep_gqa_qkv(acc, ex, outs, *, nq, nkv, off, qscale):
    qg_ref, kg_ref, cos_ref, shi_ref, slo_ref = ex
    q_out, k_out, vt_out = outs
    tabs = (cos_ref, shi_ref, slo_ref)
    _ep_headnorm_rope(acc[:, :nq * LANE], (qg_ref,) + tabs, (q_out,), nblk=nq, off=off, scale=qscale)
    _ep_headnorm_rope(acc[:, nq * LANE:(nq + nkv) * LANE], (kg_ref,) + tabs, (k_out,), nblk=nkv, off=off,
                      scale=1.0)
    vt_out[...] = acc[:, (nq + nkv) * LANE:].T.astype(vt_out.dtype)


def _rms(x, g):
    ms = jnp.mean(x * x, axis=-1, keepdims=True)
    return x * lax.rsqrt(ms + EPS) * g


def _ep_mla_q(acc, ex, outs, *, nheads, off, scale):
    cos_ref, shi_ref, slo_ref = ex
    cos, shi, slo = cos_ref[...], shi_ref[...], slo_ref[...]
    for h in range(nheads):
        c0 = h * 2 * LANE
        outs[0][:, c0:c0 + LANE] = (acc[:, c0:c0 + LANE] * scale).astype(outs[0].dtype)
        pe = acc[:, c0 + LANE:c0 + 2 * LANE]
        outs[0][:, c0 + LANE:c0 + 2 * LANE] = (_rope(pe, cos, shi, slo, off) * scale).astype(outs[0].dtype)


def _ep_mla_kv(acc, ex, outs, *, nheads):
    kpe_ref, = ex
    k_out, vt_out = outs
    kpe = kpe_ref[...]
    for h in range(nheads):
        k_out[:, 2 * h * LANE:(2 * h + 1) * LANE] = acc[:, h * LANE:(h + 1) * LANE].astype(k_out.dtype)
        k_out[:, (2 * h + 1) * LANE:(2 * h + 2) * LANE] = kpe
    vt_out[...] = acc[:, nheads * LANE:2 * nheads * LANE].T.astype(vt_out.dtype)


def _resid_ln_rows(f_out, r0, xold_ref, mod_ref, lng_ref, lnb_ref, x_out, h_out, alpha):
    n = f_out.shape[0]
    g = r0 // TILE
    assert r0 % TILE + n <= TILE
    gate = mod_ref[g, 0:1, :]
    scale = mod_ref[g, 1:2, :]
    shift = mod_ref[g, 2:3, :]
    y = alpha * xold_ref[r0:r0 + n, :] + (1.0 + gate) * f_out
    mu = jnp.mean(y, axis=-1, keepdims=True)
    yc = y - mu
    var = jnp.mean(yc * yc, axis=-1, keepdims=True)
    xn = yc * lax.rsqrt(var + EPS) * lng_ref[...] + lnb_ref[...]
    x_out[r0:r0 + n, :] = xn
    h_out[r0:r0 + n, :] = (xn * (1.0 + scale) + shift).astype(h_out.dtype)


def _pro_lru_gate(vals):
    hf, hb, y = vals
    return ((hf.astype(F32) + hb.astype(F32)) * y.astype(F32)).astype(BF16)


def _mlp_body(h_ref, w1_ref, w2_ref, xold_ref, mod_ref, lng_ref, lnb_ref, x_out, h_out, acc_ref,
              *, nf, ngroups, alpha):
    f = pl.program_id(1)

    def hidden():
        u = jnp.maximum(jnp.dot(h_ref[...], w1_ref[...], preferred_element_type=F32), 0.0)
        return (u * u).astype(BF16)

    @pl.when(f == 0)
    def _():
        acc_ref[...] = jnp.dot(hidden(), w2_ref[...], preferred_element_type=F32)

    @pl.when(jnp.logical_and(f > 0, f < nf - 1))
    def _():
        acc_ref[...] += jnp.dot(hidden(), w2_ref[...], preferred_element_type=F32)

    @pl.when(f == nf - 1)
    def _():
        u2 = hidden()
        for g in range(ngroups):
            r0 = g * TILE
            tot = acc_ref[r0:r0 + TILE, :] + jnp.dot(u2[r0:r0 + TILE, :], w2_ref[...],
                                                   preferred_element_type=F32)
            _resid_ln_rows(tot, r0, xold_ref, mod_ref, lng_ref, lnb_ref, x_out, h_out, alpha)


def _mlp(h, w1, w2, layer, x_old, mod, lng, lnb, *, tm, tf, alpha):
    rows, d = h.shape
    ff = w1.shape[2]
    nf = ff // tf
    assert rows % tm == 0 and ff % tf == 0
    assert nf >= 2
    row = pl.BlockSpec((tm, d), lambda i, f: (i, 0))
    vec = pl.BlockSpec((1, d), lambda i, f: (0, 0))
    return pl.pallas_call(
        functools.partial(_mlp_body, nf=nf, ngroups=tm // TILE, alpha=alpha),
        grid=(rows // tm, nf),
        in_specs=[row,
                  pl.BlockSpec((None, d, tf), lambda i, f: (layer, 0, f)),
                  pl.BlockSpec((None, tf, d), lambda i, f: (layer, f, 0)),
                  row,
                  pl.BlockSpec((tm // TILE, 3, d), lambda i, f: (i, 0, 0)),
                  vec, vec],
        out_specs=[row, row],
        out_shape=[jax.ShapeDtypeStruct((rows, d), F32), jax.ShapeDtypeStruct((rows, d), BF16)],
        scratch_shapes=[pltpu.VMEM((tm, d), F32)],
        compiler_params=_cparams(("arbitrary", "arbitrary")),
        name="mlp",
    )(h, w1, w2, x_old, mod, lng.reshape(1, d), lnb.reshape(1, d))


def _mla_proj_body(h_ref, wa_ref, wqb_ref, wkvb_ref, qg_ref, kvg_ref, cos_ref, shi_ref, slo_ref,
                   q_out, k_out, vt_out, *, lora, nheads, off, qscale):
    tabs = (cos_ref, shi_ref, slo_ref)
    t = jnp.dot(h_ref[...], wa_ref[...], preferred_element_type=F32)
    qa = _rms(t[:, 0:lora], qg_ref[...]).astype(BF16)
    ckv = _rms(t[:, lora:2 * lora], kvg_ref[...]).astype(BF16)
    kpe = _rope(t[:, 2 * lora:2 * lora + LANE], cos_ref[...], shi_ref[...], slo_ref[...], off).astype(BF16)
    _ep_mla_q(jnp.dot(qa, wqb_ref[...], preferred_element_type=F32), tabs, (q_out,),
              nheads=nheads, off=off, scale=qscale)
    _ep_mla_kv(jnp.dot(ckv, wkvb_ref[...], preferred_element_type=F32), (kpe,), (k_out, vt_out),
               nheads=nheads)


def _mla_proj(h, w_a, wq_b, wkv_b, q_g, kv_g, tabs, *, batch, n_tiles, qscale):
    rows, d = h.shape
    t_len = n_tiles * TILE
    nq, nkv = wq_b.shape[1], wkv_b.shape[1]
    nv = MLA_HEADS * MLA_V

    const = _const_spec
    tab = pl.BlockSpec((TILE, LANE), lambda i: (i % n_tiles, 0))
    return pl.pallas_call(
        functools.partial(_mla_proj_body, lora=MLA_LORA, nheads=MLA_HEADS, off=MLA_ROPE // 4, qscale=qscale),
        grid=(rows // TILE,),
        in_specs=[pl.BlockSpec((TILE, d), lambda i: (i, 0)),
                  const(w_a.shape), const(wq_b.shape), const(wkv_b.shape),
                  const((1, MLA_LORA)), const((1, MLA_LORA)), tab, tab, tab],
        out_specs=[pl.BlockSpec((TILE, nq), lambda i: (i, 0)),
                   pl.BlockSpec((TILE, nq), lambda i: (i, 0)),
                   pl.BlockSpec((None, nv, TILE), lambda i: (i // n_tiles, 0, i % n_tiles))],
        out_shape=[jax.ShapeDtypeStruct((rows, nq), BF16), jax.ShapeDtypeStruct((rows, nq), BF16),
                   jax.ShapeDtypeStruct((batch, nv, t_len), BF16)],
        compiler_params=_cparams(("arbitrary",)),
        name="mla_proj",
    )(h, w_a, wq_b, wkv_b, q_g.reshape(1, MLA_LORA), kv_g.reshape(1, MLA_LORA), *tabs)


def _ada_body(c_ref, w_ref, b_ref, o_ref):
    cond = c_ref[...]
    sc = (cond * jax.nn.sigmoid(cond)).astype(BF16)
    o_ref[0] = jnp.dot(sc, w_ref[0].astype(BF16), preferred_element_type=F32) + b_ref[0]


def _ada_mods(cond, ada_w, ada_b, tn=1024):
    depth, d, n = ada_w.shape
    rows = cond.shape[0]
    return pl.pallas_call(
        _ada_body,
        grid=(depth, n // tn),
        in_specs=[pl.BlockSpec((rows, d), lambda l, j: (0, 0)),
                  pl.BlockSpec((1, d, tn), lambda l, j: (l, 0, j)),
                  pl.BlockSpec((1, 1, tn), lambda l, j: (l, 0, j))],
        out_specs=pl.BlockSpec((1, rows, tn), lambda l, j: (l, 0, j)),
        out_shape=jax.ShapeDtypeStruct((depth, rows, n), F32),
        compiler_params=_cparams(("arbitrary", "arbitrary")),
        name="ada_mods",
    )(cond, ada_w, ada_b.reshape(depth, 1, n))


def _modulate_body(ctx_ref, x_ref, mod_ref, xs_ref, h_ref):
    j = pl.program_id(1)

    def emit(src):
        xs_ref[...] = src
        h_ref[...] = (src * (1.0 + mod_ref[0, 1:2, :]) + mod_ref[0, 0:1, :]).astype(h_ref.dtype)

    @pl.when(j == 0)
    def _():
        emit(ctx_ref[...])

    @pl.when(j > 0)
    def _():
        emit(x_ref[...])


def _modulate(x, ctx, mod):
    batch, n_lat, d = x.shape
    n_tiles = n_lat // TILE + 1
    rows = batch * n_tiles * TILE
    row = pl.BlockSpec((TILE, d), lambda b, j: (b * n_tiles + j, 0))
    return pl.pallas_call(
        _modulate_body,
        grid=(batch, n_tiles),
        in_specs=[pl.BlockSpec((None, TILE, d), lambda b, j: (b, 0, 0)),
                  pl.BlockSpec((None, TILE, d), lambda b, j: (b, jnp.maximum(j - 1, 0), 0)),
                  pl.BlockSpec((1, mod.shape[1], d), lambda b, j: (b * n_tiles + j, 0, 0))],
        out_specs=[row, row],
        out_shape=[jax.ShapeDtypeStruct((rows, d), F32), jax.ShapeDtypeStruct((rows, d), BF16)],
        compiler_params=_cparams(("arbitrary", "arbitrary")),
        name="modulate",
    )(ctx, x, mod)


def _dot_nt(a, b):
    return lax.dot_general(a, b, (((1,), (1,)), ((), ())), preferred_element_type=F32)


def _attn_body(q_ref, k_ref, vt_ref, o_ref, s_ref, p_ref, *, group, dq, dv, q_per_kv, n_lat_chunks):
    i = pl.program_id(2)

    def kcol(g):
        return (g // q_per_kv) * dq

    def vrow(g):
        return (g // q_per_kv) * dv

    def finish(g, acc, l8):
        l = jnp.sum(l8, axis=0, keepdims=True)
        out = acc * (1.0 / l)
        o_ref[:, g * dv:(g + 1) * dv] = out.T.astype(o_ref.dtype)

    def colmax8(x):
        return jnp.max(x.reshape(x.shape[0] // SUBLANE, SUBLANE, x.shape[1]), axis=0)

    def colsum8(x):
        return jnp.sum(x.reshape(x.shape[0] // SUBLANE, SUBLANE, x.shape[1]), axis=0)

    @pl.when(i == 0)
    def _():
        for g in range(group):
            st = _dot_nt(k_ref[0:TILE, kcol(g):kcol(g) + dq], q_ref[:, g * dq:(g + 1) * dq])
            m = jnp.max(st, axis=0, keepdims=True)
            p = jnp.exp2(st - m)
            acc = jnp.dot(vt_ref[vrow(g):vrow(g) + dv, 0:TILE], p.astype(BF16), preferred_element_type=F32)
            finish(g, acc, colsum8(p))

    @pl.when(i > 0)
    def _():
        neg = jnp.full((SUBLANE, TILE), -jnp.inf, F32)
        zero8 = jnp.zeros((SUBLANE, TILE), F32)
        zacc = jnp.zeros((dv, TILE), F32)
        state = {}
        for ph in range(group + 2):
            ga, gb, gc = ph, ph - 1, ph - 2
            do_a, do_b, do_c = ga < group, 0 <= gb < group, 0 <= gc < group
            q_a = q_ref[:, ga * dq:(ga + 1) * dq] if do_a else None
            m_b = jnp.max(state["m8"], axis=0, keepdims=True) if do_b else None

            def stage(start, size, carry, q_a=q_a, m_b=m_b, ga=ga, gb=gb, gc=gc,
                      do_a=do_a, do_b=do_b, do_c=do_c):
                m8, l8, acc = carry

                def value_product(acc):
                    return acc + jnp.dot(vt_ref[vrow(gc):vrow(gc) + dv, pl.ds(start, size)],
                                         p_ref[gc % 2, pl.ds(start, size), :], preferred_element_type=F32)

                if do_c and q_per_kv > 1:
                    acc = value_product(acc)
                if do_a:
                    st = _dot_nt(k_ref[pl.ds(start, size), kcol(ga):kcol(ga) + dq], q_a)
                    s_ref[ga % 2, pl.ds(start, size), :] = st
                    m8 = jnp.maximum(m8, colmax8(st))
                if do_c and q_per_kv == 1:
                    acc = value_product(acc)
                if do_b:
                    p = jnp.exp2(s_ref[gb % 2, pl.ds(start, size), :] - m_b)
                    p_ref[gb % 2, pl.ds(start, size), :] = p.astype(BF16)
                    l8 = l8 + colsum8(p)
                return m8, l8, acc

            carry = stage(0, TILE, (neg, zero8, zacc))

            def body(c, carry, stage=stage):
                return stage(pl.multiple_of(TILE + c * KV_CHUNK, TILE), KV_CHUNK, carry)

            m8, l8, acc = lax.fori_loop(0, n_lat_chunks, body, carry, unroll=True)
            if do_c:
                finish(gc, acc, state["l8"])
            if do_b:
                state["l8"] = l8
            if do_a:
                state["m8"] = m8


def _attention(q, k, vt, *, batch, n_tiles, n_steps, group, dq, dv, q_per_kv):
    r = q.shape[0]
    t_len = n_tiles * TILE
    kvw = group // q_per_kv
    return pl.pallas_call(
        functools.partial(_attn_body, group=group, dq=dq, dv=dv, q_per_kv=q_per_kv,
                          n_lat_chunks=(t_len - TILE) // KV_CHUNK),
        grid=(batch, n_steps, n_tiles),
        in_specs=[pl.BlockSpec((TILE, group * dq), lambda b, h, i: (b * n_tiles + i, h)),
                  pl.BlockSpec((t_len, kvw * dq), lambda b, h, i: (b, h), pipeline_mode=pl.Buffered(1)),
                  pl.BlockSpec((None, kvw * dv, t_len), lambda b, h, i: (b, h, 0), pipeline_mode=pl.Buffered(1))],
        out_specs=pl.BlockSpec((TILE, group * dv), lambda b, h, i: (b * n_tiles + i, h)),
        out_shape=jax.ShapeDtypeStruct((r, n_steps * group * dv), BF16),
        scratch_shapes=[pltpu.VMEM((2, t_len, TILE), F32), pltpu.VMEM((2, t_len, TILE), BF16)],
        compiler_params=_cparams(("arbitrary", "arbitrary", "arbitrary")),
        name="attention",
    )(q, k, vt)


def _lru_in_body(h_ref, prev_ref, next_ref, w_ref, cw_ref, cb_ref, u_ref, y_ref, *, n_tiles, ngroups):
    i = pl.program_id(0)
    width = u_ref.shape[1]
    h_ext = jnp.concatenate([prev_ref[...], h_ref[...], next_ref[...]], axis=0)
    xw = jnp.dot(h_ext, w_ref[:, 0:width], preferred_element_type=F32)
    left = CONV_W // 2
    for g in range(ngroups):
        seg = (i * ngroups + g) % n_tiles
        has_prev = seg > 1
        has_next = jnp.logical_and(seg > 0, seg < n_tiles - 1)
        o = HALO + g * TILE
        ext = jnp.concatenate([jnp.where(has_prev, xw[o - SUBLANE:o, :], 0.0), xw[o:o + TILE, :],
                               jnp.where(has_next, xw[o + TILE:o + TILE + SUBLANE, :], 0.0)], axis=0)
        n = ext.shape[0]
        acc = cb_ref[...]
        for j in range(CONV_W):
            shift = left - j
            rows = ext if shift == 0 else pltpu.roll(ext, shift % n, 0)
            acc = acc + rows[SUBLANE:SUBLANE + TILE, :] * cw_ref[j:j + 1, :]
        u_ref[g * TILE:(g + 1) * TILE, :] = acc
    y = jnp.dot(h_ref[...], w_ref[:, width:2 * width], preferred_element_type=F32)
    y_ref[...] = jax.nn.gelu(y, approximate=True).astype(y_ref.dtype)


def _lru_in(h, wxy, conv_w, conv_b, *, n_tiles, ngroups):
    r, d = h.shape
    width = wxy.shape[1] // 2
    tm = ngroups * TILE
    assert r % tm == 0
    per = tm // HALO
    n_halo = r // HALO
    row = pl.BlockSpec((tm, width), lambda i: (i, 0))
    return pl.pallas_call(
        functools.partial(_lru_in_body, n_tiles=n_tiles, ngroups=ngroups),
        grid=(r // tm,),
        in_specs=[pl.BlockSpec((tm, d), lambda i: (i, 0)),
                  pl.BlockSpec((HALO, d), lambda i: (jnp.maximum(i * per - 1, 0), 0)),
                  pl.BlockSpec((HALO, d), lambda i: (jnp.minimum((i + 1) * per, n_halo - 1), 0)),
                  _const_spec(wxy.shape), _const_spec((CONV_W, width)), _const_spec((1, width))],
        out_specs=[row, row],
        out_shape=[jax.ShapeDtypeStruct((r, width), F32), jax.ShapeDtypeStruct((r, width), BF16)],
        compiler_params=_cparams(("arbitrary",)),
        name="lru_in",
    )(h, h, h, wxy, conv_w, conv_b.reshape(1, width))


def _log_sigmoid(x):
    return jnp.minimum(x, 0.0) - jnp.log1p(jnp.exp(-jnp.abs(x)))


def _scan_tile(a, b, carry, reverse):
    n, w = a.shape
    groups = n // SUBLANE
    a3 = a.reshape(groups, SUBLANE, w)
    b3 = b.reshape(groups, SUBLANE, w)
    sub = lax.broadcasted_iota(jnp.int32, (groups, SUBLANE, w), 1)
    s = 1
    while s < SUBLANE:
        shift = SUBLANE - s if reverse else s
        valid = (sub < SUBLANE - s) if reverse else (sub >= s)
        a_sh = pltpu.roll(a3, shift, 1)
        b_sh = pltpu.roll(b3, shift, 1)
        b3 = jnp.where(valid, b3 + a3 * b_sh, b3)
        a3 = jnp.where(valid, a3 * a_sh, a3)
        s *= 2
    outs = [None] * groups
    for r in (range(groups - 1, -1, -1) if reverse else range(groups)):
        hv = b3[r] + a3[r] * carry
        outs[r] = hv
        edge = hv[0:1, :] if reverse else hv[SUBLANE - 1:SUBLANE, :]
        carry = jnp.broadcast_to(edge, (SUBLANE, w))
    return jnp.concatenate(outs, axis=0), carry


def _lru_body(uf_ref, ub_ref, raw_ref, ixw_ref, rab_ref, ixb_ref, lam_ref,
              hf_ref, hb_ref, carry_ref, *, blk):
    j = pl.program_id(1)

    @pl.when(j == 0)
    def _():
        carry_ref[...] = jnp.zeros_like(carry_ref)

    for d, (u_ref, h_ref) in enumerate(((uf_ref, hf_ref), (ub_ref, hb_ref))):
        half_l2 = (0.5 * LRU_C * LOG2E) * _log_sigmoid(lam_ref[d])
        for n in range(LRU_BLOCKS):
            cs = slice(n * blk, (n + 1) * blk)
            u = u_ref[:, cs]
            ub16 = u.astype(BF16)
            z_r = jnp.dot(ub16, raw_ref[d, n], preferred_element_type=F32) + rab_ref[d][:, cs]
            z_g = jnp.dot(ub16, ixw_ref[d, n], preferred_element_type=F32) + ixb_ref[d][:, cs]
            for c0 in range(0, blk, LANE):
                lanes = slice(n * blk + c0, n * blk + c0 + LANE)
                t_r = jnp.tanh(z_r[:, c0:c0 + LANE])
                t_g = jnp.tanh(z_g[:, c0:c0 + LANE])
                a = jnp.exp2((t_r + 1.0) * half_l2[:, lanes])
                om = 1.0 - a * a
                root = jnp.where(om > 0.0, om * lax.rsqrt(om), 0.0)
                bx = root * ((t_g + 1.0) * (0.5 * u[:, c0:c0 + LANE]))
                h, carry = _scan_tile(a, bx, carry_ref[d, :, lanes], reverse=(d == 1))
                h_ref[:, lanes] = h.astype(h_ref.dtype)
                carry_ref[d, :, lanes] = carry


def _lru_scan(u, ra_w, ix_w, ra_b, ix_b, lam, *, batch, n_tiles):
    r, w = u.shape
    blk = w // LRU_BLOCKS

    def fwd_map(b, j):
        return (b * n_tiles + j, 0)

    def bwd_map(b, j):
        return (b * n_tiles + jnp.where(j == 0, 0, n_tiles - j), 0)

    full5 = pl.BlockSpec((2, LRU_BLOCKS, blk, blk), lambda b, j: (0, 0, 0, 0))
    vec = pl.BlockSpec((2, 1, w), lambda b, j: (0, 0, 0))
    return pl.pallas_call(
        functools.partial(_lru_body, blk=blk),
        grid=(batch, n_tiles),
        in_specs=[pl.BlockSpec((TILE, w), fwd_map), pl.BlockSpec((TILE, w), bwd_map),
                  full5, full5, vec, vec, vec],
        out_specs=[pl.BlockSpec((TILE, w), fwd_map), pl.BlockSpec((TILE, w), bwd_map)],
        out_shape=[jax.ShapeDtypeStruct((r, w), BF16), jax.ShapeDtypeStruct((r, w), BF16)],
        scratch_shapes=[pltpu.VMEM((2, SUBLANE, w), F32)],
        compiler_params=_cparams(("arbitrary", "arbitrary")),
        name="lru_scan",
    )(u, u, ra_w, ix_w, ra_b.reshape(2, 1, w), ix_b.reshape(2, 1, w), lam.reshape(2, 1, w))


def _rope_tables(n_lat, rot_dim):
    m = rot_dim // 2
    half = m // 2
    t = np.arange(n_lat)
    pos = np.stack([t // GRID_W, t % GRID_W], axis=1).astype(np.float32)
    inv = (ROPE_THETA ** (-(np.arange(half, dtype=np.float32) * 2.0) / m)).astype(np.float32)
    lane = np.arange(LANE)
    axis = np.minimum(lane // m, 1)
    freq = inv[lane % half]
    ang = pos[:, axis] * freq[None, :]
    live = (lane < rot_dim)[None, :]
    first = ((lane % m) < half)[None, :]
    cos = np.where(live, np.cos(ang), 1.0)
    sin = np.where(live, np.sin(ang), 0.0)
    s_hi = np.where(first, -sin, 0.0)
    s_lo = np.where(first, 0.0, sin)
    ident = np.concatenate([np.ones((TILE, LANE)), np.zeros((TILE, LANE)), np.zeros((TILE, LANE))], 1)
    tab = np.concatenate([cos, s_hi, s_lo], axis=1)
    tab = np.concatenate([ident, tab], axis=0).astype(np.float32)
    return (jnp.asarray(tab[:, :LANE]), jnp.asarray(tab[:, LANE:2 * LANE]), jnp.asarray(tab[:, 2 * LANE:]))


def kernel(x, c, ctx, c_ctx, ada_w, ada_b, ln_g, ln_b, mlp_w1, mlp_w2, gqa_wq, gqa_wk, gqa_wv, gqa_wo, gqa_q_g, gqa_k_g, lru_wx, lru_wy, lru_conv_w, lru_conv_b, lru_ra_w, lru_ra_b, lru_ix_w, lru_ix_b, lru_lam, lru_wo, mla_wq_a, mla_q_a_g, mla_wq_b, mla_wkv_a, mla_kv_a_g, mla_wkv_b, mla_wo):
    batch, n_lat, d = x.shape
    depth = ada_w.shape[0]
    assert ctx.shape[1] == TILE and n_lat % KV_CHUNK == 0 and n_lat % GRID_W == 0
    assert (batch * (TILE + n_lat)) % (2 * TILE) == 0 and (batch * n_lat) % (2 * TILE) == 0
    t_len = TILE + n_lat
    n_tiles = t_len // TILE
    rows = batch * t_len
    n_groups = rows // TILE
    alpha = (2 * depth) ** 0.25


    cond_rows = -(-(batch + 1) // SUBLANE) * SUBLANE
    cond = jnp.concatenate([c, c_ctx[None, :], jnp.zeros((cond_rows - batch - 1, d), F32)], axis=0)
    mods = _ada_mods(cond, ada_w, ada_b)
    m_lat = jnp.broadcast_to(mods[:, :batch, None, :], (depth, batch, n_tiles - 1, 6 * d))
    m_ctx = jnp.broadcast_to(mods[:, batch:batch + 1, None, :], (depth, batch, 1, 6 * d))
    mods = jnp.concatenate([m_ctx, m_lat], axis=2).reshape(depth, n_groups, 6, d)

    def mod3(gate, scale, shift):
        return jnp.stack([gate, scale, shift], axis=1)

    tabs_gqa = _rope_tables(n_lat, A_HEAD_DIM)
    tabs_mla = _rope_tables(n_lat, MLA_ROPE)

    def tab_specs():
        return [pl.BlockSpec((TILE, LANE), lambda i: (i % n_tiles, 0)) for _ in range(3)]

    def vt_spec(n):
        return pl.BlockSpec((None, n, TILE), lambda i: (i // n_tiles, 0, i % n_tiles))

    def lat_tile(i):
        return i + i // (n_tiles - 1) + 1

    xs, h = _modulate(x, ctx, mods[0])
    w1_all = mlp_w1.astype(BF16)
    w2_all = mlp_w2.astype(BF16)

    for i in range(depth):
        kind = i % 3
        slot = i // 3
        m_i = mods[i]
        if kind == 0:
            wq, wk, wv, wo = (gqa_wq[slot].astype(BF16), gqa_wk[slot].astype(BF16),
                              gqa_wv[slot].astype(BF16), gqa_wo[slot].astype(BF16))

            nq, nkv = wq.shape[1], wk.shape[1]
            wqkv = jnp.concatenate([wq, wk, wv], axis=1)
            ex = [(gqa_q_g[slot].reshape(1, LANE), _const_spec((1, LANE))),
                  (gqa_k_g[slot].reshape(1, LANE), _const_spec((1, LANE)))] + list(zip(tabs_gqa, tab_specs()))
            outs = [(jax.ShapeDtypeStruct((rows, nq), BF16), _row_spec(TILE, nq)),
                    (jax.ShapeDtypeStruct((rows, nkv), BF16), _row_spec(TILE, nkv)),
                    (jax.ShapeDtypeStruct((batch, nkv, t_len), BF16), vt_spec(nkv))]
            q, k, vt = _proj(h, wqkv, ex, outs, tm=TILE, name="gqa_qkv",
                             epilogue=functools.partial(_ep_gqa_qkv, nq=nq // LANE, nkv=nkv // LANE,
                                                        off=A_HEAD_DIM // 4,
                                                        qscale=A_HEAD_DIM ** -0.5 * LOG2E))
            o = _attention(q, k, vt, batch=batch, n_tiles=n_tiles, n_steps=A_HEADS // GQA_STEP_HEADS,
                           group=GQA_STEP_HEADS, dq=A_HEAD_DIM, dv=A_HEAD_DIM, q_per_kv=A_GROUP)
            a_list, w_out, prologue = [o], wo, None
        elif kind == 1:
            wx, wy, wo = lru_wx[slot].astype(BF16), lru_wy[slot].astype(BF16), lru_wo[slot].astype(BF16)
            u, y = _lru_in(h, jnp.concatenate([wx, wy], axis=1), lru_conv_w[slot], lru_conv_b[slot],
                           n_tiles=n_tiles, ngroups=2)
            hf, hb = _lru_scan(u, (0.5 * lru_ra_w[slot]).astype(BF16), (0.5 * lru_ix_w[slot]).astype(BF16),
                               0.5 * lru_ra_b[slot], 0.5 * lru_ix_b[slot], lru_lam[slot],
                               batch=batch, n_tiles=n_tiles)
            a_list, w_out, prologue = [hf, hb, y], wo, _pro_lru_gate
        else:
            nq = MLA_NOPE + MLA_ROPE
            wq_a = mla_wq_a[slot].astype(BF16)
            wq_b = mla_wq_b[slot].reshape(MLA_LORA, MLA_HEADS, nq)
            wq_b = jnp.concatenate([wq_b, jnp.zeros((MLA_LORA, MLA_HEADS, 2 * LANE - nq), F32)], axis=-1)
            wq_b = wq_b.reshape(MLA_LORA, MLA_HEADS * 2 * LANE).astype(BF16)
            wkv_a = jnp.concatenate([mla_wkv_a[slot], jnp.zeros((d, LANE - MLA_ROPE), F32)], axis=-1).astype(BF16)
            wkv_b = mla_wkv_b[slot].reshape(MLA_LORA, MLA_HEADS, MLA_NOPE + MLA_V)
            wkv_b = jnp.concatenate([wkv_b[:, :, :MLA_NOPE].reshape(MLA_LORA, -1),
                                     wkv_b[:, :, MLA_NOPE:].reshape(MLA_LORA, -1)], axis=-1).astype(BF16)
            wo = mla_wo[slot].astype(BF16)

            q, kfull, vt = _mla_proj(h, jnp.concatenate([wq_a, wkv_a], axis=1), wq_b, wkv_b,
                                     mla_q_a_g[slot], mla_kv_a_g[slot], tabs_mla,
                                     batch=batch, n_tiles=n_tiles, qscale=float(nq) ** -0.5 * LOG2E)
            o = _attention(q, kfull, vt, batch=batch, n_tiles=n_tiles, n_steps=MLA_HEADS // MLA_STEP_HEADS,
                           group=MLA_STEP_HEADS, dq=2 * LANE, dv=MLA_V, q_per_kv=1)
            a_list, w_out, prologue = [o], wo, None

        last = i == depth - 1
        xs, h2 = _mix_out(a_list, w_out, xs, mod3(m_i[:, 2], m_i[:, 4], m_i[:, 3]), ln_g[i, 0], ln_b[i, 0],
                          ngroups=1 if last else 2, out_rows=batch * n_lat if last else rows,
                          row_map=lat_tile if last else (lambda t: t), prologue=prologue, alpha=alpha)
        if not last:
            nxt = mod3(m_i[:, 5], mods[i + 1][:, 1], mods[i + 1][:, 0])
        else:
            zero = jnp.zeros_like(m_i[:, 5])
            nxt = mod3(m_i[:, 5], zero, zero)
            nxt = nxt.reshape(batch, n_tiles, 3, d)[:, 1:].reshape(batch * (n_tiles - 1), 3, d)
        xs, h = _mlp(h2, w1_all, w2_all, i, xs, nxt, ln_g[i, 1], ln_b[i, 1], tm=2 * TILE, tf=1024,
                     alpha=alpha)

    return xs.reshape(batch, n_lat, d)
```
